```python
import math
import jax
import jax.numpy as jnp
from jax import lax
import numpy as np

D_MODEL = 1024
BATCH = 2
SEQ = 16384
DEPTH = 2

GRID_W = 64
CTX_LEN = 256
HEAD_DIM = 64
N_GROUP_HEADS = 4
NA_WIN_H = 8
NA_WIN_W = 16
MLA_Q_RANK = 256
MLA_KV_RANK = 128
MLA_NOPE_DIM = 64
MLA_ROPE_DIM = 32
MLA_V_DIM = 64
DIFF_QK_DIM = 32
DIFF_V_DIM = 64
GQA_KV_HEADS = 2
N_EXPERTS = 16
EC_CAPACITY = 2
D_FF_EXPERT = 2816
ROPE_THETA = 10000.0
NORM_EPS = 1e-6
Q_BLOCK = 128
N_MOD = 6

IN_SPLITS = (
    N_GROUP_HEADS * HEAD_DIM, N_GROUP_HEADS * HEAD_DIM, N_GROUP_HEADS * HEAD_DIM,
    MLA_Q_RANK, MLA_KV_RANK, MLA_ROPE_DIM,
    N_GROUP_HEADS * 2 * DIFF_QK_DIM, N_GROUP_HEADS * 2 * DIFF_QK_DIM, N_GROUP_HEADS * DIFF_V_DIM,
    N_GROUP_HEADS * HEAD_DIM, GQA_KV_HEADS * HEAD_DIM, GQA_KV_HEADS * HEAD_DIM,
)
D_IN = sum(IN_SPLITS)

kernel_name = 'hybrid_parallel_heads_ec_moe_diffusion'


def rms_norm(x, g):
    xf = x.astype(jnp.float32)
    y = xf * lax.rsqrt(jnp.mean(xf * xf, axis=-1, keepdims=True) + NORM_EPS)
    return (y * g.astype(jnp.float32)).astype(x.dtype)


def modulate(x, g, shift, scale):
    return rms_norm(x, g) * (1 + scale) + shift


def split_heads(t, n):
    return t.reshape(t.shape[:-1] + (n, t.shape[-1] // n))


def merge_heads(t):
    return t.reshape(t.shape[:-2] + (-1,))


def split_cols(t):
    pts, acc = [], 0
    for w in IN_SPLITS[:-1]:
        acc += w
        pts.append(acc)
    return jnp.split(t, pts, axis=-1)


def rope_1d(x, pos):
    n = x.shape[-1]
    inv = ROPE_THETA ** (-jnp.arange(0, n, 2, dtype=jnp.float32) / n)
    ang = pos.astype(jnp.float32)[:, None] * inv[None, :]
    bshape = (x.shape[1],) + (1,) * (x.ndim - 3) + (n // 2,)
    cos, sin = jnp.cos(ang).reshape(bshape), jnp.sin(ang).reshape(bshape)
    xf = x.astype(jnp.float32)
    x1, x2 = xf[..., : n // 2], xf[..., n // 2:]
    return jnp.concatenate([x1 * cos - x2 * sin, x1 * sin + x2 * cos], axis=-1).astype(x.dtype)


def rope_2d(x, row, col):
    half = x.shape[-1] // 2
    return jnp.concatenate([rope_1d(x[..., :half], row), rope_1d(x[..., half:], col)], axis=-1)


def attend(q, k, v):
    B, Sq, H, dk = q.shape
    scale = dk ** -0.5
    nb = Sq // Q_BLOCK
    qb = jnp.swapaxes(q.reshape(B, nb, Q_BLOCK, H, dk), 0, 1)

    def one_block(qblk):
        s = jnp.einsum('bqhd,bkhd->bhqk', qblk, k, preferred_element_type=jnp.float32) * scale
        p = jax.nn.softmax(s, axis=-1).astype(v.dtype)
        return jnp.einsum('bhqk,bkhd->bqhd', p, v)

    out = lax.map(one_block, qb)
    return jnp.swapaxes(out, 0, 1).reshape(B, Sq, H, v.shape[-1])


def neighbourhood_attend(q, k, v, kc, vc, rpb, rows):
    B, S, H, d = q.shape
    kh = min(NA_WIN_H, rows)
    n_loc = kh * NA_WIN_W
    scale = d ** -0.5
    qg = q.reshape(B, rows, GRID_W, H, d)
    kg = k.reshape(B, rows, GRID_W, H, d)
    vg = v.reshape(B, rows, GRID_W, H, d)
    cols = jnp.arange(GRID_W)
    col_idx = jnp.clip(cols - NA_WIN_W // 2, 0, GRID_W - NA_WIN_W)[:, None] + jnp.arange(NA_WIN_W)[None, :]
    col_bias_idx = col_idx - cols[:, None] + (NA_WIN_W - 1)

    def one_row(r):
        r0 = jnp.clip(r - kh // 2, 0, rows - kh)
        kr = lax.dynamic_slice_in_dim(kg, r0, kh, axis=1)[:, :, col_idx]
        vr = lax.dynamic_slice_in_dim(vg, r0, kh, axis=1)[:, :, col_idx]
        qr = lax.dynamic_index_in_dim(qg, r, axis=1, keepdims=False)
        row_bias_idx = r0 + jnp.arange(kh) - r + (NA_WIN_H - 1)
        bias = rpb[:, row_bias_idx[None, :, None], col_bias_idx[:, None, :]].astype(jnp.float32)
        s_loc = jnp.einsum('bqhd,bjqwhd->bhqjw', qr, kr, preferred_element_type=jnp.float32) * scale + bias[None]
        s_ctx = jnp.einsum('bqhd,bkhd->bhqk', qr, kc, preferred_element_type=jnp.float32) * scale
        p = jax.nn.softmax(jnp.concatenate([s_loc.reshape(B, H, GRID_W, n_loc), s_ctx], axis=-1), axis=-1).astype(v.dtype)
        p_loc = p[..., :n_loc].reshape(B, H, GRID_W, kh, NA_WIN_W)
        return (jnp.einsum('bhqjw,bjqwhd->bqhd', p_loc, vr)
                + jnp.einsum('bhqk,bkhd->bqhd', p[..., n_loc:], vc))

    out = lax.map(one_row, jnp.arange(rows))
    return jnp.swapaxes(out, 0, 1).reshape(B, S, H, d)


def na_mixer(p, pc, g_q, g_k, rpb, rows, need_ctx):
    q = rms_norm(split_heads(p[0], N_GROUP_HEADS), g_q)
    k = rms_norm(split_heads(p[1], N_GROUP_HEADS), g_k)
    v = split_heads(p[2], N_GROUP_HEADS)
    kc = rms_norm(split_heads(pc[1], N_GROUP_HEADS), g_k)
    vc = split_heads(pc[2], N_GROUP_HEADS)
    o = merge_heads(neighbourhood_attend(q, k, v, kc, vc, rpb, rows))
    oc = None
    if need_ctx:
        qc = rms_norm(split_heads(pc[0], N_GROUP_HEADS), g_q)
        oc = merge_heads(attend(qc, kc, vc))
    return o, oc


def mla_q(p_cq, g_cq, w_uq, g_q, pos):
    q = rms_norm(split_heads(rms_norm(p_cq, g_cq) @ w_uq, N_GROUP_HEADS), g_q)
    if pos is None:
        return q
    return jnp.concatenate([q[..., :MLA_NOPE_DIM], rope_2d(q[..., MLA_NOPE_DIM:], *pos)], axis=-1)


def mla_kv(p_ckv, p_kr, g_ckv, w_ukv, g_kn, g_kr, pos):
    kv = split_heads(rms_norm(p_ckv, g_ckv) @ w_ukv, N_GROUP_HEADS)
    k_nope = rms_norm(kv[..., :MLA_NOPE_DIM], g_kn)
    v = kv[..., MLA_NOPE_DIM:]
    k_rope = rms_norm(p_kr, g_kr)[:, :, None, :]
    if pos is not None:
        k_rope = rope_2d(k_rope, *pos)
    k = jnp.concatenate([k_nope, jnp.broadcast_to(k_rope, k_nope.shape[:-1] + (MLA_ROPE_DIM,))], axis=-1)
    return k, v


def mla_mixer(p, pc, g_cq, w_uq, g_q, g_ckv, w_ukv, g_kn, g_kr, pos, need_ctx):
    q = mla_q(p[0], g_cq, w_uq, g_q, pos)
    k, v = mla_kv(p[1], p[2], g_ckv, w_ukv, g_kn, g_kr, pos)
    kc, vc = mla_kv(pc[1], pc[2], g_ckv, w_ukv, g_kn, g_kr, None)
    o = merge_heads(attend(q, jnp.concatenate([kc, k], axis=1), jnp.concatenate([vc, v], axis=1)))
    oc = None
    if need_ctx:
        qc = mla_q(pc[0], g_cq, w_uq, g_q, None)
        oc = merge_heads(attend(qc, kc, vc))
    return o, oc


def diff_qk(t, g, pos):
    t = rms_norm(t.reshape(t.shape[:-1] + (N_GROUP_HEADS, 2, DIFF_QK_DIM)), g)
    return t if pos is None else rope_2d(t, *pos)


def diff_attend(q, k, v, lam, lam_init, g_sub):
    a1 = attend(q[..., 0, :], k[..., 0, :], v)
    a2 = attend(q[..., 1, :], k[..., 1, :], v)
    o = a1 - lam.astype(a1.dtype) * a2
    return merge_heads(rms_norm(o, g_sub) * (1 - lam_init))


def diff_mixer(p, pc, g_q, g_k, lam_p, g_sub, lam_init, pos, need_ctx):
    lp = lam_p.astype(jnp.float32)
    lam = jnp.exp(jnp.sum(lp[0] * lp[1])) - jnp.exp(jnp.sum(lp[2] * lp[3])) + lam_init
    q, k = diff_qk(p[0], g_q, pos), diff_qk(p[1], g_k, pos)
    v = split_heads(p[2], N_GROUP_HEADS)
    kc, vc = diff_qk(pc[1], g_k, None), split_heads(pc[2], N_GROUP_HEADS)
    o = diff_attend(q, jnp.concatenate([kc, k], axis=1), jnp.concatenate([vc, v], axis=1), lam, lam_init, g_sub)
    oc = None
    if need_ctx:
        oc = diff_attend(diff_qk(pc[0], g_q, None), kc, vc, lam, lam_init, g_sub)
    return o, oc


def gqa_kv(pk, pv, g_k, pos):
    rep = N_GROUP_HEADS // GQA_KV_HEADS
    k = rms_norm(split_heads(pk, GQA_KV_HEADS), g_k)
    if pos is not None:
        k = rope_2d(k, *pos)
    return jnp.repeat(k, rep, axis=2), jnp.repeat(split_heads(pv, GQA_KV_HEADS), rep, axis=2)


def gqa_q(pq, g_q, pos):
    q = rms_norm(split_heads(pq, N_GROUP_HEADS), g_q)
    return q if pos is None else rope_2d(q, *pos)


def gqa_mixer(p, pc, g_q, g_k, pos, need_ctx):
    q = gqa_q(p[0], g_q, pos)
    k, v = gqa_kv(p[1], p[2], g_k, pos)
    kc, vc = gqa_kv(pc[1], pc[2], g_k, None)
    o = merge_heads(attend(q, jnp.concatenate([kc, k], axis=1), jnp.concatenate([vc, v], axis=1)))
    oc = merge_heads(attend(gqa_q(pc[0], g_q, None), kc, vc)) if need_ctx else None
    return o, oc


def swiglu(xe, wg, wu, wd):
    return (jax.nn.silu(xe @ wg) * (xe @ wu)) @ wd


def expert_choice_ffn(h, w_router, w_gate, w_up, w_down):
    B, n, D = h.shape
    cap = EC_CAPACITY * n // N_EXPERTS
    logits = jnp.einsum('bnd,de->ben', h, w_router, preferred_element_type=jnp.float32)
    aff = jax.nn.softmax(logits, axis=1)
    gate, idx = lax.top_k(aff, cap)
    xe = jax.vmap(lambda hb, ib: hb[ib])(h, idx)
    ye = lax.map(lambda a: swiglu(*a), (jnp.swapaxes(xe, 0, 1), w_gate, w_up, w_down))
    ye = jnp.swapaxes(ye, 0, 1) * gate[..., None].astype(h.dtype)

    def combine(ib, yb):
        return jnp.zeros((n, D), yb.dtype).at[ib.reshape(-1)].add(yb.reshape(-1, D))

    return jax.vmap(combine)(idx, ye)


def setup_inputs(seed: int = 0) -> dict:
    key = jax.random.key(seed)
    ks = iter(jax.random.split(key, 32))
    L, D, H = DEPTH, D_MODEL, N_GROUP_HEADS

    def nrm(shape, s):
        return jax.random.normal(next(ks), shape, jnp.float32) * s

    def gain(shape):
        return 1.0 + nrm(shape, 0.02)

    return {
        'x': nrm((BATCH, SEQ, D), 1.0),
        'c': nrm((BATCH, D), 1.0),
        'ctx': nrm((BATCH, CTX_LEN, D), 1.0),
        'c_ctx': nrm((D,), 1.0),
        'w_mod': nrm((L, D, N_MOD * D), 0.5 * D ** -0.5),
        'b_mod': nrm((L, N_MOD * D), 0.01),
        'g_attn': gain((L, D)),
        'g_ffn': gain((L, D)),
        'w_in': nrm((L, D, D_IN), D ** -0.5),
        'g_na_q': gain((L, HEAD_DIM)),
        'g_na_k': gain((L, HEAD_DIM)),
        'na_rpb': nrm((L, H, 2 * NA_WIN_H - 1, 2 * NA_WIN_W - 1), 0.1),
        'g_mla_cq': gain((L, MLA_Q_RANK)),
        'w_mla_uq': nrm((L, MLA_Q_RANK, H * (MLA_NOPE_DIM + MLA_ROPE_DIM)), MLA_Q_RANK ** -0.5),
        'g_mla_q': gain((L, MLA_NOPE_DIM + MLA_ROPE_DIM)),
        'g_mla_ckv': gain((L, MLA_KV_RANK)),
        'w_mla_ukv': nrm((L, MLA_KV_RANK, H * (MLA_NOPE_DIM + MLA_V_DIM)), MLA_KV_RANK ** -0.5),
        'g_mla_k_nope': gain((L, MLA_NOPE_DIM)),
        'g_mla_k_rope': gain((L, MLA_ROPE_DIM)),
        'g_diff_q': gain((L, 2, DIFF_QK_DIM)),
        'g_diff_k': gain((L, 2, DIFF_QK_DIM)),
        'diff_lambda': nrm((L, 4, DIFF_QK_DIM), 0.1),
        'g_diff_sub': gain((L, DIFF_V_DIM)),
        'g_gqa_q': gain((L, HEAD_DIM)),
        'g_gqa_k': gain((L, HEAD_DIM)),
        'w_out': nrm((L, D, D), D ** -0.5),
        'w_router': nrm((L, D, N_EXPERTS), D ** -0.5),
        'w_gate': nrm((L, N_EXPERTS, D, D_FF_EXPERT), D ** -0.5),
        'w_up': nrm((L, N_EXPERTS, D, D_FF_EXPERT), D ** -0.5),
        'w_down': nrm((L, N_EXPERTS, D_FF_EXPERT, D), D_FF_EXPERT ** -0.5),
    }


def reference(x, c, ctx, c_ctx, w_mod, b_mod, g_attn, g_ffn, w_in, g_na_q, g_na_k, na_rpb,
              g_mla_cq, w_mla_uq, g_mla_q, g_mla_ckv, w_mla_ukv, g_mla_k_nope, g_mla_k_rope,
              g_diff_q, g_diff_k, diff_lambda, g_diff_sub, g_gqa_q, g_gqa_k, w_out,
              w_router, w_gate, w_up, w_down):
    B, S, D = x.shape
    rows = S // GRID_W
    t = jnp.arange(S, dtype=jnp.int32)
    pos = (t // GRID_W, t % GRID_W)
    xc = ctx
    silu_c = jax.nn.silu(c)
    silu_cc = jax.nn.silu(c_ctx)[None]
    for l in range(DEPTH):
        need_ctx = l < DEPTH - 1
        lam_init = 0.8 - 0.6 * math.exp(-0.3 * l)
        mod = jnp.split((silu_c @ w_mod[l] + b_mod[l])[:, None, :], N_MOD, axis=-1)
        mod_c = jnp.split((silu_cc @ w_mod[l] + b_mod[l])[:, None, :], N_MOD, axis=-1)

        p = split_cols(modulate(x, g_attn[l], mod[0], mod[1]) @ w_in[l])
        pc = split_cols(modulate(xc, g_attn[l], mod_c[0], mod_c[1]) @ w_in[l])
        o_na, oc_na = na_mixer(p[0:3], pc[0:3], g_na_q[l], g_na_k[l], na_rpb[l], rows, need_ctx)
        o_mla, oc_mla = mla_mixer(p[3:6], pc[3:6], g_mla_cq[l], w_mla_uq[l], g_mla_q[l], g_mla_ckv[l],
                                  w_mla_ukv[l], g_mla_k_nope[l], g_mla_k_rope[l], pos, need_ctx)
        o_diff, oc_diff = diff_mixer(p[6:9], pc[6:9], g_diff_q[l], g_diff_k[l], diff_lambda[l],
                                     g_diff_sub[l], lam_init, pos, need_ctx)
        o_gqa, oc_gqa = gqa_mixer(p[9:12], pc[9:12], g_gqa_q[l], g_gqa_k[l], pos, need_ctx)
        x = x + mod[2] * (jnp.concatenate([o_na, o_mla, o_diff, o_gqa], axis=-1) @ w_out[l])
        if need_ctx:
            xc = xc + mod_c[2] * (jnp.concatenate([oc_na, oc_mla, oc_diff, oc_gqa], axis=-1) @ w_out[l])

        x = x + mod[5] * expert_choice_ffn(modulate(x, g_ffn[l], mod[3], mod[4]),
                                           w_router[l], w_gate[l], w_up[l], w_down[l])
        if need_ctx:
            xc = xc + mod_c[5] * expert_choice_ffn(modulate(xc, g_ffn[l], mod_c[3], mod_c[4]),
                                                   w_router[l], w_gate[l], w_up[l], w_down[l])
    return x
```

```python
import functools
import math

import numpy as np
import jax
import jax.numpy as jnp
from jax import lax
from jax.experimental import pallas as pl
from jax.experimental.pallas import tpu as pltpu

D_MODEL = 1024
GRID_W = 64
HEAD_DIM = 64
N_HEADS = 4
NA_WIN_H = 8
NA_WIN_W = 16
MLA_Q_RANK = 256
MLA_KV_RANK = 128
MLA_NOPE_DIM = 64
MLA_ROPE_DIM = 32
MLA_V_DIM = 64
DIFF_QK_DIM = 32
N_EXPERTS = 16
EC_CAPACITY = 2
D_FF = 2816
ROPE_THETA = 10000.0
NORM_EPS = 1e-6
N_MOD = 6
DEPTH = 2

LANES = 128
P_COLS = 2560
VMEM_LIMIT = 56 * 1024 * 1024
NEG_BIG = -1e30

F32 = jnp.float32
BF16 = jnp.bfloat16


def _cparams(sem):
    return pltpu.CompilerParams(dimension_semantics=sem, vmem_limit_bytes=VMEM_LIMIT)


def _split(a):
    hi = a.astype(BF16)
    lo = (a - hi.astype(F32)).astype(BF16)
    return hi, lo


def _dot(a, b):
    return jnp.dot(a, b, preferred_element_type=F32)


def _dot_nt(a, b):
    return lax.dot_general(a, b, (((1,), (1,)), ((), ())), preferred_element_type=F32)


def _dot3(a, b):
    ah, al = _split(a)
    bh, bl = _split(b)
    return _dot(ah, bh) + _dot(ah, bl) + _dot(al, bh)


def _dot3_nt(a, b):
    ah, al = _split(a)
    bh, bl = _split(b)
    return _dot_nt(ah, bh) + _dot_nt(ah, bl) + _dot_nt(al, bh)


def _silu(v):
    return v * jax.nn.sigmoid(v)


def _mod_kernel(c_ref, w_ref, b_ref, o_ref):
    o_ref[...] = _dot3(_silu(c_ref[...]), w_ref[...]) + b_ref[...]


def mod_vectors(c_rows, w_mod, b_mod):
    m, d = c_rows.shape
    n = w_mod.shape[1]
    tn = 1536
    return pl.pallas_call(
        _mod_kernel,
        grid=(n // tn,),
        in_specs=[pl.BlockSpec((m, d), lambda j: (0, 0)),
                  pl.BlockSpec((d, tn), lambda j: (0, j)),
                  pl.BlockSpec((1, tn), lambda j: (0, j))],
        out_specs=pl.BlockSpec((m, tn), lambda j: (0, j)),
        out_shape=jax.ShapeDtypeStruct((m, n), F32),
        compiler_params=_cparams(("arbitrary",)),
        name="mod_vectors",
    )(c_rows, w_mod, b_mod.reshape(1, n))


def _ln_proj_kernel(x_ref, g_ref, sh_ref, sc_ref, w_ref, o_ref):
    x = x_ref[0]
    ms = jnp.mean(x * x, axis=-1, keepdims=True)
    y = x * lax.rsqrt(ms + NORM_EPS) * g_ref[...]
    h = y * (1.0 + sc_ref[0]) + sh_ref[0]
    o_ref[0] = _dot(h.astype(BF16), w_ref[...])


def ln_mod_proj(x, g, shift, scale, w):
    b, n, d = x.shape
    ncol = w.shape[1]
    tm = min(512, n)
    return pl.pallas_call(
        _ln_proj_kernel,
        grid=(b, n // tm),
        in_specs=[pl.BlockSpec((1, tm, d), lambda i, j: (i, j, 0)),
                  pl.BlockSpec((1, d), lambda i, j: (0, 0)),
                  pl.BlockSpec((1, 1, d), lambda i, j: (i, 0, 0)),
                  pl.BlockSpec((1, 1, d), lambda i, j: (i, 0, 0)),
                  pl.BlockSpec((d, ncol), lambda i, j: (0, 0))],
        out_specs=pl.BlockSpec((1, tm, ncol), lambda i, j: (i, j, 0)),
        out_shape=jax.ShapeDtypeStruct((b, n, ncol), F32),
        compiler_params=_cparams(("parallel", "arbitrary")),
        name="ln_mod_proj",
    )(x, g.reshape(1, d), shift, scale, w)


P_NAQ, P_NAK, P_NAV, P_CQ, P_CKV = 0, 256, 512, 768, 1024
P_DQ, P_DK, P_DV, P_GQ, P_GK, P_GV, P_KR = 1152, 1408, 1664, 1920, 2176, 2304, 2432
(G_NAQ, G_NAK, G_CQ, G_MQ, G_CKV, G_KN, G_KR, G_DQ, G_DK, G_GQ, G_GK) = range(11)
M_ONES, M_SEG64, M_SEG32 = 0, 1, 2
T_MLA, T_DIFF, T_GQA = 0, 3, 6


def _seg_norm(x, mat, inv_n, g):
    sq = x * x
    hi, lo = _split(sq)
    ms = (_dot(hi, mat) + _dot(lo, mat)) * inv_n
    return x * lax.rsqrt(ms + NORM_EPS) * g


def _rope(x, tab_ref, t0, half):
    c, s1, s2 = tab_ref[t0], tab_ref[t0 + 1], tab_ref[t0 + 2]
    return (x * c + pltpu.roll(x, LANES - half, axis=1) * s1
            + pltpu.roll(x, half, axis=1) * s2)


def _prep_kernel(p_ref, tab_ref, gain_ref, mat_ref, wuq_ref, wuk_ref, wuv_ref,
                 naq_ref, nak_ref, nav_ref, mq_ref, mk_ref, mv_ref,
                 dq_ref, dk_ref, dv_ref, gq_ref, gk_ref, gv_ref):
    lane = lax.broadcasted_iota(jnp.int32, (1, LANES), 1)
    lo_half = lane < 64
    ones_m, m64, m32 = mat_ref[M_ONES], mat_ref[M_SEG64], mat_ref[M_SEG32]

    def slab(off, i):
        return p_ref[0, :, off + i * LANES: off + (i + 1) * LANES]

    def gain(row, i):
        return gain_ref[row:row + 1, i * LANES:(i + 1) * LANES]

    s_na = HEAD_DIM ** -0.5
    for i in range(2):
        q = _seg_norm(slab(P_NAQ, i), m64, 1.0 / 64, gain(G_NAQ, i)) * s_na
        naq_ref[0, :, (2 * i) * LANES:(2 * i + 1) * LANES] = jnp.where(lo_half, q, 0.0).astype(BF16)
        naq_ref[0, :, (2 * i + 1) * LANES:(2 * i + 2) * LANES] = jnp.where(lo_half, 0.0, q).astype(BF16)
        k = _seg_norm(slab(P_NAK, i), m64, 1.0 / 64, gain(G_NAK, i))
        nak_ref[0, :, i * LANES:(i + 1) * LANES] = k.astype(BF16)
        nav_ref[0, :, i * LANES:(i + 1) * LANES] = slab(P_NAV, i).astype(BF16)

    cq = p_ref[0, :, P_CQ:P_CQ + MLA_Q_RANK]
    cq = cq * lax.rsqrt(jnp.mean(cq * cq, axis=-1, keepdims=True) + NORM_EPS) * gain_ref[G_CQ:G_CQ + 1, :MLA_Q_RANK]
    uq = _dot(cq.astype(BF16), wuq_ref[...])
    s_mla = (MLA_NOPE_DIM + MLA_ROPE_DIM) ** -0.5
    ckv = p_ref[0, :, P_CKV:P_CKV + MLA_KV_RANK]
    ckv = ckv * lax.rsqrt(jnp.mean(ckv * ckv, axis=-1, keepdims=True) + NORM_EPS) * gain_ref[G_CKV:G_CKV + 1, :MLA_KV_RANK]
    ckv_b = ckv.astype(BF16)
    uk = _dot(ckv_b, wuk_ref[...])
    mv_ref[0] = _dot(ckv_b, wuv_ref[...]).astype(BF16)
    kr = _seg_norm(slab(P_KR, 0), ones_m, 1.0 / MLA_ROPE_DIM, gain(G_KR, 0))
    kr = _rope(kr, tab_ref, T_MLA, 8)
    for h in range(N_HEADS):
        q = _seg_norm(uq[:, h * LANES:(h + 1) * LANES], ones_m, 1.0 / (MLA_NOPE_DIM + MLA_ROPE_DIM), gain(G_MQ, h))
        q = _rope(q, tab_ref, T_MLA, 8) * s_mla
        mq_ref[0, :, h * LANES:(h + 1) * LANES] = q.astype(BF16)
        kn = _seg_norm(uk[:, h * LANES:(h + 1) * LANES], ones_m, 1.0 / MLA_NOPE_DIM, gain(G_KN, h))
        mk_ref[0, :, h * LANES:(h + 1) * LANES] = (kn + kr).astype(BF16)

    s_d = DIFF_QK_DIM ** -0.5
    seg = lane >> 5
    for i in range(2):
        q = _seg_norm(slab(P_DQ, i), m32, 1.0 / 32, gain(G_DQ, i))
        q = _rope(q, tab_ref, T_DIFF, 8) * s_d
        for j in range(4):
            dq_ref[0, :, (4 * i + j) * LANES:(4 * i + j + 1) * LANES] = jnp.where(seg == j, q, 0.0).astype(BF16)
        k = _seg_norm(slab(P_DK, i), m32, 1.0 / 32, gain(G_DK, i))
        dk_ref[0, :, i * LANES:(i + 1) * LANES] = _rope(k, tab_ref, T_DIFF, 8).astype(BF16)
        dv_ref[0, :, i * LANES:(i + 1) * LANES] = slab(P_DV, i).astype(BF16)

    s_g = HEAD_DIM ** -0.5
    for i in range(2):
        q = _seg_norm(slab(P_GQ, i), m64, 1.0 / 64, gain(G_GQ, i))
        q = _rope(q, tab_ref, T_GQA, 16) * s_g
        gq_ref[0, :, (2 * i) * LANES:(2 * i + 1) * LANES] = jnp.where(lo_half, q, 0.0).astype(BF16)
        gq_ref[0, :, (2 * i + 1) * LANES:(2 * i + 2) * LANES] = jnp.where(lo_half, 0.0, q).astype(BF16)
    k = _seg_norm(slab(P_GK, 0), m64, 1.0 / 64, gain(G_GK, 0))
    gk_ref[0] = _rope(k, tab_ref, T_GQA, 16).astype(BF16)
    gv_ref[0] = slab(P_GV, 0).astype(BF16)


PREP_WIDTHS = (512, 256, 256, 512, 512, 256, 1024, 256, 256, 512, 128, 128)


def prep(p, tabs, gains, mats, wuq, wuk, wuv):
    b, n, _ = p.shape
    tm = min(512, n)
    const2 = lambda i, j: (0, 0)
    return pl.pallas_call(
        _prep_kernel,
        grid=(b, n // tm),
        in_specs=[pl.BlockSpec((1, tm, P_COLS), lambda i, j: (i, j, 0)),
                  pl.BlockSpec((9, tm, LANES), lambda i, j: (0, j, 0)),
                  pl.BlockSpec(gains.shape, const2),
                  pl.BlockSpec(mats.shape, lambda i, j: (0, 0, 0)),
                  pl.BlockSpec(wuq.shape, const2),
                  pl.BlockSpec(wuk.shape, const2),
                  pl.BlockSpec(wuv.shape, const2)],
        out_specs=[pl.BlockSpec((1, tm, w), lambda i, j: (i, j, 0)) for w in PREP_WIDTHS],
        out_shape=[jax.ShapeDtypeStruct((b, n, w), BF16) for w in PREP_WIDTHS],
        compiler_params=_cparams(("parallel", "arbitrary")),
        name="prep",
    )(p, tabs, gains, mats, wuq, wuk, wuv)


def _flash_kernel(q_ref, k_ref, v_ref, o_ref, m_ref, l_ref, acc_ref, *, heads, n_out):
    ki = pl.program_id(2)

    @pl.when(ki == 0)
    def _():
        m_ref[...] = jnp.full(m_ref.shape, NEG_BIG, F32)
        l_ref[...] = jnp.zeros(l_ref.shape, F32)
        acc_ref[...] = jnp.zeros(acc_ref.shape, F32)

    lo_half = lax.broadcasted_iota(jnp.int32, (1, LANES), 1) < 64
    for o in range(n_out):
        pair = [h for h in range(len(heads)) if heads[h][3] == o]
        pv, alpha = [], []
        for h in pair:
            qs, ks, vs, _ = heads[h]
            q = q_ref[0, :, qs * LANES:(qs + 1) * LANES]
            k = k_ref[0, :, ks * LANES:(ks + 1) * LANES]
            v = v_ref[0, :, vs * LANES:(vs + 1) * LANES]
            s = _dot_nt(q, k)
            m_prev = m_ref[h]
            m_new = jnp.maximum(m_prev, jnp.max(s, axis=-1, keepdims=True))
            a = jnp.exp(m_prev - m_new)
            p = jnp.exp(s - m_new)
            l_ref[h] = a * l_ref[h] + jnp.sum(p, axis=-1, keepdims=True)
            m_ref[h] = m_new
            pv.append(_dot(p.astype(BF16), v))
            alpha.append(a)
        acc_ref[o] = acc_ref[o] * jnp.where(lo_half, alpha[0], alpha[1]) + jnp.where(lo_half, pv[0], pv[1])

    @pl.when(ki == pl.num_programs(2) - 1)
    def _():
        for o in range(n_out):
            pair = [h for h in range(len(heads)) if heads[h][3] == o]
            l = jnp.where(lo_half, l_ref[pair[0]], l_ref[pair[1]])
            o_ref[0, :, o * LANES:(o + 1) * LANES] = acc_ref[o] / l


def _pick_tk(n_keys):
    best = 256
    for t in range(256, 1537, 256):
        if n_keys % t == 0:
            best = t
    return best


def flash_attention(q, k, v, heads):
    b, s, qw = q.shape
    nk = k.shape[1]
    n_out = max(h[3] for h in heads) + 1
    tq = min(512, s)
    tk = _pick_tk(nk)
    kern = functools.partial(_flash_kernel, heads=heads, n_out=n_out)
    return pl.pallas_call(
        kern,
        grid=(b, s // tq, nk // tk),
        in_specs=[pl.BlockSpec((1, tq, qw), lambda i, j, t: (i, j, 0)),
                  pl.BlockSpec((1, tk, k.shape[2]), lambda i, j, t: (i, t, 0)),
                  pl.BlockSpec((1, tk, v.shape[2]), lambda i, j, t: (i, t, 0))],
        out_specs=pl.BlockSpec((1, tq, n_out * LANES), lambda i, j, t: (i, j, 0)),
        out_shape=jax.ShapeDtypeStruct((b, s, n_out * LANES), F32),
        scratch_shapes=[pltpu.VMEM((len(heads), tq, 1), F32),
                        pltpu.VMEM((len(heads), tq, 1), F32),
                        pltpu.VMEM((n_out, tq, LANES), F32)],
        compiler_params=_cparams(("parallel", "parallel", "arbitrary")),
        name="flash_attention",
    )(q, k, v)


HEADS_PAIRED = tuple((h, h // 2, h // 2, h // 2) for h in range(4))
HEADS_MLA = tuple((h, h, h // 2, h // 2) for h in range(4))
HEADS_GQA = tuple((h, 0, 0, h // 2) for h in range(4))
HEADS_DIFF = tuple((j, j // 4, j // 4, (j % 2) * 2 + j // 4) for j in range(8))


NA_QROWS = 4
NA_TQ = NA_QROWS * GRID_W


def _na_kernel(q_ref, kp_ref, kc_ref, kn_ref, vp_ref, vc_ref, vn_ref, kx_ref, vx_ref, bias_ref, o_ref, *, rows):
    j = pl.program_id(1)
    kh = min(NA_WIN_H, rows)
    qi = lax.broadcasted_iota(jnp.int32, (NA_TQ, 1), 0)
    ki = lax.broadcasted_iota(jnp.int32, (1, NA_TQ), 1)
    wshift = GRID_W.bit_length() - 1
    r = j * NA_QROWS + (qi >> wshift)
    c = qi & (GRID_W - 1)
    r0 = jnp.clip(r - kh // 2, 0, rows - kh)
    c0 = jnp.clip(c - NA_WIN_W // 2, 0, GRID_W - NA_WIN_W)
    kcol = ki & (GRID_W - 1)
    col_ok = (kcol >= c0) & (kcol < c0 + NA_WIN_W)
    masks = []
    for d in (-1, 0, 1):
        kr = (j + d) * NA_QROWS + (ki >> wshift)
        masks.append(col_ok & (kr >= r0) & (kr < r0 + kh))
    lo_half = lax.broadcasted_iota(jnp.int32, (1, LANES), 1) < 64
    for o in range(2):
        k_loc = [kp_ref[0, :, o * LANES:(o + 1) * LANES], kc_ref[0, :, o * LANES:(o + 1) * LANES],
                 kn_ref[0, :, o * LANES:(o + 1) * LANES]]
        v_all = jnp.concatenate([vp_ref[0, :, o * LANES:(o + 1) * LANES], vc_ref[0, :, o * LANES:(o + 1) * LANES],
                                 vn_ref[0, :, o * LANES:(o + 1) * LANES], vx_ref[0, :, o * LANES:(o + 1) * LANES]],
                                axis=0)
        kx = kx_ref[0, :, o * LANES:(o + 1) * LANES]
        res = []
        for h in (2 * o, 2 * o + 1):
            q = q_ref[0, :, h * LANES:(h + 1) * LANES]
            parts = [jnp.where(masks[d], _dot_nt(q, k_loc[d]) + bias_ref[h, d], NEG_BIG) for d in range(3)]
            parts.append(_dot_nt(q, kx))
            s = jnp.concatenate(parts, axis=1)
            m = jnp.max(s, axis=-1, keepdims=True)
            p = jnp.exp(s - m)
            l = jnp.sum(p, axis=-1, keepdims=True)
            res.append(_dot(p.astype(BF16), v_all) / l)
        o_ref[0, :, o * LANES:(o + 1) * LANES] = jnp.where(lo_half, res[0], res[1])


def na_attention(q, k, v, kx, vx, bias, rows):
    b, s, _ = q.shape
    nblk = rows // NA_QROWS
    nx = kx.shape[1]
    kern = functools.partial(_na_kernel, rows=rows)
    prev = lambda i, j: (i, jnp.maximum(j - 1, 0), 0)
    cur = lambda i, j: (i, j, 0)
    nxt = lambda i, j: (i, jnp.minimum(j + 1, nblk - 1), 0)
    kv_spec = lambda f: pl.BlockSpec((1, NA_TQ, 2 * LANES), f)
    return pl.pallas_call(
        kern,
        grid=(b, nblk),
        in_specs=[pl.BlockSpec((1, NA_TQ, 4 * LANES), cur),
                  kv_spec(prev), kv_spec(cur), kv_spec(nxt),
                  kv_spec(prev), kv_spec(cur), kv_spec(nxt),
                  pl.BlockSpec((1, nx, 2 * LANES), lambda i, j: (i, 0, 0)),
                  pl.BlockSpec((1, nx, 2 * LANES), lambda i, j: (i, 0, 0)),
                  pl.BlockSpec(bias.shape, lambda i, j: (0, 0, 0, 0))],
        out_specs=pl.BlockSpec((1, NA_TQ, 2 * LANES), cur),
        out_shape=jax.ShapeDtypeStruct((b, s, 2 * LANES), F32),
        compiler_params=_cparams(("parallel", "arbitrary")),
        name="na_attention",
    )(q, k, k, k, v, v, v, kx, vx, bias)


ROUTE_T = 256


def _outproj_kernel(x_ref, ona_ref, omla_ref, od_ref, ogqa_ref, mod_ref, gsub_ref, gffn_ref, lam_ref, mat_ref,
                    wout_ref, wr_ref, xo_ref, h_ref, lg_ref, *, lam_init):
    lam = lam_ref[0, 0]
    m64 = mat_ref[M_SEG64]
    pieces = [ona_ref[0].astype(BF16), omla_ref[0].astype(BF16)]
    dsl = []
    for i in range(2):
        d = od_ref[0, :, i * LANES:(i + 1) * LANES] - lam * od_ref[0, :, (2 + i) * LANES:(3 + i) * LANES]
        d = _seg_norm(d, m64, 1.0 / 64, gsub_ref[...]) * (1.0 - lam_init)
        dsl.append(d.astype(BF16))
    pieces += dsl + [ogqa_ref[0].astype(BF16)]
    o = jnp.concatenate(pieces, axis=1)
    y = _dot(o, wout_ref[...])
    x = x_ref[0] + mod_ref[0, 2:3, :] * y
    xo_ref[0] = x
    ms = jnp.mean(x * x, axis=-1, keepdims=True)
    h = x * lax.rsqrt(ms + NORM_EPS) * gffn_ref[...]
    h = h * (1.0 + mod_ref[0, 4:5, :]) + mod_ref[0, 3:4, :]
    h_ref[0] = h.astype(BF16)
    lg = _dot3_nt(wr_ref[...], h)
    for t in range(lg.shape[1] // ROUTE_T):
        lg_ref[0, t] = lg[:, t * ROUTE_T:(t + 1) * ROUTE_T]


def outproj_mod_router(x, o_na, o_mla, o_diff, o_gqa, mod, g_sub_t, g_ffn, lam, mats, w_out, w_router_t, lam_init):
    b, n, d = x.shape
    tm = min(512, n)
    nt = tm // ROUTE_T
    kern = functools.partial(_outproj_kernel, lam_init=lam_init)
    tok = lambda w: pl.BlockSpec((1, tm, w), lambda i, j: (i, j, 0))
    c2 = lambda i, j: (0, 0)
    return pl.pallas_call(
        kern,
        grid=(b, n // tm),
        in_specs=[tok(d), tok(256), tok(256), tok(512), tok(256),
                  pl.BlockSpec((1, 8, d), lambda i, j: (i, 0, 0)),
                  pl.BlockSpec((1, LANES), c2), pl.BlockSpec((1, d), c2),
                  pl.BlockSpec(memory_space=pltpu.SMEM),
                  pl.BlockSpec(mats.shape, lambda i, j: (0, 0, 0)),
                  pl.BlockSpec((d, d), c2), pl.BlockSpec((N_EXPERTS, d), c2)],
        out_specs=[tok(d), tok(d),
                   pl.BlockSpec((1, nt, N_EXPERTS, ROUTE_T), lambda i, j: (i, j, 0, 0))],
        out_shape=[jax.ShapeDtypeStruct((b, n, d), F32), jax.ShapeDtypeStruct((b, n, d), BF16),
                   jax.ShapeDtypeStruct((b, n // ROUTE_T, N_EXPERTS, ROUTE_T), F32)],
        compiler_params=_cparams(("parallel", "arbitrary")),
        name="outproj_mod_router",
    )(x, o_na, o_mla, o_diff, o_gqa, mod, g_sub_t, g_ffn.reshape(1, d), lam, mats, w_out, w_router_t)


def _route_kernel(lg_ref, tri_ref, aff_ref, pos_ref, off_ref, *, cap):
    nb = lg_ref.shape[1]
    lg = lg_ref[0]
    mx = jnp.max(lg, axis=1, keepdims=True)
    ex = jnp.exp(lg - mx)
    aff = ex / jnp.sum(ex, axis=1, keepdims=True)
    aff_ref[0] = aff
    bits = lax.bitcast_convert_type(aff, jnp.int32)

    def count_ge(t):
        hit = jnp.where(bits >= t[None], 1.0, 0.0)
        return jnp.sum(jnp.sum(hit, axis=0), axis=1, keepdims=True)

    def bis(i, t):
        cand = t | (jnp.int32(1) << (30 - i))
        return jnp.where(count_ge(cand) >= float(cap), cand, t)

    thr = lax.fori_loop(0, 31, bis, jnp.zeros((N_EXPERTS, 1), jnp.int32))
    need = float(cap) - count_ge(thr + 1)
    tri = tri_ref[...]

    def blk(jb, carry):
        c_eq, c_pos = carry
        bb = lax.bitcast_convert_type(aff_ref[0, jb], jnp.int32)
        gt = bb > thr
        eq = bb == thr
        eq_before = _dot(jnp.where(eq, 1.0, 0.0).astype(BF16), tri) + c_eq
        sel = gt | (eq & (eq_before < need))
        sel_f = jnp.where(sel, 1.0, 0.0)
        before = _dot(sel_f.astype(BF16), tri) + c_pos
        pos_ref[0, jb] = jnp.where(sel, before, -1.0).astype(jnp.int32)
        off_ref[0, jb] = jnp.broadcast_to(c_pos, (N_EXPERTS, LANES)).astype(jnp.int32)
        c_eq = c_eq + jnp.sum(jnp.where(eq, 1.0, 0.0), axis=1, keepdims=True)
        c_pos = c_pos + jnp.sum(sel_f, axis=1, keepdims=True)
        return c_eq, c_pos

    zero = jnp.zeros((N_EXPERTS, 1), F32)
    lax.fori_loop(0, nb, blk, (zero, zero))


def route(logits, tri, cap):
    b, nb, e, t = logits.shape
    kern = functools.partial(_route_kernel, cap=cap)
    spec = pl.BlockSpec((1, nb, e, t), lambda i: (i, 0, 0, 0))
    return pl.pallas_call(
        kern,
        grid=(b,),
        in_specs=[spec, pl.BlockSpec(tri.shape, lambda i: (0, 0))],
        out_specs=[spec, spec, pl.BlockSpec((1, nb, e, LANES), lambda i: (i, 0, 0, 0))],
        out_shape=[jax.ShapeDtypeStruct((b, nb, e, t), F32), jax.ShapeDtypeStruct((b, nb, e, t), jnp.int32),
                   jax.ShapeDtypeStruct((b, nb, e, LANES), jnp.int32)],
        compiler_params=_cparams(("arbitrary",)),
        name="route",
    )(logits, tri)


def _gather_kernel(off_ref, h_ref, pos_ref, aff_ref, xe_ref, gs_ref, acc_ref, gacc_ref, *, st, n_tiles, nsub, nb):
    bi, e, ch = pl.program_id(0), pl.program_id(1), pl.program_id(2)

    @pl.when(ch == 0)
    def _():
        acc_ref[...] = jnp.zeros(acc_ref.shape, F32)
        gacc_ref[...] = jnp.zeros(gacc_ref.shape, F32)

    slot = lax.broadcasted_iota(jnp.int32, (st, ROUTE_T), 0)
    for jj in range(nsub):
        jb = ch * nsub + jj
        a = off_ref[(bi * N_EXPERTS + e) * nb + jb] // st
        prow = pos_ref[0, jj, pl.ds(e, 1), :]
        arow = aff_ref[0, jj, pl.ds(e, 1), :]
        hs = h_ref[0, jj * ROUTE_T:(jj + 1) * ROUTE_T, :]
        for w in range(2):
            tile = jnp.minimum(a + w, n_tiles - 1)
            hit = slot == (prow - (a + w) * st)
            base = pl.multiple_of(tile * st, st)
            acc_ref[pl.ds(base, st), :] += _dot(jnp.where(hit, 1.0, 0.0).astype(BF16), hs)
            gacc_ref[pl.ds(base, st), :] += jnp.sum(jnp.where(hit, arow, 0.0), axis=1, keepdims=True)

    @pl.when(ch == pl.num_programs(2) - 1)
    def _():
        xe_ref[0, 0] = acc_ref[...].astype(BF16)
        gs_ref[0, 0] = gacc_ref[...]


def moe_gather(offs, h, pos, aff, cap):
    b, n, d = h.shape
    nb = n // ROUTE_T
    st = min(ROUTE_T, cap)
    n_tiles = cap // st
    chunk = min(2048, n)
    nsub = chunk // ROUTE_T
    kern = functools.partial(_gather_kernel, st=st, n_tiles=n_tiles, nsub=nsub, nb=nb)
    rspec = pl.BlockSpec((1, nsub, N_EXPERTS, ROUTE_T), lambda i, e, c, off: (i, c, 0, 0))
    return pl.pallas_call(
        kern,
        grid_spec=pltpu.PrefetchScalarGridSpec(
            num_scalar_prefetch=1,
            grid=(b, N_EXPERTS, n // chunk),
            in_specs=[pl.BlockSpec((1, chunk, d), lambda i, e, c, off: (i, c, 0)), rspec, rspec],
            out_specs=[pl.BlockSpec((1, 1, cap, d), lambda i, e, c, off: (i, e, 0, 0)),
                       pl.BlockSpec((1, 1, cap, 1), lambda i, e, c, off: (i, e, 0, 0))],
            scratch_shapes=[pltpu.VMEM((cap, d), F32), pltpu.VMEM((cap, 1), F32)]),
        out_shape=[jax.ShapeDtypeStruct((b, N_EXPERTS, cap, d), BF16),
                   jax.ShapeDtypeStruct((b, N_EXPERTS, cap, 1), F32)],
        compiler_params=_cparams(("parallel", "parallel", "arbitrary")),
        name="moe_gather",
    )(offs, h, pos, aff)


FF_TILE = 256


def _ffn_kernel(x_ref, g_ref, wg_ref, wu_ref, wd_ref, y_ref, acc_ref):
    f = pl.program_id(2)

    @pl.when(f == 0)
    def _():
        acc_ref[...] = jnp.zeros(acc_ref.shape, F32)

    x = x_ref[0, 0]
    gate = _dot(x, wg_ref[0].astype(BF16))
    up = _dot(x, wu_ref[0].astype(BF16))
    hmid = (_silu(gate) * up).astype(BF16)
    acc_ref[...] += _dot(hmid, wd_ref[0].astype(BF16))

    @pl.when(f == pl.num_programs(2) - 1)
    def _():
        y_ref[0, 0] = (acc_ref[...] * g_ref[0, 0]).astype(BF16)


def moe_ffn(xe, gs, w_gate, w_up, w_down):
    b, e, cap, d = xe.shape
    dff = w_gate.shape[2]
    return pl.pallas_call(
        _ffn_kernel,
        grid=(e, b, dff // FF_TILE),
        in_specs=[pl.BlockSpec((1, 1, cap, d), lambda ei, bi, f: (bi, ei, 0, 0)),
                  pl.BlockSpec((1, 1, cap, 1), lambda ei, bi, f: (bi, ei, 0, 0)),
                  pl.BlockSpec((1, d, FF_TILE), lambda ei, bi, f: (ei, 0, f)),
                  pl.BlockSpec((1, d, FF_TILE), lambda ei, bi, f: (ei, 0, f)),
                  pl.BlockSpec((1, FF_TILE, d), lambda ei, bi, f: (ei, f, 0))],
        out_specs=pl.BlockSpec((1, 1, cap, d), lambda ei, bi, f: (bi, ei, 0, 0)),
        out_shape=jax.ShapeDtypeStruct((b, e, cap, d), BF16),
        scratch_shapes=[pltpu.VMEM((cap, d), F32)],
        compiler_params=_cparams(("parallel", "parallel", "arbitrary")),
        name="moe_ffn",
    )(xe, gs, w_gate, w_up, w_down)


def _combine_kernel(off_ref, x_ref, ye_ref, pos_ref, mod_ref, o_ref, *, st, n_tiles, nsub, nb):
    bi, ch, e = pl.program_id(0), pl.program_id(1), pl.program_id(2)

    @pl.when(e == 0)
    def _():
        o_ref[...] = jnp.zeros(o_ref.shape, F32)

    slot = lax.broadcasted_iota(jnp.int32, (st, ROUTE_T), 0)
    for jj in range(nsub):
        jb = ch * nsub + jj
        a = off_ref[(bi * N_EXPERTS + e) * nb + jb] // st
        prow = pos_ref[0, jj, pl.ds(e, 1), :]
        tot = jnp.zeros((ROUTE_T, o_ref.shape[2]), F32)
        for w in range(2):
            tile = jnp.minimum(a + w, n_tiles - 1)
            hit = slot == (prow - (a + w) * st)
            base = pl.multiple_of(tile * st, st)
            oh = jnp.where(hit, 1.0, 0.0).astype(BF16)
            tot = tot + lax.dot_general(oh, ye_ref[0, 0, pl.ds(base, st), :], (((0,), (0,)), ((), ())),
                                        preferred_element_type=F32)
        o_ref[0, jj * ROUTE_T:(jj + 1) * ROUTE_T, :] += tot

    @pl.when(e == pl.num_programs(2) - 1)
    def _():
        o_ref[0] = x_ref[0] + mod_ref[0, 5:6, :] * o_ref[0]


def moe_combine(offs, x, ye, pos, mod, cap):
    b, n, d = x.shape
    nb = n // ROUTE_T
    st = min(ROUTE_T, cap)
    n_tiles = cap // st
    chunk = min(2048, n)
    nsub = chunk // ROUTE_T
    kern = functools.partial(_combine_kernel, st=st, n_tiles=n_tiles, nsub=nsub, nb=nb)
    return pl.pallas_call(
        kern,
        grid_spec=pltpu.PrefetchScalarGridSpec(
            num_scalar_prefetch=1,
            grid=(b, n // chunk, N_EXPERTS),
            in_specs=[pl.BlockSpec((1, chunk, d), lambda i, c, e, off: (i, c, 0)),
                      pl.BlockSpec((1, 1, cap, d), lambda i, c, e, off: (i, e, 0, 0)),
                      pl.BlockSpec((1, nsub, N_EXPERTS, ROUTE_T), lambda i, c, e, off: (i, c, 0, 0)),
                      pl.BlockSpec((1, 8, d), lambda i, c, e, off: (i, 0, 0))],
            out_specs=pl.BlockSpec((1, chunk, d), lambda i, c, e, off: (i, c, 0))),
        out_shape=jax.ShapeDtypeStruct((b, n, d), F32),
        compiler_params=_cparams(("parallel", "parallel", "arbitrary")),
        name="moe_combine",
    )(offs, x, ye, pos, mod)


def expert_choice_ffn(x, h, logits, mod, tri, w_gate, w_up, w_down):
    b, n, _ = x.shape
    cap = EC_CAPACITY * n // N_EXPERTS
    aff, pos, off = route(logits, tri, cap)
    offs = jnp.transpose(off[..., 0], (0, 2, 1)).reshape(-1)
    xe, gs = moe_gather(offs, h, pos, aff, cap)
    ye = moe_ffn(xe, gs, w_gate, w_up, w_down)
    return moe_combine(offs, x, ye, pos, mod, cap)


def _rope_tables(n_rows_grid):
    s = n_rows_grid * GRID_W
    t = np.arange(s)
    row, col = (t // GRID_W).astype(np.float64), (t % GRID_W).astype(np.float64)

    def unit(n):
        half = n // 2
        inv = ROPE_THETA ** (-np.arange(0, n, 2, dtype=np.float64) / n)
        out = []
        for pos in (row, col):
            ang = pos[:, None] * inv[None, :]
            c, sn = np.cos(ang), np.sin(ang)
            z = np.zeros_like(sn)
            out.append((np.concatenate([c, c], 1), np.concatenate([-sn, z], 1), np.concatenate([z, sn], 1)))
        return [np.concatenate([out[0][i], out[1][i]], 1) for i in range(3)]

    ident = lambda w: (np.ones((s, w), np.float32), np.zeros((s, w), np.float32), np.zeros((s, w), np.float32))
    u16 = unit(16)
    u32 = unit(32)
    tabs = []
    idt = ident(64)
    idt32 = ident(32)
    for i in range(3):
        tabs.append(np.concatenate([idt[i], u16[i], idt32[i]], 1))
    for i in range(3):
        tabs.append(np.concatenate([u16[i]] * 4, 1))
    for i in range(3):
        tabs.append(np.concatenate([u32[i]] * 2, 1))
    return np.stack(tabs).astype(np.float32)


def _identity_tables(n):
    one, zero = np.ones((n, LANES), np.float32), np.zeros((n, LANES), np.float32)
    return np.stack([one, zero, zero] * 3)


def _seg_matrices():
    i = np.arange(LANES)
    ones = np.ones((LANES, LANES), np.float32)
    m64 = (i[:, None] // 64 == i[None, :] // 64).astype(np.float32)
    m32 = (i[:, None] // 32 == i[None, :] // 32).astype(np.float32)
    return np.stack([ones, m64, m32])


def _na_bias(rpb):
    t = np.arange(NA_TQ)
    rl, c = t // GRID_W, t % GRID_W
    out = []
    for d in (-1, 0, 1):
        dr = np.clip(NA_QROWS * d + rl[None, :] - rl[:, None] + NA_WIN_H - 1, 0, 2 * NA_WIN_H - 2)
        dc = np.clip(c[None, :] - c[:, None] + NA_WIN_W - 1, 0, 2 * NA_WIN_W - 2)
        out.append(rpb[:, dr, dc])
    return jnp.stack(out, axis=1).astype(F32)


def _pad_cols(w, width):
    return jnp.pad(w, ((0, 0), (0, width - w.shape[1])))


def _layer_params(l, w_in, g_na_q, g_na_k, na_rpb, g_mla_cq, w_mla_uq, g_mla_q, g_mla_ckv, w_mla_ukv,
                  g_mla_k_nope, g_mla_k_rope, g_diff_q, g_diff_k, g_diff_sub, g_gqa_q, g_gqa_k, w_out):
    wi = w_in[l]
    (naq, nak, nav, cq, ckv, kr, dq, dk, dv, gq, gk, gv) = jnp.split(
        wi, np.cumsum([256, 256, 256, 256, 128, 32, 256, 256, 256, 256, 128])[:], axis=1)
    gq4 = gq.reshape(-1, 4, 64)[:, jnp.array([0, 2, 1, 3])].reshape(-1, 256)
    zeros = lambda w: jnp.zeros((wi.shape[0], w), wi.dtype)
    w_in_r = jnp.concatenate([naq, nak, nav, cq, ckv, dq, dk, dv, gq4, gk, gv, zeros(64), kr, zeros(32)],
                             axis=1).astype(BF16)
    uq = w_mla_uq[l].reshape(MLA_Q_RANK, 4, 96)
    wuq = jnp.pad(uq, ((0, 0), (0, 0), (0, 32))).reshape(MLA_Q_RANK, 512).astype(BF16)
    ukv = w_mla_ukv[l].reshape(MLA_KV_RANK, 4, 128)
    wuk = jnp.pad(ukv[:, :, :64], ((0, 0), (0, 0), (0, 64))).reshape(MLA_KV_RANK, 512).astype(BF16)
    wuv = ukv[:, :, 64:].reshape(MLA_KV_RANK, 256).astype(BF16)
    row = lambda v: jnp.pad(v, (0, 512 - v.shape[0]))
    z32, z64 = jnp.zeros((32,), F32), jnp.zeros((64,), F32)
    gains = jnp.stack([
        row(jnp.tile(g_na_q[l], 4)), row(jnp.tile(g_na_k[l], 4)), row(g_mla_cq[l]),
        row(jnp.tile(jnp.concatenate([g_mla_q[l], z32]), 4)), row(g_mla_ckv[l]),
        row(jnp.tile(jnp.concatenate([g_mla_k_nope[l], z64]), 4)),
        row(jnp.concatenate([z64, g_mla_k_rope[l], z32])),
        row(jnp.tile(g_diff_q[l].reshape(-1), 4)), row(jnp.tile(g_diff_k[l].reshape(-1), 4)),
        row(jnp.tile(g_gqa_q[l], 4)), row(jnp.tile(g_gqa_k[l], 2))] + [jnp.zeros((512,), F32)] * 5)
    wo = w_out[l]
    wo_g = wo[768:].reshape(4, 64, -1)[jnp.array([0, 2, 1, 3])].reshape(256, -1)
    w_out_r = jnp.concatenate([wo[:768], wo_g], axis=0).astype(BF16)
    return dict(w_in=w_in_r, wuq=wuq, wuk=wuk, wuv=wuv, gains=gains, w_out=w_out_r,
                bias=_na_bias(na_rpb[l]), g_sub=jnp.tile(g_diff_sub[l], 2).reshape(1, LANES))


def kernel(x, c, ctx, c_ctx, w_mod, b_mod, g_attn, g_ffn, w_in, g_na_q, g_na_k, na_rpb, g_mla_cq, w_mla_uq, g_mla_q, g_mla_ckv, w_mla_ukv, g_mla_k_nope, g_mla_k_rope, g_diff_q, g_diff_k, diff_lambda, g_diff_sub, g_gqa_q, g_gqa_k, w_out, w_router, w_gate, w_up, w_down):
    b, s, d = x.shape
    n_ctx = ctx.shape[1]
    rows = s // GRID_W
    tabs = jnp.asarray(_rope_tables(rows))
    tabs_ctx = jnp.asarray(_identity_tables(n_ctx))
    mats = jnp.asarray(_seg_matrices()).astype(BF16)
    tri = jnp.asarray(np.triu(np.ones((ROUTE_T, ROUTE_T), np.float32), 1)).astype(BF16)
    c_rows = jnp.concatenate([c, c_ctx[None], jnp.zeros((8 - b - 1, d), F32)], axis=0)
    xc = ctx
    for l in range(DEPTH):
        need_ctx = l < DEPTH - 1
        lam_init = 0.8 - 0.6 * math.exp(-0.3 * l)
        lp = diff_lambda[l].astype(F32)
        lam = (jnp.exp(jnp.sum(lp[0] * lp[1])) - jnp.exp(jnp.sum(lp[2] * lp[3])) + lam_init).reshape(1, 1)
        prm = _layer_params(l, w_in, g_na_q, g_na_k, na_rpb, g_mla_cq, w_mla_uq, g_mla_q, g_mla_ckv, w_mla_ukv,
                            g_mla_k_nope, g_mla_k_rope, g_diff_q, g_diff_k, g_diff_sub, g_gqa_q, g_gqa_k, w_out)
        modv = mod_vectors(c_rows, w_mod[l], b_mod[l]).reshape(8, N_MOD, d)
        mod = jnp.pad(modv[:b], ((0, 0), (0, 2), (0, 0)))
        mod_c = jnp.broadcast_to(jnp.pad(modv[b:b + 1], ((0, 0), (0, 2), (0, 0))), (b, 8, d))
        w_router_t = w_router[l].T

        def mix_inputs(xin, m, tb):
            p = ln_mod_proj(xin, g_attn[l], m[:, 0:1], m[:, 1:2], prm["w_in"])
            return prep(p, tb, prm["gains"], mats, prm["wuq"], prm["wuk"], prm["wuv"])

        (naq, nak, nav, mq, mk, mv, dq, dk, dv, gq, gk, gv) = mix_inputs(x, mod, tabs)
        (naq_c, nak_c, nav_c, mq_c, mk_c, mv_c, dq_c, dk_c, dv_c, gq_c, gk_c, gv_c) = mix_inputs(xc, mod_c, tabs_ctx)
        cat = lambda a, bb: jnp.concatenate([a, bb], axis=1)
        o_na = na_attention(naq, nak, nav, nak_c, nav_c, prm["bias"], rows)
        o_mla = flash_attention(mq, cat(mk_c, mk), cat(mv_c, mv), HEADS_MLA)
        o_diff = flash_attention(dq, cat(dk_c, dk), cat(dv_c, dv), HEADS_DIFF)
        o_gqa = flash_attention(gq, cat(gk_c, gk), cat(gv_c, gv), HEADS_GQA)
        x_mid, h2, logits = outproj_mod_router(x, o_na, o_mla, o_diff, o_gqa, mod, prm["g_sub"], g_ffn[l], lam,
                                               mats, prm["w_out"], w_router_t, lam_init)
        if need_ctx:
            oc_na = flash_attention(naq_c, nak_c, nav_c, HEADS_PAIRED)
            oc_mla = flash_attention(mq_c, mk_c, mv_c, HEADS_MLA)
            oc_diff = flash_attention(dq_c, dk_c, dv_c, HEADS_DIFF)
            oc_gqa = flash_attention(gq_c, gk_c, gv_c, HEADS_GQA)
            xc_mid, hc2, logits_c = outproj_mod_router(xc, oc_na, oc_mla, oc_diff, oc_gqa, mod_c, prm["g_sub"],
                                                       g_ffn[l], lam, mats, prm["w_out"], w_router_t, lam_init)
            xc = expert_choice_ffn(xc_mid, hc2, logits_c, mod_c, tri, w_gate[l], w_up[l], w_down[l])
        x = expert_choice_ffn(x_mid, h2, logits, mod, tri, w_gate[l], w_up[l], w_down[l])
    return x
```

```python
import functools
import math

import numpy as np
import jax
import jax.numpy as jnp
from jax import lax
from jax.experimental import pallas as pl
from jax.experimental.pallas import tpu as pltpu

D_MODEL = 1024
GRID_W = 64
HEAD_DIM = 64
N_HEADS = 4
NA_WIN_H = 8
NA_WIN_W = 16
MLA_Q_RANK = 256
MLA_KV_RANK = 128
MLA_NOPE_DIM = 64
MLA_ROPE_DIM = 32
MLA_V_DIM = 64
DIFF_QK_DIM = 32
N_EXPERTS = 16
EC_CAPACITY = 2
D_FF = 2816
ROPE_THETA = 10000.0
NORM_EPS = 1e-6
N_MOD = 6
DEPTH = 2

LANES = 128
P_COLS = 2560
VMEM_LIMIT = 56 * 1024 * 1024
NEG_BIG = -1e30
LOG2E = math.log2(math.e)

F32 = jnp.float32
BF16 = jnp.bfloat16


def _cparams(sem):
    return pltpu.CompilerParams(dimension_semantics=sem, vmem_limit_bytes=VMEM_LIMIT)


def _split(a):
    hi = a.astype(BF16)
    lo = (a - hi.astype(F32)).astype(BF16)
    return hi, lo


def _dot(a, b):
    return jnp.dot(a, b, preferred_element_type=F32)


def _dot_nt(a, b):
    return lax.dot_general(a, b, (((1,), (1,)), ((), ())), preferred_element_type=F32)


def _dot3(a, b):
    ah, al = _split(a)
    bh, bl = _split(b)
    return _dot(ah, bh) + _dot(ah, bl) + _dot(al, bh)


def _dot3_nt(a, b):
    ah, al = _split(a)
    bh, bl = _split(b)
    return _dot_nt(ah, bh) + _dot_nt(ah, bl) + _dot_nt(al, bh)


def _silu(v):
    return v * jax.nn.sigmoid(v)


def _mod_kernel(c_ref, w_ref, b_ref, o_ref):
    o_ref[...] = _dot3(_silu(c_ref[...]), w_ref[...]) + b_ref[...]


def mod_vectors(c_rows, w_mod, b_mod):
    m, d = c_rows.shape
    n = w_mod.shape[1]
    tn = 1536
    return pl.pallas_call(
        _mod_kernel,
        grid=(n // tn,),
        in_specs=[pl.BlockSpec((m, d), lambda j: (0, 0)),
                  pl.BlockSpec((d, tn), lambda j: (0, j)),
                  pl.BlockSpec((1, tn), lambda j: (0, j))],
        out_specs=pl.BlockSpec((m, tn), lambda j: (0, j)),
        out_shape=jax.ShapeDtypeStruct((m, n), F32),
        compiler_params=_cparams(("arbitrary",)),
        name="mod_vectors",
    )(c_rows, w_mod, b_mod.reshape(1, n))


def _ln_proj_kernel(x_ref, g_ref, sh_ref, sc_ref, w_ref, o_ref):
    x = x_ref[0]
    ms = jnp.mean(x * x, axis=-1, keepdims=True)
    y = x * lax.rsqrt(ms + NORM_EPS) * g_ref[...]
    h = y * (1.0 + sc_ref[0]) + sh_ref[0]
    o_ref[0] = _dot(h.astype(BF16), w_ref[...])


def ln_mod_proj(x, g, shift, scale, w):
    b, n, d = x.shape
    ncol = w.shape[1]
    tm = min(512, n)
    return pl.pallas_call(
        _ln_proj_kernel,
        grid=(b, n // tm),
        in_specs=[pl.BlockSpec((1, tm, d), lambda i, j: (i, j, 0)),
                  pl.BlockSpec((1, d), lambda i, j: (0, 0)),
                  pl.BlockSpec((1, 1, d), lambda i, j: (i, 0, 0)),
                  pl.BlockSpec((1, 1, d), lambda i, j: (i, 0, 0)),
                  pl.BlockSpec((d, ncol), lambda i, j: (0, 0))],
        out_specs=pl.BlockSpec((1, tm, ncol), lambda i, j: (i, j, 0)),
        out_shape=jax.ShapeDtypeStruct((b, n, ncol), F32),
        compiler_params=_cparams(("parallel", "arbitrary")),
        name="ln_mod_proj",
    )(x, g.reshape(1, d), shift, scale, w)


P_NAQ, P_NAK, P_NAV, P_CQ, P_CKV = 0, 256, 512, 768, 1024
P_DQ, P_DK, P_DV, P_GQ, P_GK, P_GV, P_KR = 1152, 1408, 1664, 1920, 2176, 2304, 2432
(G_NAQ, G_NAK, G_CQ, G_MQ, G_CKV, G_KN, G_KR, G_DQ, G_DK, G_GQ, G_GK) = range(11)
M_ONES, M_SEG64, M_SEG32 = 0, 1, 2
T_MLA, T_DIFF, T_GQA = 0, 3, 6


def _seg_norm(x, mat, inv_n, g):
    sq = x * x
    hi, lo = _split(sq)
    ms = (_dot(hi, mat) + _dot(lo, mat)) * inv_n
    return x * lax.rsqrt(ms + NORM_EPS) * g


def _rope(x, tab_ref, t0, half):
    c, s1, s2 = tab_ref[t0], tab_ref[t0 + 1], tab_ref[t0 + 2]
    return (x * c + pltpu.roll(x, LANES - half, axis=1) * s1
            + pltpu.roll(x, half, axis=1) * s2)


def _prep_kernel(p_ref, tab_ref, gain_ref, mat_ref, wuq_ref, wuk_ref, wuv_ref,
                 naq_ref, nak_ref, nav_ref, mq_ref, mk_ref, mv_ref,
                 dq_ref, dk_ref, dv_ref, gq_ref, gk_ref, gv_ref):
    lane = lax.broadcasted_iota(jnp.int32, (1, LANES), 1)
    lo_half = lane < 64
    ones_m, m64, m32 = mat_ref[M_ONES], mat_ref[M_SEG64], mat_ref[M_SEG32]

    def slab(off, i):
        return p_ref[0, :, off + i * LANES: off + (i + 1) * LANES]

    def gain(row, i):
        return gain_ref[row:row + 1, i * LANES:(i + 1) * LANES]

    s_na = HEAD_DIM ** -0.5 * LOG2E
    for i in range(2):
        q = _seg_norm(slab(P_NAQ, i), m64, 1.0 / 64, gain(G_NAQ, i)) * s_na
        naq_ref[0, :, (2 * i) * LANES:(2 * i + 1) * LANES] = jnp.where(lo_half, q, 0.0).astype(BF16)
        naq_ref[0, :, (2 * i + 1) * LANES:(2 * i + 2) * LANES] = jnp.where(lo_half, 0.0, q).astype(BF16)
        k = _seg_norm(slab(P_NAK, i), m64, 1.0 / 64, gain(G_NAK, i))
        nak_ref[0, :, i * LANES:(i + 1) * LANES] = k.astype(BF16)
        nav_ref[0, :, i * LANES:(i + 1) * LANES] = slab(P_NAV, i).astype(BF16)

    cq = p_ref[0, :, P_CQ:P_CQ + MLA_Q_RANK]
    cq = cq * lax.rsqrt(jnp.mean(cq * cq, axis=-1, keepdims=True) + NORM_EPS) * gain_ref[G_CQ:G_CQ + 1, :MLA_Q_RANK]
    uq = _dot(cq.astype(BF16), wuq_ref[...])
    s_mla = (MLA_NOPE_DIM + MLA_ROPE_DIM) ** -0.5 * LOG2E
    ckv = p_ref[0, :, P_CKV:P_CKV + MLA_KV_RANK]
    ckv = ckv * lax.rsqrt(jnp.mean(ckv * ckv, axis=-1, keepdims=True) + NORM_EPS) * gain_ref[G_CKV:G_CKV + 1, :MLA_KV_RANK]
    ckv_b = ckv.astype(BF16)
    uk = _dot(ckv_b, wuk_ref[...])
    mv_ref[0] = _dot(ckv_b, wuv_ref[...]).astype(BF16)
    kr = _seg_norm(slab(P_KR, 0), ones_m, 1.0 / MLA_ROPE_DIM, gain(G_KR, 0))
    kr = _rope(kr, tab_ref, T_MLA, 8)
    for h in range(N_HEADS):
        q = _seg_norm(uq[:, h * LANES:(h + 1) * LANES], ones_m, 1.0 / (MLA_NOPE_DIM + MLA_ROPE_DIM), gain(G_MQ, h))
        q = _rope(q, tab_ref, T_MLA, 8) * s_mla
        mq_ref[0, :, h * LANES:(h + 1) * LANES] = q.astype(BF16)
        kn = _seg_norm(uk[:, h * LANES:(h + 1) * LANES], ones_m, 1.0 / MLA_NOPE_DIM, gain(G_KN, h))
        mk_ref[0, :, h * LANES:(h + 1) * LANES] = (kn + kr).astype(BF16)

    s_d = DIFF_QK_DIM ** -0.5 * LOG2E
    seg = lane >> 5
    for i in range(2):
        q = _seg_norm(slab(P_DQ, i), m32, 1.0 / 32, gain(G_DQ, i))
        q = _rope(q, tab_ref, T_DIFF, 8) * s_d
        for j in range(4):
            dq_ref[0, :, (4 * i + j) * LANES:(4 * i + j + 1) * LANES] = jnp.where(seg == j, q, 0.0).astype(BF16)
        k = _seg_norm(slab(P_DK, i), m32, 1.0 / 32, gain(G_DK, i))
        dk_ref[0, :, i * LANES:(i + 1) * LANES] = _rope(k, tab_ref, T_DIFF, 8).astype(BF16)
        dv_ref[0, :, i * LANES:(i + 1) * LANES] = slab(P_DV, i).astype(BF16)

    s_g = HEAD_DIM ** -0.5 * LOG2E
    for i in range(2):
        q = _seg_norm(slab(P_GQ, i), m64, 1.0 / 64, gain(G_GQ, i))
        q = _rope(q, tab_ref, T_GQA, 16) * s_g
        gq_ref[0, :, (2 * i) * LANES:(2 * i + 1) * LANES] = jnp.where(lo_half, q, 0.0).astype(BF16)
        gq_ref[0, :, (2 * i + 1) * LANES:(2 * i + 2) * LANES] = jnp.where(lo_half, 0.0, q).astype(BF16)
    k = _seg_norm(slab(P_GK, 0), m64, 1.0 / 64, gain(G_GK, 0))
    gk_ref[0] = _rope(k, tab_ref, T_GQA, 16).astype(BF16)
    gv_ref[0] = slab(P_GV, 0).astype(BF16)


PREP_WIDTHS = (512, 256, 256, 512, 512, 256, 1024, 256, 256, 512, 128, 128)


def prep(p, tabs, gains, mats, wuq, wuk, wuv):
    b, n, _ = p.shape
    tm = min(512, n)
    const2 = lambda i, j: (0, 0)
    return pl.pallas_call(
        _prep_kernel,
        grid=(b, n // tm),
        in_specs=[pl.BlockSpec((1, tm, P_COLS), lambda i, j: (i, j, 0)),
                  pl.BlockSpec((9, tm, LANES), lambda i, j: (0, j, 0)),
                  pl.BlockSpec(gains.shape, const2),
                  pl.BlockSpec(mats.shape, lambda i, j: (0, 0, 0)),
                  pl.BlockSpec(wuq.shape, const2),
                  pl.BlockSpec(wuk.shape, const2),
                  pl.BlockSpec(wuv.shape, const2)],
        out_specs=[pl.BlockSpec((1, tm, w), lambda i, j: (i, j, 0)) for w in PREP_WIDTHS],
        out_shape=[jax.ShapeDtypeStruct((b, n, w), BF16) for w in PREP_WIDTHS],
        compiler_params=_cparams(("parallel", "arbitrary")),
        name="prep",
    )(p, tabs, gains, mats, wuq, wuk, wuv)


def _flash_kernel(q_ref, k_ref, v_ref, o_ref, m_ref, l_ref, acc_ref, *, heads, n_out):
    ki = pl.program_id(2)

    @pl.when(ki == 0)
    def _():
        m_ref[...] = jnp.full(m_ref.shape, NEG_BIG, F32)
        l_ref[...] = jnp.zeros(l_ref.shape, F32)
        acc_ref[...] = jnp.zeros(acc_ref.shape, F32)

    lo_half = lax.broadcasted_iota(jnp.int32, (1, LANES), 1) < 64
    for o in range(n_out):
        pair = [h for h in range(len(heads)) if heads[h][3] == o]
        pv, alpha = [], []
        for h in pair:
            qs, ks, vs, _ = heads[h]
            q = q_ref[0, :, qs * LANES:(qs + 1) * LANES]
            k = k_ref[0, :, ks * LANES:(ks + 1) * LANES]
            v = v_ref[0, :, vs * 2 * LANES:(vs + 1) * 2 * LANES]
            s = _dot_nt(q, k)
            m_prev = m_ref[h]
            m_new = jnp.maximum(m_prev, jnp.max(s, axis=-1, keepdims=True))
            a = jnp.exp2(m_prev - m_new)
            p = jnp.exp2(s - m_new).astype(BF16)
            r = _dot(p, v)
            l_ref[h] = a * l_ref[h] + r[:, LANES:]
            m_ref[h] = m_new
            pv.append(r[:, :LANES])
            alpha.append(a)
        acc_ref[o] = acc_ref[o] * jnp.where(lo_half, alpha[0], alpha[1]) + jnp.where(lo_half, pv[0], pv[1])

    @pl.when(ki == pl.num_programs(2) - 1)
    def _():
        for o in range(n_out):
            pair = [h for h in range(len(heads)) if heads[h][3] == o]
            l = jnp.where(lo_half, l_ref[pair[0]], l_ref[pair[1]])
            o_ref[0, :, o * LANES:(o + 1) * LANES] = acc_ref[o] / l


def _pick_tk(n_keys):
    best = 256
    for t in range(256, 1537, 256):
        if n_keys % t == 0:
            best = t
    return best


def flash_attention(q, k, v, heads):
    b, s, qw = q.shape
    nk = k.shape[1]
    n_out = max(h[3] for h in heads) + 1
    tq = min(512, s)
    tk = _pick_tk(nk)
    kern = functools.partial(_flash_kernel, heads=heads, n_out=n_out)
    return pl.pallas_call(
        kern,
        grid=(b, s // tq, nk // tk),
        in_specs=[pl.BlockSpec((1, tq, qw), lambda i, j, t: (i, j, 0)),
                  pl.BlockSpec((1, tk, k.shape[2]), lambda i, j, t: (i, t, 0)),
                  pl.BlockSpec((1, tk, v.shape[2]), lambda i, j, t: (i, t, 0))],
        out_specs=pl.BlockSpec((1, tq, n_out * LANES), lambda i, j, t: (i, j, 0)),
        out_shape=jax.ShapeDtypeStruct((b, s, n_out * LANES), F32),
        scratch_shapes=[pltpu.VMEM((len(heads), tq, 1), F32),
                        pltpu.VMEM((len(heads), tq, LANES), F32),
                        pltpu.VMEM((n_out, tq, LANES), F32)],
        compiler_params=_cparams(("parallel", "parallel", "arbitrary")),
        name="flash_attention",
    )(q, k, v)


HEADS_PAIRED = tuple((h, h // 2, h // 2, h // 2) for h in range(4))
HEADS_MLA = tuple((h, h, h // 2, h // 2) for h in range(4))
HEADS_GQA = tuple((h, 0, 0, h // 2) for h in range(4))
HEADS_DIFF = tuple((j, j // 4, j // 4, (j % 2) * 2 + j // 4) for j in range(8))


NA_QROWS = 4
NA_TQ = NA_QROWS * GRID_W


def _na_kernel(q_ref, kp_ref, kc_ref, kn_ref, vp_ref, vc_ref, vn_ref, kx_ref, vx_ref, bias_ref, o_ref, *, rows):
    j = pl.program_id(1)
    kh = min(NA_WIN_H, rows)
    qi = lax.broadcasted_iota(jnp.int32, (NA_TQ, 1), 0)
    ki = lax.broadcasted_iota(jnp.int32, (1, NA_TQ), 1)
    wshift = GRID_W.bit_length() - 1
    r = j * NA_QROWS + (qi >> wshift)
    c = qi & (GRID_W - 1)
    r0 = jnp.clip(r - kh // 2, 0, rows - kh)
    c0 = jnp.clip(c - NA_WIN_W // 2, 0, GRID_W - NA_WIN_W)
    kcol = ki & (GRID_W - 1)
    col_ok = (kcol >= c0) & (kcol < c0 + NA_WIN_W)
    masks = []
    for d in (-1, 0, 1):
        kr = (j + d) * NA_QROWS + (ki >> wshift)
        masks.append(col_ok & (kr >= r0) & (kr < r0 + kh))
    lo_half = lax.broadcasted_iota(jnp.int32, (1, LANES), 1) < 64
    for o in range(2):
        k_loc = [kp_ref[0, :, o * LANES:(o + 1) * LANES], kc_ref[0, :, o * LANES:(o + 1) * LANES],
                 kn_ref[0, :, o * LANES:(o + 1) * LANES]]
        v_all = jnp.concatenate([vp_ref[0, :, o * LANES:(o + 1) * LANES], vc_ref[0, :, o * LANES:(o + 1) * LANES],
                                 vn_ref[0, :, o * LANES:(o + 1) * LANES], vx_ref[0, :, o * LANES:(o + 1) * LANES]],
                                axis=0)
        kx = kx_ref[0, :, o * LANES:(o + 1) * LANES]
        res = []
        for h in (2 * o, 2 * o + 1):
            q = q_ref[0, :, h * LANES:(h + 1) * LANES]
            parts = [jnp.where(masks[d], _dot_nt(q, k_loc[d]) + bias_ref[h, d], NEG_BIG) for d in range(3)]
            parts.append(_dot_nt(q, kx))
            s = jnp.concatenate(parts, axis=1)
            m = jnp.max(s, axis=-1, keepdims=True)
            p = jnp.exp2(s - m)
            l = jnp.sum(p, axis=-1, keepdims=True)
            res.append(_dot(p.astype(BF16), v_all) / l)
        o_ref[0, :, o * LANES:(o + 1) * LANES] = jnp.where(lo_half, res[0], res[1])


def na_attention(q, k, v, kx, vx, bias, rows):
    b, s, _ = q.shape
    nblk = rows // NA_QROWS
    nx = kx.shape[1]
    kern = functools.partial(_na_kernel, rows=rows)
    prev = lambda i, j: (i, jnp.maximum(j - 1, 0), 0)
    cur = lambda i, j: (i, j, 0)
    nxt = lambda i, j: (i, jnp.minimum(j + 1, nblk - 1), 0)
    kv_spec = lambda f: pl.BlockSpec((1, NA_TQ, 2 * LANES), f)
    return pl.pallas_call(
        kern,
        grid=(b, nblk),
        in_specs=[pl.BlockSpec((1, NA_TQ, 4 * LANES), cur),
                  kv_spec(prev), kv_spec(cur), kv_spec(nxt),
                  kv_spec(prev), kv_spec(cur), kv_spec(nxt),
                  pl.BlockSpec((1, nx, 2 * LANES), lambda i, j: (i, 0, 0)),
                  pl.BlockSpec((1, nx, 2 * LANES), lambda i, j: (i, 0, 0)),
                  pl.BlockSpec(bias.shape, lambda i, j: (0, 0, 0, 0))],
        out_specs=pl.BlockSpec((1, NA_TQ, 2 * LANES), cur),
        out_shape=jax.ShapeDtypeStruct((b, s, 2 * LANES), F32),
        compiler_params=_cparams(("parallel", "arbitrary")),
        name="na_attention",
    )(q, k, k, k, v, v, v, kx, vx, bias)


ROUTE_T = 256


def _outproj_kernel(x_ref, ona_ref, omla_ref, od_ref, ogqa_ref, mod_ref, gsub_ref, gffn_ref, lam_ref, mat_ref,
                    wout_ref, wr_ref, xo_ref, h_ref, lg_ref, *, lam_init):
    lam = lam_ref[0, 0]
    m64 = mat_ref[M_SEG64]
    pieces = [ona_ref[0].astype(BF16), omla_ref[0].astype(BF16)]
    dsl = []
    for i in range(2):
        d = od_ref[0, :, i * LANES:(i + 1) * LANES] - lam * od_ref[0, :, (2 + i) * LANES:(3 + i) * LANES]
        d = _seg_norm(d, m64, 1.0 / 64, gsub_ref[...]) * (1.0 - lam_init)
        dsl.append(d.astype(BF16))
    pieces += dsl + [ogqa_ref[0].astype(BF16)]
    o = jnp.concatenate(pieces, axis=1)
    y = _dot(o, wout_ref[...])
    x = x_ref[0] + mod_ref[0, 2:3, :] * y
    xo_ref[0] = x
    ms = jnp.mean(x * x, axis=-1, keepdims=True)
    h = x * lax.rsqrt(ms + NORM_EPS) * gffn_ref[...]
    h = h * (1.0 + mod_ref[0, 4:5, :]) + mod_ref[0, 3:4, :]
    h_ref[0] = h.astype(BF16)
    lg = _dot3_nt(wr_ref[...], h)
    for t in range(lg.shape[1] // ROUTE_T):
        lg_ref[0, t] = lg[:, t * ROUTE_T:(t + 1) * ROUTE_T]


def outproj_mod_router(x, o_na, o_mla, o_diff, o_gqa, mod, g_sub_t, g_ffn, lam, mats, w_out, w_router_t, lam_init):
    b, n, d = x.shape
    tm = min(512, n)
    nt = tm // ROUTE_T
    kern = functools.partial(_outproj_kernel, lam_init=lam_init)
    tok = lambda w: pl.BlockSpec((1, tm, w), lambda i, j: (i, j, 0))
    c2 = lambda i, j: (0, 0)
    return pl.pallas_call(
        kern,
        grid=(b, n // tm),
        in_specs=[tok(d), tok(256), tok(256), tok(512), tok(256),
                  pl.BlockSpec((1, 8, d), lambda i, j: (i, 0, 0)),
                  pl.BlockSpec((1, LANES), c2), pl.BlockSpec((1, d), c2),
                  pl.BlockSpec(memory_space=pltpu.SMEM),
                  pl.BlockSpec(mats.shape, lambda i, j: (0, 0, 0)),
                  pl.BlockSpec((d, d), c2), pl.BlockSpec((N_EXPERTS, d), c2)],
        out_specs=[tok(d), tok(d),
                   pl.BlockSpec((1, nt, N_EXPERTS, ROUTE_T), lambda i, j: (i, j, 0, 0))],
        out_shape=[jax.ShapeDtypeStruct((b, n, d), F32), jax.ShapeDtypeStruct((b, n, d), BF16),
                   jax.ShapeDtypeStruct((b, n // ROUTE_T, N_EXPERTS, ROUTE_T), F32)],
        compiler_params=_cparams(("parallel", "arbitrary")),
        name="outproj_mod_router",
    )(x, o_na, o_mla, o_diff, o_gqa, mod, g_sub_t, g_ffn.reshape(1, d), lam, mats, w_out, w_router_t)


def _route_kernel(lg_ref, tri_ref, aff_ref, pos_ref, off_ref, *, cap):
    nb = lg_ref.shape[1]
    lg = lg_ref[0]
    mx = jnp.max(lg, axis=1, keepdims=True)
    ex = jnp.exp(lg - mx)
    aff = ex / jnp.sum(ex, axis=1, keepdims=True)
    aff_ref[0] = aff
    bits = lax.bitcast_convert_type(aff, jnp.int32)

    def count_ge(t):
        hit = jnp.where(bits >= t[None], 1.0, 0.0)
        return jnp.sum(jnp.sum(hit, axis=0), axis=1, keepdims=True)

    def bis(i, t):
        cand = t | (jnp.int32(1) << (30 - i))
        return jnp.where(count_ge(cand) >= float(cap), cand, t)

    thr = lax.fori_loop(0, 31, bis, jnp.zeros((N_EXPERTS, 1), jnp.int32))
    need = float(cap) - count_ge(thr + 1)
    tri = tri_ref[...]

    def blk(jb, carry):
        c_eq, c_pos = carry
        bb = lax.bitcast_convert_type(aff_ref[0, jb], jnp.int32)
        gt = bb > thr
        eq = bb == thr
        eq_before = _dot(jnp.where(eq, 1.0, 0.0).astype(BF16), tri) + c_eq
        sel = gt | (eq & (eq_before < need))
        sel_f = jnp.where(sel, 1.0, 0.0)
        before = _dot(sel_f.astype(BF16), tri) + c_pos
        pos_ref[0, jb] = jnp.where(sel, before, -1.0).astype(jnp.int32)
        off_ref[0, jb] = jnp.broadcast_to(c_pos, (N_EXPERTS, LANES)).astype(jnp.int32)
        c_eq = c_eq + jnp.sum(jnp.where(eq, 1.0, 0.0), axis=1, keepdims=True)
        c_pos = c_pos + jnp.sum(sel_f, axis=1, keepdims=True)
        return c_eq, c_pos

    zero = jnp.zeros((N_EXPERTS, 1), F32)
    lax.fori_loop(0, nb, blk, (zero, zero))


def route(logits, tri, cap):
    b, nb, e, t = logits.shape
    kern = functools.partial(_route_kernel, cap=cap)
    spec = pl.BlockSpec((1, nb, e, t), lambda i: (i, 0, 0, 0))
    return pl.pallas_call(
        kern,
        grid=(b,),
        in_specs=[spec, pl.BlockSpec(tri.shape, lambda i: (0, 0))],
        out_specs=[spec, spec, pl.BlockSpec((1, nb, e, LANES), lambda i: (i, 0, 0, 0))],
        out_shape=[jax.ShapeDtypeStruct((b, nb, e, t), F32), jax.ShapeDtypeStruct((b, nb, e, t), jnp.int32),
                   jax.ShapeDtypeStruct((b, nb, e, LANES), jnp.int32)],
        compiler_params=_cparams(("arbitrary",)),
        name="route",
    )(logits, tri)


def _gather_kernel(off_ref, h_ref, pos_ref, aff_ref, xe_ref, gs_ref, acc_ref, gacc_ref, *, st, n_tiles, nsub, nb):
    bi, e, ch = pl.program_id(0), pl.program_id(1), pl.program_id(2)

    @pl.when(ch == 0)
    def _():
        acc_ref[...] = jnp.zeros(acc_ref.shape, F32)
        gacc_ref[...] = jnp.zeros(gacc_ref.shape, F32)

    slot = lax.broadcasted_iota(jnp.int32, (st, ROUTE_T), 0)
    for jj in range(nsub):
        jb = ch * nsub + jj
        a = off_ref[(bi * N_EXPERTS + e) * nb + jb] // st
        prow = pos_ref[0, jj, pl.ds(e, 1), :]
        arow = aff_ref[0, jj, pl.ds(e, 1), :]
        hs = h_ref[0, jj * ROUTE_T:(jj + 1) * ROUTE_T, :]
        for w in range(2):
            tile = jnp.minimum(a + w, n_tiles - 1)
            hit = slot == (prow - (a + w) * st)
            base = pl.multiple_of(tile * st, st)
            acc_ref[pl.ds(base, st), :] += _dot(jnp.where(hit, 1.0, 0.0).astype(BF16), hs)
            gacc_ref[pl.ds(base, st), :] += jnp.sum(jnp.where(hit, arow, 0.0), axis=1, keepdims=True)

    @pl.when(ch == pl.num_programs(2) - 1)
    def _():
        xe_ref[0, 0] = acc_ref[...].astype(BF16)
        gs_ref[0, 0] = gacc_ref[...]


def moe_gather(offs, h, pos, aff, cap):
    b, n, d = h.shape
    nb = n // ROUTE_T
    st = min(ROUTE_T, cap)
    n_tiles = cap // st
    chunk = min(2048, n)
    nsub = chunk // ROUTE_T
    kern = functools.partial(_gather_kernel, st=st, n_tiles=n_tiles, nsub=nsub, nb=nb)
    rspec = pl.BlockSpec((1, nsub, N_EXPERTS, ROUTE_T), lambda i, e, c, off: (i, c, 0, 0))
    return pl.pallas_call(
        kern,
        grid_spec=pltpu.PrefetchScalarGridSpec(
            num_scalar_prefetch=1,
            grid=(b, N_EXPERTS, n // chunk),
            in_specs=[pl.BlockSpec((1, chunk, d), lambda i, e, c, off: (i, c, 0)), rspec, rspec],
            out_specs=[pl.BlockSpec((1, 1, cap, d), lambda i, e, c, off: (i, e, 0, 0)),
                       pl.BlockSpec((1, 1, cap, 1), lambda i, e, c, off: (i, e, 0, 0))],
            scratch_shapes=[pltpu.VMEM((cap, d), F32), pltpu.VMEM((cap, 1), F32)]),
        out_shape=[jax.ShapeDtypeStruct((b, N_EXPERTS, cap, d), BF16),
                   jax.ShapeDtypeStruct((b, N_EXPERTS, cap, 1), F32)],
        compiler_params=_cparams(("parallel", "parallel", "arbitrary")),
        name="moe_gather",
    )(offs, h, pos, aff)


FF_TILE = 256


def _ffn_kernel(x_ref, g_ref, wg_ref, wu_ref, wd_ref, y_ref, acc_ref):
    f = pl.program_id(2)

    @pl.when(f == 0)
    def _():
        acc_ref[...] = jnp.zeros(acc_ref.shape, F32)

    x = x_ref[0, 0]
    gate = _dot(x, wg_ref[0].astype(BF16))
    up = _dot(x, wu_ref[0].astype(BF16))
    hmid = (_silu(gate) * up).astype(BF16)
    acc_ref[...] += _dot(hmid, wd_ref[0].astype(BF16))

    @pl.when(f == pl.num_programs(2) - 1)
    def _():
        y_ref[0, 0] = (acc_ref[...] * g_ref[0, 0]).astype(BF16)


def moe_ffn(xe, gs, w_gate, w_up, w_down):
    b, e, cap, d = xe.shape
    dff = w_gate.shape[2]
    return pl.pallas_call(
        _ffn_kernel,
        grid=(e, b, dff // FF_TILE),
        in_specs=[pl.BlockSpec((1, 1, cap, d), lambda ei, bi, f: (bi, ei, 0, 0)),
                  pl.BlockSpec((1, 1, cap, 1), lambda ei, bi, f: (bi, ei, 0, 0)),
                  pl.BlockSpec((1, d, FF_TILE), lambda ei, bi, f: (ei, 0, f)),
                  pl.BlockSpec((1, d, FF_TILE), lambda ei, bi, f: (ei, 0, f)),
                  pl.BlockSpec((1, FF_TILE, d), lambda ei, bi, f: (ei, f, 0))],
        out_specs=pl.BlockSpec((1, 1, cap, d), lambda ei, bi, f: (bi, ei, 0, 0)),
        out_shape=jax.ShapeDtypeStruct((b, e, cap, d), BF16),
        scratch_shapes=[pltpu.VMEM((cap, d), F32)],
        compiler_params=_cparams(("parallel", "parallel", "arbitrary")),
        name="moe_ffn",
    )(xe, gs, w_gate, w_up, w_down)


def _combine_kernel(off_ref, x_ref, ye_ref, pos_ref, mod_ref, o_ref, *, st, n_tiles, nsub, nb):
    bi, ch, e = pl.program_id(0), pl.program_id(1), pl.program_id(2)

    @pl.when(e == 0)
    def _():
        o_ref[...] = jnp.zeros(o_ref.shape, F32)

    slot = lax.broadcasted_iota(jnp.int32, (st, ROUTE_T), 0)
    for jj in range(nsub):
        jb = ch * nsub + jj
        a = off_ref[(bi * N_EXPERTS + e) * nb + jb] // st
        prow = pos_ref[0, jj, pl.ds(e, 1), :]
        tot = jnp.zeros((ROUTE_T, o_ref.shape[2]), F32)
        for w in range(2):
            tile = jnp.minimum(a + w, n_tiles - 1)
            hit = slot == (prow - (a + w) * st)
            base = pl.multiple_of(tile * st, st)
            oh = jnp.where(hit, 1.0, 0.0).astype(BF16)
            tot = tot + lax.dot_general(oh, ye_ref[0, 0, pl.ds(base, st), :], (((0,), (0,)), ((), ())),
                                        preferred_element_type=F32)
        o_ref[0, jj * ROUTE_T:(jj + 1) * ROUTE_T, :] += tot

    @pl.when(e == pl.num_programs(2) - 1)
    def _():
        o_ref[0] = x_ref[0] + mod_ref[0, 5:6, :] * o_ref[0]


def moe_combine(offs, x, ye, pos, mod, cap):
    b, n, d = x.shape
    nb = n // ROUTE_T
    st = min(ROUTE_T, cap)
    n_tiles = cap // st
    chunk = min(2048, n)
    nsub = chunk // ROUTE_T
    kern = functools.partial(_combine_kernel, st=st, n_tiles=n_tiles, nsub=nsub, nb=nb)
    return pl.pallas_call(
        kern,
        grid_spec=pltpu.PrefetchScalarGridSpec(
            num_scalar_prefetch=1,
            grid=(b, n // chunk, N_EXPERTS),
            in_specs=[pl.BlockSpec((1, chunk, d), lambda i, c, e, off: (i, c, 0)),
                      pl.BlockSpec((1, 1, cap, d), lambda i, c, e, off: (i, e, 0, 0)),
                      pl.BlockSpec((1, nsub, N_EXPERTS, ROUTE_T), lambda i, c, e, off: (i, c, 0, 0)),
                      pl.BlockSpec((1, 8, d), lambda i, c, e, off: (i, 0, 0))],
            out_specs=pl.BlockSpec((1, chunk, d), lambda i, c, e, off: (i, c, 0))),
        out_shape=jax.ShapeDtypeStruct((b, n, d), F32),
        compiler_params=_cparams(("parallel", "parallel", "arbitrary")),
        name="moe_combine",
    )(offs, x, ye, pos, mod)


def expert_choice_ffn(x, h, logits, mod, tri, w_gate, w_up, w_down):
    b, n, _ = x.shape
    cap = EC_CAPACITY * n // N_EXPERTS
    aff, pos, off = route(logits, tri, cap)
    offs = jnp.transpose(off[..., 0], (0, 2, 1)).reshape(-1)
    xe, gs = moe_gather(offs, h, pos, aff, cap)
    ye = moe_ffn(xe, gs, w_gate, w_up, w_down)
    return moe_combine(offs, x, ye, pos, mod, cap)


def _rope_tables(n_rows_grid):
    s = n_rows_grid * GRID_W
    t = np.arange(s)
    row, col = (t // GRID_W).astype(np.float64), (t % GRID_W).astype(np.float64)

    def unit(n):
        half = n // 2
        inv = ROPE_THETA ** (-np.arange(0, n, 2, dtype=np.float64) / n)
        out = []
        for pos in (row, col):
            ang = pos[:, None] * inv[None, :]
            c, sn = np.cos(ang), np.sin(ang)
            z = np.zeros_like(sn)
            out.append((np.concatenate([c, c], 1), np.concatenate([-sn, z], 1), np.concatenate([z, sn], 1)))
        return [np.concatenate([out[0][i], out[1][i]], 1) for i in range(3)]

    ident = lambda w: (np.ones((s, w), np.float32), np.zeros((s, w), np.float32), np.zeros((s, w), np.float32))
    u16 = unit(16)
    u32 = unit(32)
    tabs = []
    idt = ident(64)
    idt32 = ident(32)
    for i in range(3):
        tabs.append(np.concatenate([idt[i], u16[i], idt32[i]], 1))
    for i in range(3):
        tabs.append(np.concatenate([u16[i]] * 4, 1))
    for i in range(3):
        tabs.append(np.concatenate([u32[i]] * 2, 1))
    return np.stack(tabs).astype(np.float32)


def _identity_tables(n):
    one, zero = np.ones((n, LANES), np.float32), np.zeros((n, LANES), np.float32)
    return np.stack([one, zero, zero] * 3)


def _seg_matrices():
    i = np.arange(LANES)
    ones = np.ones((LANES, LANES), np.float32)
    m64 = (i[:, None] // 64 == i[None, :] // 64).astype(np.float32)
    m32 = (i[:, None] // 32 == i[None, :] // 32).astype(np.float32)
    return np.stack([ones, m64, m32])


def _na_bias(rpb):
    c = np.arange(GRID_W)
    rl = np.arange(NA_QROWS)
    dc = np.clip(c[None, :] - c[:, None] + NA_WIN_W - 1, 0, 2 * NA_WIN_W - 2)
    d = np.array([-1, 0, 1])
    dr = np.clip(NA_QROWS * d[:, None, None] + rl[None, None, :] - rl[None, :, None] + NA_WIN_H - 1,
                 0, 2 * NA_WIN_H - 2)
    cols = jnp.take(rpb.astype(F32) * LOG2E, jnp.asarray(dc), axis=2)
    full = jnp.take(cols, jnp.asarray(dr), axis=1)
    return jnp.transpose(full, (0, 1, 2, 4, 3, 5)).reshape(rpb.shape[0], 3, NA_TQ, NA_TQ)


def _pad_cols(w, width):
    return jnp.pad(w, ((0, 0), (0, width - w.shape[1])))


def _layer_params(l, w_in, g_na_q, g_na_k, na_rpb, g_mla_cq, w_mla_uq, g_mla_q, g_mla_ckv, w_mla_ukv,
                  g_mla_k_nope, g_mla_k_rope, g_diff_q, g_diff_k, g_diff_sub, g_gqa_q, g_gqa_k, w_out):
    wi = w_in[l]
    (naq, nak, nav, cq, ckv, kr, dq, dk, dv, gq, gk, gv) = jnp.split(
        wi, np.cumsum([256, 256, 256, 256, 128, 32, 256, 256, 256, 256, 128])[:], axis=1)
    gq4 = gq.reshape(-1, 4, 64)[:, jnp.array([0, 2, 1, 3])].reshape(-1, 256)
    zeros = lambda w: jnp.zeros((wi.shape[0], w), wi.dtype)
    w_in_r = jnp.concatenate([naq, nak, nav, cq, ckv, dq, dk, dv, gq4, gk, gv, zeros(64), kr, zeros(32)],
                             axis=1).astype(BF16)
    uq = w_mla_uq[l].reshape(MLA_Q_RANK, 4, 96)
    wuq = jnp.pad(uq, ((0, 0), (0, 0), (0, 32))).reshape(MLA_Q_RANK, 512).astype(BF16)
    ukv = w_mla_ukv[l].reshape(MLA_KV_RANK, 4, 128)
    wuk = jnp.pad(ukv[:, :, :64], ((0, 0), (0, 0), (0, 64))).reshape(MLA_KV_RANK, 512).astype(BF16)
    wuv = ukv[:, :, 64:].reshape(MLA_KV_RANK, 256).astype(BF16)
    row = lambda v: jnp.pad(v, (0, 512 - v.shape[0]))
    z32, z64 = jnp.zeros((32,), F32), jnp.zeros((64,), F32)
    gains = jnp.stack([
        row(jnp.tile(g_na_q[l], 4)), row(jnp.tile(g_na_k[l], 4)), row(g_mla_cq[l]),
        row(jnp.tile(jnp.concatenate([g_mla_q[l], z32]), 4)), row(g_mla_ckv[l]),
        row(jnp.tile(jnp.concatenate([g_mla_k_nope[l], z64]), 4)),
        row(jnp.concatenate([z64, g_mla_k_rope[l], z32])),
        row(jnp.tile(g_diff_q[l].reshape(-1), 4)), row(jnp.tile(g_diff_k[l].reshape(-1), 4)),
        row(jnp.tile(g_gqa_q[l], 4)), row(jnp.tile(g_gqa_k[l], 2))] + [jnp.zeros((512,), F32)] * 5)
    wo = w_out[l]
    wo_g = wo[768:].reshape(4, 64, -1)[jnp.array([0, 2, 1, 3])].reshape(256, -1)
    w_out_r = jnp.concatenate([wo[:768], wo_g], axis=0).astype(BF16)
    return dict(w_in=w_in_r, wuq=wuq, wuk=wuk, wuv=wuv, gains=gains, w_out=w_out_r,
                bias=_na_bias(na_rpb[l]), g_sub=jnp.tile(g_diff_sub[l], 2).reshape(1, LANES))


def kernel(x, c, ctx, c_ctx, w_mod, b_mod, g_attn, g_ffn, w_in, g_na_q, g_na_k, na_rpb, g_mla_cq, w_mla_uq, g_mla_q, g_mla_ckv, w_mla_ukv, g_mla_k_nope, g_mla_k_rope, g_diff_q, g_diff_k, diff_lambda, g_diff_sub, g_gqa_q, g_gqa_k, w_out, w_router, w_gate, w_up, w_down):
    b, s, d = x.shape
    n_ctx = ctx.shape[1]
    rows = s // GRID_W
    tabs = jnp.asarray(_rope_tables(rows))
    tabs_ctx = jnp.asarray(_identity_tables(n_ctx))
    mats = jnp.asarray(_seg_matrices()).astype(BF16)
    tri = jnp.asarray(np.triu(np.ones((ROUTE_T, ROUTE_T), np.float32), 1)).astype(BF16)
    c_rows = jnp.concatenate([c, c_ctx[None], jnp.zeros((8 - b - 1, d), F32)], axis=0)
    xc = ctx
    for l in range(DEPTH):
        need_ctx = l < DEPTH - 1
        lam_init = 0.8 - 0.6 * math.exp(-0.3 * l)
        lp = diff_lambda[l].astype(F32)
        lam = (jnp.exp(jnp.sum(lp[0] * lp[1])) - jnp.exp(jnp.sum(lp[2] * lp[3])) + lam_init).reshape(1, 1)
        prm = _layer_params(l, w_in, g_na_q, g_na_k, na_rpb, g_mla_cq, w_mla_uq, g_mla_q, g_mla_ckv, w_mla_ukv,
                            g_mla_k_nope, g_mla_k_rope, g_diff_q, g_diff_k, g_diff_sub, g_gqa_q, g_gqa_k, w_out)
        modv = mod_vectors(c_rows, w_mod[l], b_mod[l]).reshape(8, N_MOD, d)
        mod = jnp.pad(modv[:b], ((0, 0), (0, 2), (0, 0)))
        mod_c = jnp.broadcast_to(jnp.pad(modv[b:b + 1], ((0, 0), (0, 2), (0, 0))), (b, 8, d))
        w_router_t = w_router[l].T

        def mix_inputs(xin, m, tb):
            p = ln_mod_proj(xin, g_attn[l], m[:, 0:1], m[:, 1:2], prm["w_in"])
            return prep(p, tb, prm["gains"], mats, prm["wuq"], prm["wuk"], prm["wuv"])

        (naq, nak, nav, mq, mk, mv, dq, dk, dv, gq, gk, gv) = mix_inputs(x, mod, tabs)
        (naq_c, nak_c, nav_c, mq_c, mk_c, mv_c, dq_c, dk_c, dv_c, gq_c, gk_c, gv_c) = mix_inputs(xc, mod_c, tabs_ctx)
        cat = lambda a, bb: jnp.concatenate([a, bb], axis=1)

        def ext(v):
            one = jnp.ones(v.shape[:2] + (LANES,), BF16)
            parts = []
            for i in range(v.shape[2] // LANES):
                parts += [v[:, :, i * LANES:(i + 1) * LANES], one]
            return jnp.concatenate(parts, axis=2)

        o_na = na_attention(naq, nak, nav, nak_c, nav_c, prm["bias"], rows)
        o_mla = flash_attention(mq, cat(mk_c, mk), ext(cat(mv_c, mv)), HEADS_MLA)
        o_diff = flash_attention(dq, cat(dk_c, dk), ext(cat(dv_c, dv)), HEADS_DIFF)
        o_gqa = flash_attention(gq, cat(gk_c, gk), ext(cat(gv_c, gv)), HEADS_GQA)
        x_mid, h2, logits = outproj_mod_router(x, o_na, o_mla, o_diff, o_gqa, mod, prm["g_sub"], g_ffn[l], lam,
                                               mats, prm["w_out"], w_router_t, lam_init)
        if need_ctx:
            oc_na = flash_attention(naq_c, nak_c, ext(nav_c), HEADS_PAIRED)
            oc_mla = flash_attention(mq_c, mk_c, ext(mv_c), HEADS_MLA)
            oc_diff = flash_attention(dq_c, dk_c, ext(dv_c), HEADS_DIFF)
            oc_gqa = flash_attention(gq_c, gk_c, ext(gv_c), HEADS_GQA)
            xc_mid, hc2, logits_c = outproj_mod_router(xc, oc_na, oc_mla, oc_diff, oc_gqa, mod_c, prm["g_sub"],
                                                       g_ffn[l], lam, mats, prm["w_out"], w_router_t, lam_init)
            xc = expert_choice_ffn(xc_mid, hc2, logits_c, mod_c, tri, w_gate[l], w_up[l], w_down[l])
        x = expert_choice_ffn(x_mid, h2, logits, mod, tri, w_gate[l], w_up[l], w_down[l])
    return x
```

```python
import functools
import math

import numpy as np
import jax
import jax.numpy as jnp
from jax import lax
from jax.experimental import pallas as pl
from jax.experimental.pallas import tpu as pltpu

D_MODEL = 1024
GRID_W = 64
HEAD_DIM = 64
N_HEADS = 4
NA_WIN_H = 8
NA_WIN_W = 16
MLA_Q_RANK = 256
MLA_KV_RANK = 128
MLA_NOPE_DIM = 64
MLA_ROPE_DIM = 32
MLA_V_DIM = 64
DIFF_QK_DIM = 32
N_EXPERTS = 16
EC_CAPACITY = 2
D_FF = 2816
ROPE_THETA = 10000.0
NORM_EPS = 1e-6
N_MOD = 6
DEPTH = 2

LANES = 128
P_COLS = 2560
VMEM_LIMIT = 56 * 1024 * 1024
NEG_BIG = -1e30
LOG2E = math.log2(math.e)

F32 = jnp.float32
BF16 = jnp.bfloat16


def _cparams(sem):
    return pltpu.CompilerParams(dimension_semantics=sem, vmem_limit_bytes=VMEM_LIMIT)


def _split(a):
    hi = a.astype(BF16)
    lo = (a - hi.astype(F32)).astype(BF16)
    return hi, lo


def _dot(a, b):
    return jnp.dot(a, b, preferred_element_type=F32)


def _dot_nt(a, b):
    return lax.dot_general(a, b, (((1,), (1,)), ((), ())), preferred_element_type=F32)


def _dot3(a, b):
    ah, al = _split(a)
    bh, bl = _split(b)
    return _dot(ah, bh) + _dot(ah, bl) + _dot(al, bh)


def _dot3_nt(a, b):
    ah, al = _split(a)
    bh, bl = _split(b)
    return _dot_nt(ah, bh) + _dot_nt(ah, bl) + _dot_nt(al, bh)


def _silu(v):
    return v * jax.nn.sigmoid(v)


def _mod_kernel(c_ref, w_ref, b_ref, o_ref):
    o_ref[...] = _dot3(_silu(c_ref[...]), w_ref[...]) + b_ref[...]


def mod_vectors(c_rows, w_mod, b_mod):
    m, d = c_rows.shape
    n = w_mod.shape[1]
    tn = 1536
    return pl.pallas_call(
        _mod_kernel,
        grid=(n // tn,),
        in_specs=[pl.BlockSpec((m, d), lambda j: (0, 0)),
                  pl.BlockSpec((d, tn), lambda j: (0, j)),
                  pl.BlockSpec((1, tn), lambda j: (0, j))],
        out_specs=pl.BlockSpec((m, tn), lambda j: (0, j)),
        out_shape=jax.ShapeDtypeStruct((m, n), F32),
        compiler_params=_cparams(("arbitrary",)),
        name="mod_vectors",
    )(c_rows, w_mod, b_mod.reshape(1, n))


def _ln_proj_kernel(x_ref, g_ref, sh_ref, sc_ref, w_ref, o_ref):
    x = x_ref[0]
    ms = jnp.mean(x * x, axis=-1, keepdims=True)
    y = x * lax.rsqrt(ms + NORM_EPS) * g_ref[...]
    h = y * (1.0 + sc_ref[0]) + sh_ref[0]
    o_ref[0] = _dot(h.astype(BF16), w_ref[...])


def ln_mod_proj(x, g, shift, scale, w):
    b, n, d = x.shape
    ncol = w.shape[1]
    tm = min(512, n)
    return pl.pallas_call(
        _ln_proj_kernel,
        grid=(b, n // tm),
        in_specs=[pl.BlockSpec((1, tm, d), lambda i, j: (i, j, 0)),
                  pl.BlockSpec((1, d), lambda i, j: (0, 0)),
                  pl.BlockSpec((1, 1, d), lambda i, j: (i, 0, 0)),
                  pl.BlockSpec((1, 1, d), lambda i, j: (i, 0, 0)),
                  pl.BlockSpec((d, ncol), lambda i, j: (0, 0))],
        out_specs=pl.BlockSpec((1, tm, ncol), lambda i, j: (i, j, 0)),
        out_shape=jax.ShapeDtypeStruct((b, n, ncol), F32),
        compiler_params=_cparams(("parallel", "arbitrary")),
        name="ln_mod_proj",
    )(x, g.reshape(1, d), shift, scale, w)


P_NAQ, P_NAK, P_NAV, P_CQ, P_CKV = 0, 256, 512, 768, 1024
P_DQ, P_DK, P_DV, P_GQ, P_GK, P_GV, P_KR = 1152, 1408, 1664, 1920, 2176, 2304, 2432
(G_NAQ, G_NAK, G_CQ, G_MQ, G_CKV, G_KN, G_KR, G_DQ, G_DK, G_GQ, G_GK) = range(11)
M_ONES, M_SEG64, M_SEG32 = 0, 1, 2
T_MLA, T_DIFF, T_GQA = 0, 3, 6


def _seg_norm(x, mat, inv_n, g):
    sq = x * x
    hi, lo = _split(sq)
    ms = (_dot(hi, mat) + _dot(lo, mat)) * inv_n
    return x * lax.rsqrt(ms + NORM_EPS) * g


def _rope(x, tab_ref, t0, half):
    c, s1, s2 = tab_ref[t0], tab_ref[t0 + 1], tab_ref[t0 + 2]
    return (x * c + pltpu.roll(x, LANES - half, axis=1) * s1
            + pltpu.roll(x, half, axis=1) * s2)


def _prep_kernel(p_ref, tab_ref, gain_ref, mat_ref, wuq_ref, wuk_ref, wuv_ref,
                 naq_ref, nak_ref, nav_ref, mq_ref, mk_ref, mv_ref,
                 dq_ref, dk_ref, dv_ref, gq_ref, gk_ref, gv_ref):
    lane = lax.broadcasted_iota(jnp.int32, (1, LANES), 1)
    lo_half = lane < 64
    ones_m, m64, m32 = mat_ref[M_ONES], mat_ref[M_SEG64], mat_ref[M_SEG32]

    def slab(off, i):
        return p_ref[0, :, off + i * LANES: off + (i + 1) * LANES]

    def gain(row, i):
        return gain_ref[row:row + 1, i * LANES:(i + 1) * LANES]

    s_na = HEAD_DIM ** -0.5 * LOG2E
    for i in range(2):
        q = _seg_norm(slab(P_NAQ, i), m64, 1.0 / 64, gain(G_NAQ, i)) * s_na
        naq_ref[0, :, (2 * i) * LANES:(2 * i + 1) * LANES] = jnp.where(lo_half, q, 0.0).astype(BF16)
        naq_ref[0, :, (2 * i + 1) * LANES:(2 * i + 2) * LANES] = jnp.where(lo_half, 0.0, q).astype(BF16)
        k = _seg_norm(slab(P_NAK, i), m64, 1.0 / 64, gain(G_NAK, i))
        nak_ref[0, :, i * LANES:(i + 1) * LANES] = k.astype(BF16)
        nav_ref[0, :, i * LANES:(i + 1) * LANES] = slab(P_NAV, i).astype(BF16)

    cq = p_ref[0, :, P_CQ:P_CQ + MLA_Q_RANK]
    cq = cq * lax.rsqrt(jnp.mean(cq * cq, axis=-1, keepdims=True) + NORM_EPS) * gain_ref[G_CQ:G_CQ + 1, :MLA_Q_RANK]
    uq = _dot(cq.astype(BF16), wuq_ref[...])
    s_mla = (MLA_NOPE_DIM + MLA_ROPE_DIM) ** -0.5 * LOG2E
    ckv = p_ref[0, :, P_CKV:P_CKV + MLA_KV_RANK]
    ckv = ckv * lax.rsqrt(jnp.mean(ckv * ckv, axis=-1, keepdims=True) + NORM_EPS) * gain_ref[G_CKV:G_CKV + 1, :MLA_KV_RANK]
    ckv_b = ckv.astype(BF16)
    uk = _dot(ckv_b, wuk_ref[...])
    mv_ref[0] = _dot(ckv_b, wuv_ref[...]).astype(BF16)
    kr = _seg_norm(slab(P_KR, 0), ones_m, 1.0 / MLA_ROPE_DIM, gain(G_KR, 0))
    kr = _rope(kr, tab_ref, T_MLA, 8)
    for h in range(N_HEADS):
        q = _seg_norm(uq[:, h * LANES:(h + 1) * LANES], ones_m, 1.0 / (MLA_NOPE_DIM + MLA_ROPE_DIM), gain(G_MQ, h))
        q = _rope(q, tab_ref, T_MLA, 8) * s_mla
        mq_ref[0, :, h * LANES:(h + 1) * LANES] = q.astype(BF16)
        kn = _seg_norm(uk[:, h * LANES:(h + 1) * LANES], ones_m, 1.0 / MLA_NOPE_DIM, gain(G_KN, h))
        mk_ref[0, :, h * LANES:(h + 1) * LANES] = (kn + kr).astype(BF16)

    s_d = DIFF_QK_DIM ** -0.5 * LOG2E
    seg = lane >> 5
    for i in range(2):
        q = _seg_norm(slab(P_DQ, i), m32, 1.0 / 32, gain(G_DQ, i))
        q = _rope(q, tab_ref, T_DIFF, 8) * s_d
        for j in range(4):
            dq_ref[0, :, (4 * i + j) * LANES:(4 * i + j + 1) * LANES] = jnp.where(seg == j, q, 0.0).astype(BF16)
        k = _seg_norm(slab(P_DK, i), m32, 1.0 / 32, gain(G_DK, i))
        dk_ref[0, :, i * LANES:(i + 1) * LANES] = _rope(k, tab_ref, T_DIFF, 8).astype(BF16)
        dv_ref[0, :, i * LANES:(i + 1) * LANES] = slab(P_DV, i).astype(BF16)

    s_g = HEAD_DIM ** -0.5 * LOG2E
    for i in range(2):
        q = _seg_norm(slab(P_GQ, i), m64, 1.0 / 64, gain(G_GQ, i))
        q = _rope(q, tab_ref, T_GQA, 16) * s_g
        gq_ref[0, :, (2 * i) * LANES:(2 * i + 1) * LANES] = jnp.where(lo_half, q, 0.0).astype(BF16)
        gq_ref[0, :, (2 * i + 1) * LANES:(2 * i + 2) * LANES] = jnp.where(lo_half, 0.0, q).astype(BF16)
    k = _seg_norm(slab(P_GK, 0), m64, 1.0 / 64, gain(G_GK, 0))
    gk_ref[0] = _rope(k, tab_ref, T_GQA, 16).astype(BF16)
    gv_ref[0] = slab(P_GV, 0).astype(BF16)


PREP_WIDTHS = (512, 256, 256, 512, 512, 256, 1024, 256, 256, 512, 128, 128)


def prep(p, tabs, gains, mats, wuq, wuk, wuv):
    b, n, _ = p.shape
    tm = min(512, n)
    const2 = lambda i, j: (0, 0)
    return pl.pallas_call(
        _prep_kernel,
        grid=(b, n // tm),
        in_specs=[pl.BlockSpec((1, tm, P_COLS), lambda i, j: (i, j, 0)),
                  pl.BlockSpec((9, tm, LANES), lambda i, j: (0, j, 0)),
                  pl.BlockSpec(gains.shape, const2),
                  pl.BlockSpec(mats.shape, lambda i, j: (0, 0, 0)),
                  pl.BlockSpec(wuq.shape, const2),
                  pl.BlockSpec(wuk.shape, const2),
                  pl.BlockSpec(wuv.shape, const2)],
        out_specs=[pl.BlockSpec((1, tm, w), lambda i, j: (i, j, 0)) for w in PREP_WIDTHS],
        out_shape=[jax.ShapeDtypeStruct((b, n, w), BF16) for w in PREP_WIDTHS],
        compiler_params=_cparams(("parallel", "arbitrary")),
        name="prep",
    )(p, tabs, gains, mats, wuq, wuk, wuv)


VT_ROWS = LANES + 16
KEY_CHUNK = 256
FLASH_TQ = 512
FLASH_TK_MAX = 3584


def _flash_kernel(q_ref, k_ref, vt_ref, o_ref, m_ref, l_ref, acc_ref, *, heads, n_out):
    ki = pl.program_id(2)

    @pl.when(ki == 0)
    def _():
        m_ref[...] = jnp.full(m_ref.shape, NEG_BIG, F32)
        l_ref[...] = jnp.zeros(l_ref.shape, F32)
        acc_ref[...] = jnp.zeros(acc_ref.shape, F32)

    for o in range(n_out):
        pair = [h for h in range(len(heads)) if heads[h][3] == o]
        for pos, h in enumerate(pair):
            qs, ks, vs, _ = heads[h]
            q = q_ref[0, :, qs * LANES:(qs + 1) * LANES]
            k = k_ref[0, :, ks * LANES:(ks + 1) * LANES]
            st = _dot_nt(k, q)
            lo, hi = pos * 64, (pos + 1) * 64
            m, l, acc = m_ref[h], l_ref[h], acc_ref[o, lo:hi, :]
            for c in range(st.shape[0] // KEY_CHUNK):
                rows = slice(c * KEY_CHUNK, (c + 1) * KEY_CHUNK)
                sc = st[rows]
                m_new = jnp.maximum(m, jnp.max(sc, axis=0, keepdims=True))
                a = jnp.exp2(m - m_new)
                pt = jnp.exp2(sc - m_new).astype(BF16)
                r = _dot(vt_ref[0, vs, :, rows], pt)
                l = a * l + r[LANES:LANES + 1]
                acc = acc * a + r[lo:hi]
                m = m_new
            m_ref[h], l_ref[h] = m, l
            acc_ref[o, lo:hi, :] = acc

    @pl.when(ki == pl.num_programs(2) - 1)
    def _():
        for o in range(n_out):
            pair = [h for h in range(len(heads)) if heads[h][3] == o]
            out_t = jnp.concatenate([acc_ref[o, 0:64, :] / l_ref[pair[0]],
                                     acc_ref[o, 64:128, :] / l_ref[pair[1]]], axis=0)
            o_ref[0, :, o * LANES:(o + 1) * LANES] = out_t.T


def _pick_tk(n_keys):
    best = 256
    for t in range(256, FLASH_TK_MAX + 1, 256):
        if n_keys % t == 0:
            best = t
    return best


def values_transposed(v):
    b, n, w = v.shape
    vt = jnp.transpose(v.reshape(b, n, w // LANES, LANES), (0, 2, 3, 1))
    return jnp.concatenate([vt, jnp.ones((b, w // LANES, VT_ROWS - LANES, n), v.dtype)], axis=2)


def flash_attention(q, k, vt, heads):
    b, s, qw = q.shape
    nk = k.shape[1]
    n_out = max(h[3] for h in heads) + 1
    tq = min(FLASH_TQ, s)
    tk = _pick_tk(nk)
    kern = functools.partial(_flash_kernel, heads=heads, n_out=n_out)
    return pl.pallas_call(
        kern,
        grid=(b, s // tq, nk // tk),
        in_specs=[pl.BlockSpec((1, tq, qw), lambda i, j, t: (i, j, 0)),
                  pl.BlockSpec((1, tk, k.shape[2]), lambda i, j, t: (i, t, 0)),
                  pl.BlockSpec((1, vt.shape[1], VT_ROWS, tk), lambda i, j, t: (i, 0, 0, t))],
        out_specs=pl.BlockSpec((1, tq, n_out * LANES), lambda i, j, t: (i, j, 0)),
        out_shape=jax.ShapeDtypeStruct((b, s, n_out * LANES), F32),
        scratch_shapes=[pltpu.VMEM((len(heads), 1, tq), F32),
                        pltpu.VMEM((len(heads), 1, tq), F32),
                        pltpu.VMEM((n_out, LANES, tq), F32)],
        compiler_params=_cparams(("parallel", "parallel", "arbitrary")),
        name="flash_attention",
    )(q, k, vt)


HEADS_PAIRED = tuple((h, h // 2, h // 2, h // 2) for h in range(4))
HEADS_MLA = tuple((h, h, h // 2, h // 2) for h in range(4))
HEADS_GQA = tuple((h, 0, 0, h // 2) for h in range(4))
HEADS_DIFF = tuple((j, j // 4, j // 4, (j % 2) * 2 + j // 4) for j in range(8))


NA_QROWS = 4
NA_TQ = NA_QROWS * GRID_W


def _na_kernel(q_ref, kp_ref, kc_ref, kn_ref, vp_ref, vc_ref, vn_ref, kx_ref, vx_ref, bias_ref, o_ref, *, rows):
    j = pl.program_id(1)
    kh = min(NA_WIN_H, rows)
    qi = lax.broadcasted_iota(jnp.int32, (NA_TQ, 1), 0)
    ki = lax.broadcasted_iota(jnp.int32, (1, NA_TQ), 1)
    wshift = GRID_W.bit_length() - 1
    r = j * NA_QROWS + (qi >> wshift)
    c = qi & (GRID_W - 1)
    r0 = jnp.clip(r - kh // 2, 0, rows - kh)
    c0 = jnp.clip(c - NA_WIN_W // 2, 0, GRID_W - NA_WIN_W)
    kcol = ki & (GRID_W - 1)
    col_ok = (kcol >= c0) & (kcol < c0 + NA_WIN_W)
    masks = []
    for d in (-1, 0, 1):
        kr = (j + d) * NA_QROWS + (ki >> wshift)
        masks.append(col_ok & (kr >= r0) & (kr < r0 + kh))
    lo_half = lax.broadcasted_iota(jnp.int32, (1, LANES), 1) < 64
    for o in range(2):
        k_loc = [kp_ref[0, :, o * LANES:(o + 1) * LANES], kc_ref[0, :, o * LANES:(o + 1) * LANES],
                 kn_ref[0, :, o * LANES:(o + 1) * LANES]]
        v_all = jnp.concatenate([vp_ref[0, :, o * LANES:(o + 1) * LANES], vc_ref[0, :, o * LANES:(o + 1) * LANES],
                                 vn_ref[0, :, o * LANES:(o + 1) * LANES], vx_ref[0, :, o * LANES:(o + 1) * LANES]],
                                axis=0)
        kx = kx_ref[0, :, o * LANES:(o + 1) * LANES]
        res = []
        for h in (2 * o, 2 * o + 1):
            q = q_ref[0, :, h * LANES:(h + 1) * LANES]
            parts = [jnp.where(masks[d], _dot_nt(q, k_loc[d]) + bias_ref[h, d], NEG_BIG) for d in range(3)]
            parts.append(_dot_nt(q, kx))
            s = jnp.concatenate(parts, axis=1)
            m = jnp.max(s, axis=-1, keepdims=True)
            p = jnp.exp2(s - m)
            l = jnp.sum(p, axis=-1, keepdims=True)
            res.append(_dot(p.astype(BF16), v_all) / l)
        o_ref[0, :, o * LANES:(o + 1) * LANES] = jnp.where(lo_half, res[0], res[1])


def na_attention(q, k, v, kx, vx, bias, rows):
    b, s, _ = q.shape
    nblk = rows // NA_QROWS
    nx = kx.shape[1]
    kern = functools.partial(_na_kernel, rows=rows)
    prev = lambda i, j: (i, jnp.maximum(j - 1, 0), 0)
    cur = lambda i, j: (i, j, 0)
    nxt = lambda i, j: (i, jnp.minimum(j + 1, nblk - 1), 0)
    kv_spec = lambda f: pl.BlockSpec((1, NA_TQ, 2 * LANES), f)
    return pl.pallas_call(
        kern,
        grid=(b, nblk),
        in_specs=[pl.BlockSpec((1, NA_TQ, 4 * LANES), cur),
                  kv_spec(prev), kv_spec(cur), kv_spec(nxt),
                  kv_spec(prev), kv_spec(cur), kv_spec(nxt),
                  pl.BlockSpec((1, nx, 2 * LANES), lambda i, j: (i, 0, 0)),
                  pl.BlockSpec((1, nx, 2 * LANES), lambda i, j: (i, 0, 0)),
                  pl.BlockSpec(bias.shape, lambda i, j: (0, 0, 0, 0))],
        out_specs=pl.BlockSpec((1, NA_TQ, 2 * LANES), cur),
        out_shape=jax.ShapeDtypeStruct((b, s, 2 * LANES), F32),
        compiler_params=_cparams(("parallel", "arbitrary")),
        name="na_attention",
    )(q, k, k, k, v, v, v, kx, vx, bias)


ROUTE_T = 256


def _outproj_kernel(x_ref, ona_ref, omla_ref, od_ref, ogqa_ref, mod_ref, gsub_ref, gffn_ref, lam_ref, mat_ref,
                    wout_ref, wr_ref, xo_ref, h_ref, lg_ref, *, lam_init):
    lam = lam_ref[0, 0]
    m64 = mat_ref[M_SEG64]
    pieces = [ona_ref[0].astype(BF16), omla_ref[0].astype(BF16)]
    dsl = []
    for i in range(2):
        d = od_ref[0, :, i * LANES:(i + 1) * LANES] - lam * od_ref[0, :, (2 + i) * LANES:(3 + i) * LANES]
        d = _seg_norm(d, m64, 1.0 / 64, gsub_ref[...]) * (1.0 - lam_init)
        dsl.append(d.astype(BF16))
    pieces += dsl + [ogqa_ref[0].astype(BF16)]
    o = jnp.concatenate(pieces, axis=1)
    y = _dot(o, wout_ref[...])
    x = x_ref[0] + mod_ref[0, 2:3, :] * y
    xo_ref[0] = x
    ms = jnp.mean(x * x, axis=-1, keepdims=True)
    h = x * lax.rsqrt(ms + NORM_EPS) * gffn_ref[...]
    h = h * (1.0 + mod_ref[0, 4:5, :]) + mod_ref[0, 3:4, :]
    h_ref[0] = h.astype(BF16)
    lg = _dot3_nt(wr_ref[...], h)
    for t in range(lg.shape[1] // ROUTE_T):
        lg_ref[0, t] = lg[:, t * ROUTE_T:(t + 1) * ROUTE_T]


def outproj_mod_router(x, o_na, o_mla, o_diff, o_gqa, mod, g_sub_t, g_ffn, lam, mats, w_out, w_router_t, lam_init):
    b, n, d = x.shape
    tm = min(512, n)
    nt = tm // ROUTE_T
    kern = functools.partial(_outproj_kernel, lam_init=lam_init)
    tok = lambda w: pl.BlockSpec((1, tm, w), lambda i, j: (i, j, 0))
    c2 = lambda i, j: (0, 0)
    return pl.pallas_call(
        kern,
        grid=(b, n // tm),
        in_specs=[tok(d), tok(256), tok(256), tok(512), tok(256),
                  pl.BlockSpec((1, 8, d), lambda i, j: (i, 0, 0)),
                  pl.BlockSpec((1, LANES), c2), pl.BlockSpec((1, d), c2),
                  pl.BlockSpec(memory_space=pltpu.SMEM),
                  pl.BlockSpec(mats.shape, lambda i, j: (0, 0, 0)),
                  pl.BlockSpec((d, d), c2), pl.BlockSpec((N_EXPERTS, d), c2)],
        out_specs=[tok(d), tok(d),
                   pl.BlockSpec((1, nt, N_EXPERTS, ROUTE_T), lambda i, j: (i, j, 0, 0))],
        out_shape=[jax.ShapeDtypeStruct((b, n, d), F32), jax.ShapeDtypeStruct((b, n, d), BF16),
                   jax.ShapeDtypeStruct((b, n // ROUTE_T, N_EXPERTS, ROUTE_T), F32)],
        compiler_params=_cparams(("parallel", "arbitrary")),
        name="outproj_mod_router",
    )(x, o_na, o_mla, o_diff, o_gqa, mod, g_sub_t, g_ffn.reshape(1, d), lam, mats, w_out, w_router_t)


def _route_kernel(lg_ref, tri_ref, aff_ref, pos_ref, off_ref, *, cap):
    nb = lg_ref.shape[1]
    lg = lg_ref[0]
    mx = jnp.max(lg, axis=1, keepdims=True)
    ex = jnp.exp(lg - mx)
    aff = ex / jnp.sum(ex, axis=1, keepdims=True)
    aff_ref[0] = aff
    bits = lax.bitcast_convert_type(aff, jnp.int32)

    def count_ge(t):
        hit = jnp.where(bits >= t[None], 1.0, 0.0)
        return jnp.sum(jnp.sum(hit, axis=0), axis=1, keepdims=True)

    def bis(i, t):
        cand = t | (jnp.int32(1) << (30 - i))
        return jnp.where(count_ge(cand) >= float(cap), cand, t)

    thr = lax.fori_loop(0, 31, bis, jnp.zeros((N_EXPERTS, 1), jnp.int32))
    need = float(cap) - count_ge(thr + 1)
    tri = tri_ref[...]

    def blk(jb, carry):
        c_eq, c_pos = carry
        bb = lax.bitcast_convert_type(aff_ref[0, jb], jnp.int32)
        gt = bb > thr
        eq = bb == thr
        eq_before = _dot(jnp.where(eq, 1.0, 0.0).astype(BF16), tri) + c_eq
        sel = gt | (eq & (eq_before < need))
        sel_f = jnp.where(sel, 1.0, 0.0)
        before = _dot(sel_f.astype(BF16), tri) + c_pos
        pos_ref[0, jb] = jnp.where(sel, before, -1.0).astype(jnp.int32)
        off_ref[0, jb] = jnp.broadcast_to(c_pos, (N_EXPERTS, LANES)).astype(jnp.int32)
        c_eq = c_eq + jnp.sum(jnp.where(eq, 1.0, 0.0), axis=1, keepdims=True)
        c_pos = c_pos + jnp.sum(sel_f, axis=1, keepdims=True)
        return c_eq, c_pos

    zero = jnp.zeros((N_EXPERTS, 1), F32)
    lax.fori_loop(0, nb, blk, (zero, zero))


def route(logits, tri, cap):
    b, nb, e, t = logits.shape
    kern = functools.partial(_route_kernel, cap=cap)
    spec = pl.BlockSpec((1, nb, e, t), lambda i: (i, 0, 0, 0))
    return pl.pallas_call(
        kern,
        grid=(b,),
        in_specs=[spec, pl.BlockSpec(tri.shape, lambda i: (0, 0))],
        out_specs=[spec, spec, pl.BlockSpec((1, nb, e, LANES), lambda i: (i, 0, 0, 0))],
        out_shape=[jax.ShapeDtypeStruct((b, nb, e, t), F32), jax.ShapeDtypeStruct((b, nb, e, t), jnp.int32),
                   jax.ShapeDtypeStruct((b, nb, e, LANES), jnp.int32)],
        compiler_params=_cparams(("arbitrary",)),
        name="route",
    )(logits, tri)


def _gather_kernel(off_ref, h_ref, pos_ref, aff_ref, xe_ref, gs_ref, acc_ref, gacc_ref, *, st, n_tiles, nsub, nb):
    bi, e, ch = pl.program_id(0), pl.program_id(1), pl.program_id(2)

    @pl.when(ch == 0)
    def _():
        acc_ref[...] = jnp.zeros(acc_ref.shape, F32)
        gacc_ref[...] = jnp.zeros(gacc_ref.shape, F32)

    slot = lax.broadcasted_iota(jnp.int32, (st, ROUTE_T), 0)
    for jj in range(nsub):
        jb = ch * nsub + jj
        a = off_ref[(bi * N_EXPERTS + e) * nb + jb] // st
        prow = pos_ref[0, jj, pl.ds(e, 1), :]
        arow = aff_ref[0, jj, pl.ds(e, 1), :]
        hs = h_ref[0, jj * ROUTE_T:(jj + 1) * ROUTE_T, :]
        for w in range(2):
            tile = jnp.minimum(a + w, n_tiles - 1)
            hit = slot == (prow - (a + w) * st)
            base = pl.multiple_of(tile * st, st)
            acc_ref[pl.ds(base, st), :] += _dot(jnp.where(hit, 1.0, 0.0).astype(BF16), hs)
            gacc_ref[pl.ds(base, st), :] += jnp.sum(jnp.where(hit, arow, 0.0), axis=1, keepdims=True)

    @pl.when(ch == pl.num_programs(2) - 1)
    def _():
        xe_ref[0, 0] = acc_ref[...].astype(BF16)
        gs_ref[0, 0] = gacc_ref[...]


def moe_gather(offs, h, pos, aff, cap):
    b, n, d = h.shape
    nb = n // ROUTE_T
    st = min(ROUTE_T, cap)
    n_tiles = cap // st
    chunk = min(2048, n)
    nsub = chunk // ROUTE_T
    kern = functools.partial(_gather_kernel, st=st, n_tiles=n_tiles, nsub=nsub, nb=nb)
    rspec = pl.BlockSpec((1, nsub, N_EXPERTS, ROUTE_T), lambda i, e, c, off: (i, c, 0, 0))
    return pl.pallas_call(
        kern,
        grid_spec=pltpu.PrefetchScalarGridSpec(
            num_scalar_prefetch=1,
            grid=(b, N_EXPERTS, n // chunk),
            in_specs=[pl.BlockSpec((1, chunk, d), lambda i, e, c, off: (i, c, 0)), rspec, rspec],
            out_specs=[pl.BlockSpec((1, 1, cap, d), lambda i, e, c, off: (i, e, 0, 0)),
                       pl.BlockSpec((1, 1, cap, 1), lambda i, e, c, off: (i, e, 0, 0))],
            scratch_shapes=[pltpu.VMEM((cap, d), F32), pltpu.VMEM((cap, 1), F32)]),
        out_shape=[jax.ShapeDtypeStruct((b, N_EXPERTS, cap, d), BF16),
                   jax.ShapeDtypeStruct((b, N_EXPERTS, cap, 1), F32)],
        compiler_params=_cparams(("parallel", "parallel", "arbitrary")),
        name="moe_gather",
    )(offs, h, pos, aff)


FF_TILE = 256


def _ffn_kernel(x_ref, g_ref, wg_ref, wu_ref, wd_ref, y_ref, acc_ref):
    f = pl.program_id(2)

    @pl.when(f == 0)
    def _():
        acc_ref[...] = jnp.zeros(acc_ref.shape, F32)

    x = x_ref[0, 0]
    gate = _dot(x, wg_ref[0].astype(BF16))
    up = _dot(x, wu_ref[0].astype(BF16))
    hmid = (_silu(gate) * up).astype(BF16)
    acc_ref[...] += _dot(hmid, wd_ref[0].astype(BF16))

    @pl.when(f == pl.num_programs(2) - 1)
    def _():
        y_ref[0, 0] = (acc_ref[...] * g_ref[0, 0]).astype(BF16)


def moe_ffn(xe, gs, w_gate, w_up, w_down):
    b, e, cap, d = xe.shape
    dff = w_gate.shape[2]
    return pl.pallas_call(
        _ffn_kernel,
        grid=(e, b, dff // FF_TILE),
        in_specs=[pl.BlockSpec((1, 1, cap, d), lambda ei, bi, f: (bi, ei, 0, 0)),
                  pl.BlockSpec((1, 1, cap, 1), lambda ei, bi, f: (bi, ei, 0, 0)),
                  pl.BlockSpec((1, d, FF_TILE), lambda ei, bi, f: (ei, 0, f)),
                  pl.BlockSpec((1, d, FF_TILE), lambda ei, bi, f: (ei, 0, f)),
                  pl.BlockSpec((1, FF_TILE, d), lambda ei, bi, f: (ei, f, 0))],
        out_specs=pl.BlockSpec((1, 1, cap, d), lambda ei, bi, f: (bi, ei, 0, 0)),
        out_shape=jax.ShapeDtypeStruct((b, e, cap, d), BF16),
        scratch_shapes=[pltpu.VMEM((cap, d), F32)],
        compiler_params=_cparams(("parallel", "parallel", "arbitrary")),
        name="moe_ffn",
    )(xe, gs, w_gate, w_up, w_down)


def _combine_kernel(off_ref, x_ref, ye_ref, pos_ref, mod_ref, o_ref, *, st, n_tiles, nsub, nb):
    bi, ch, e = pl.program_id(0), pl.program_id(1), pl.program_id(2)

    @pl.when(e == 0)
    def _():
        o_ref[...] = jnp.zeros(o_ref.shape, F32)

    slot = lax.broadcasted_iota(jnp.int32, (st, ROUTE_T), 0)
    for jj in range(nsub):
        jb = ch * nsub + jj
        a = off_ref[(bi * N_EXPERTS + e) * nb + jb] // st
        prow = pos_ref[0, jj, pl.ds(e, 1), :]
        tot = jnp.zeros((ROUTE_T, o_ref.shape[2]), F32)
        for w in range(2):
            tile = jnp.minimum(a + w, n_tiles - 1)
            hit = slot == (prow - (a + w) * st)
            base = pl.multiple_of(tile * st, st)
            oh = jnp.where(hit, 1.0, 0.0).astype(BF16)
            tot = tot + lax.dot_general(oh, ye_ref[0, 0, pl.ds(base, st), :], (((0,), (0,)), ((), ())),
                                        preferred_element_type=F32)
        o_ref[0, jj * ROUTE_T:(jj + 1) * ROUTE_T, :] += tot

    @pl.when(e == pl.num_programs(2) - 1)
    def _():
        o_ref[0] = x_ref[0] + mod_ref[0, 5:6, :] * o_ref[0]


def moe_combine(offs, x, ye, pos, mod, cap):
    b, n, d = x.shape
    nb = n // ROUTE_T
    st = min(ROUTE_T, cap)
    n_tiles = cap // st
    chunk = min(2048, n)
    nsub = chunk // ROUTE_T
    kern = functools.partial(_combine_kernel, st=st, n_tiles=n_tiles, nsub=nsub, nb=nb)
    return pl.pallas_call(
        kern,
        grid_spec=pltpu.PrefetchScalarGridSpec(
            num_scalar_prefetch=1,
            grid=(b, n // chunk, N_EXPERTS),
            in_specs=[pl.BlockSpec((1, chunk, d), lambda i, c, e, off: (i, c, 0)),
                      pl.BlockSpec((1, 1, cap, d), lambda i, c, e, off: (i, e, 0, 0)),
                      pl.BlockSpec((1, nsub, N_EXPERTS, ROUTE_T), lambda i, c, e, off: (i, c, 0, 0)),
                      pl.BlockSpec((1, 8, d), lambda i, c, e, off: (i, 0, 0))],
            out_specs=pl.BlockSpec((1, chunk, d), lambda i, c, e, off: (i, c, 0))),
        out_shape=jax.ShapeDtypeStruct((b, n, d), F32),
        compiler_params=_cparams(("parallel", "parallel", "arbitrary")),
        name="moe_combine",
    )(offs, x, ye, pos, mod)


def expert_choice_ffn(x, h, logits, mod, tri, w_gate, w_up, w_down):
    b, n, _ = x.shape
    cap = EC_CAPACITY * n // N_EXPERTS
    aff, pos, off = route(logits, tri, cap)
    offs = jnp.transpose(off[..., 0], (0, 2, 1)).reshape(-1)
    xe, gs = moe_gather(offs, h, pos, aff, cap)
    ye = moe_ffn(xe, gs, w_gate, w_up, w_down)
    return moe_combine(offs, x, ye, pos, mod, cap)


def _rope_tables(n_rows_grid):
    s = n_rows_grid * GRID_W
    t = np.arange(s)
    row, col = (t // GRID_W).astype(np.float64), (t % GRID_W).astype(np.float64)

    def unit(n):
        half = n // 2
        inv = ROPE_THETA ** (-np.arange(0, n, 2, dtype=np.float64) / n)
        out = []
        for pos in (row, col):
            ang = pos[:, None] * inv[None, :]
            c, sn = np.cos(ang), np.sin(ang)
            z = np.zeros_like(sn)
            out.append((np.concatenate([c, c], 1), np.concatenate([-sn, z], 1), np.concatenate([z, sn], 1)))
        return [np.concatenate([out[0][i], out[1][i]], 1) for i in range(3)]

    ident = lambda w: (np.ones((s, w), np.float32), np.zeros((s, w), np.float32), np.zeros((s, w), np.float32))
    u16 = unit(16)
    u32 = unit(32)
    tabs = []
    idt = ident(64)
    idt32 = ident(32)
    for i in range(3):
        tabs.append(np.concatenate([idt[i], u16[i], idt32[i]], 1))
    for i in range(3):
        tabs.append(np.concatenate([u16[i]] * 4, 1))
    for i in range(3):
        tabs.append(np.concatenate([u32[i]] * 2, 1))
    return np.stack(tabs).astype(np.float32)


def _identity_tables(n):
    one, zero = np.ones((n, LANES), np.float32), np.zeros((n, LANES), np.float32)
    return np.stack([one, zero, zero] * 3)


def _seg_matrices():
    i = np.arange(LANES)
    ones = np.ones((LANES, LANES), np.float32)
    m64 = (i[:, None] // 64 == i[None, :] // 64).astype(np.float32)
    m32 = (i[:, None] // 32 == i[None, :] // 32).astype(np.float32)
    return np.stack([ones, m64, m32])


def _na_bias(rpb):
    c = np.arange(GRID_W)
    rl = np.arange(NA_QROWS)
    dc = np.clip(c[None, :] - c[:, None] + NA_WIN_W - 1, 0, 2 * NA_WIN_W - 2)
    d = np.array([-1, 0, 1])
    dr = np.clip(NA_QROWS * d[:, None, None] + rl[None, None, :] - rl[None, :, None] + NA_WIN_H - 1,
                 0, 2 * NA_WIN_H - 2)
    cols = jnp.take(rpb.astype(F32) * LOG2E, jnp.asarray(dc), axis=2)
    full = jnp.take(cols, jnp.asarray(dr), axis=1)
    return jnp.transpose(full, (0, 1, 2, 4, 3, 5)).reshape(rpb.shape[0], 3, NA_TQ, NA_TQ)


def _pad_cols(w, width):
    return jnp.pad(w, ((0, 0), (0, width - w.shape[1])))


def _layer_params(l, w_in, g_na_q, g_na_k, na_rpb, g_mla_cq, w_mla_uq, g_mla_q, g_mla_ckv, w_mla_ukv,
                  g_mla_k_nope, g_mla_k_rope, g_diff_q, g_diff_k, g_diff_sub, g_gqa_q, g_gqa_k, w_out):
    wi = w_in[l]
    (naq, nak, nav, cq, ckv, kr, dq, dk, dv, gq, gk, gv) = jnp.split(
        wi, np.cumsum([256, 256, 256, 256, 128, 32, 256, 256, 256, 256, 128])[:], axis=1)
    gq4 = gq.reshape(-1, 4, 64)[:, jnp.array([0, 2, 1, 3])].reshape(-1, 256)
    zeros = lambda w: jnp.zeros((wi.shape[0], w), wi.dtype)
    w_in_r = jnp.concatenate([naq, nak, nav, cq, ckv, dq, dk, dv, gq4, gk, gv, zeros(64), kr, zeros(32)],
                             axis=1).astype(BF16)
    uq = w_mla_uq[l].reshape(MLA_Q_RANK, 4, 96)
    wuq = jnp.pad(uq, ((0, 0), (0, 0), (0, 32))).reshape(MLA_Q_RANK, 512).astype(BF16)
    ukv = w_mla_ukv[l].reshape(MLA_KV_RANK, 4, 128)
    wuk = jnp.pad(ukv[:, :, :64], ((0, 0), (0, 0), (0, 64))).reshape(MLA_KV_RANK, 512).astype(BF16)
    wuv = ukv[:, :, 64:].reshape(MLA_KV_RANK, 256).astype(BF16)
    row = lambda v: jnp.pad(v, (0, 512 - v.shape[0]))
    z32, z64 = jnp.zeros((32,), F32), jnp.zeros((64,), F32)
    gains = jnp.stack([
        row(jnp.tile(g_na_q[l], 4)), row(jnp.tile(g_na_k[l], 4)), row(g_mla_cq[l]),
        row(jnp.tile(jnp.concatenate([g_mla_q[l], z32]), 4)), row(g_mla_ckv[l]),
        row(jnp.tile(jnp.concatenate([g_mla_k_nope[l], z64]), 4)),
        row(jnp.concatenate([z64, g_mla_k_rope[l], z32])),
        row(jnp.tile(g_diff_q[l].reshape(-1), 4)), row(jnp.tile(g_diff_k[l].reshape(-1), 4)),
        row(jnp.tile(g_gqa_q[l], 4)), row(jnp.tile(g_gqa_k[l], 2))] + [jnp.zeros((512,), F32)] * 5)
    wo = w_out[l]
    wo_g = wo[768:].reshape(4, 64, -1)[jnp.array([0, 2, 1, 3])].reshape(256, -1)
    w_out_r = jnp.concatenate([wo[:768], wo_g], axis=0).astype(BF16)
    return dict(w_in=w_in_r, wuq=wuq, wuk=wuk, wuv=wuv, gains=gains, w_out=w_out_r,
                bias=_na_bias(na_rpb[l]), g_sub=jnp.tile(g_diff_sub[l], 2).reshape(1, LANES))


def kernel(x, c, ctx, c_ctx, w_mod, b_mod, g_attn, g_ffn, w_in, g_na_q, g_na_k, na_rpb, g_mla_cq, w_mla_uq, g_mla_q, g_mla_ckv, w_mla_ukv, g_mla_k_nope, g_mla_k_rope, g_diff_q, g_diff_k, diff_lambda, g_diff_sub, g_gqa_q, g_gqa_k, w_out, w_router, w_gate, w_up, w_down):
    b, s, d = x.shape
    n_ctx = ctx.shape[1]
    rows = s // GRID_W
    tabs = jnp.asarray(_rope_tables(rows))
    tabs_ctx = jnp.asarray(_identity_tables(n_ctx))
    mats = jnp.asarray(_seg_matrices()).astype(BF16)
    tri = jnp.asarray(np.triu(np.ones((ROUTE_T, ROUTE_T), np.float32), 1)).astype(BF16)
    c_rows = jnp.concatenate([c, c_ctx[None], jnp.zeros((8 - b - 1, d), F32)], axis=0)
    xc = ctx
    for l in range(DEPTH):
        need_ctx = l < DEPTH - 1
        lam_init = 0.8 - 0.6 * math.exp(-0.3 * l)
        lp = diff_lambda[l].astype(F32)
        lam = (jnp.exp(jnp.sum(lp[0] * lp[1])) - jnp.exp(jnp.sum(lp[2] * lp[3])) + lam_init).reshape(1, 1)
        prm = _layer_params(l, w_in, g_na_q, g_na_k, na_rpb, g_mla_cq, w_mla_uq, g_mla_q, g_mla_ckv, w_mla_ukv,
                            g_mla_k_nope, g_mla_k_rope, g_diff_q, g_diff_k, g_diff_sub, g_gqa_q, g_gqa_k, w_out)
        modv = mod_vectors(c_rows, w_mod[l], b_mod[l]).reshape(8, N_MOD, d)
        mod = jnp.pad(modv[:b], ((0, 0), (0, 2), (0, 0)))
        mod_c = jnp.broadcast_to(jnp.pad(modv[b:b + 1], ((0, 0), (0, 2), (0, 0))), (b, 8, d))
        w_router_t = w_router[l].T

        def mix_inputs(xin, m, tb):
            p = ln_mod_proj(xin, g_attn[l], m[:, 0:1], m[:, 1:2], prm["w_in"])
            return prep(p, tb, prm["gains"], mats, prm["wuq"], prm["wuk"], prm["wuv"])

        (naq, nak, nav, mq, mk, mv, dq, dk, dv, gq, gk, gv) = mix_inputs(x, mod, tabs)
        (naq_c, nak_c, nav_c, mq_c, mk_c, mv_c, dq_c, dk_c, dv_c, gq_c, gk_c, gv_c) = mix_inputs(xc, mod_c, tabs_ctx)
        cat = lambda a, bb: jnp.concatenate([a, bb], axis=1)

        ext = values_transposed
        o_na = na_attention(naq, nak, nav, nak_c, nav_c, prm["bias"], rows)
        o_mla = flash_attention(mq, cat(mk_c, mk), ext(cat(mv_c, mv)), HEADS_MLA)
        o_diff = flash_attention(dq, cat(dk_c, dk), ext(cat(dv_c, dv)), HEADS_DIFF)
        o_gqa = flash_attention(gq, cat(gk_c, gk), ext(cat(gv_c, gv)), HEADS_GQA)
        x_mid, h2, logits = outproj_mod_router(x, o_na, o_mla, o_diff, o_gqa, mod, prm["g_sub"], g_ffn[l], lam,
                                               mats, prm["w_out"], w_router_t, lam_init)
        if need_ctx:
            oc_na = flash_attention(naq_c, nak_c, ext(nav_c), HEADS_PAIRED)
            oc_mla = flash_attention(mq_c, mk_c, ext(mv_c), HEADS_MLA)
            oc_diff = flash_attention(dq_c, dk_c, ext(dv_c), HEADS_DIFF)
            oc_gqa = flash_attention(gq_c, gk_c, ext(gv_c), HEADS_GQA)
            xc_mid, hc2, logits_c = outproj_mod_router(xc, oc_na, oc_mla, oc_diff, oc_gqa, mod_c, prm["g_sub"],
                                                       g_ffn[l], lam, mats, prm["w_out"], w_router_t, lam_init)
            xc = expert_choice_ffn(xc_mid, hc2, logits_c, mod_c, tri, w_gate[l], w_up[l], w_down[l])
        x = expert_choice_ffn(x_mid, h2, logits, mod, tri, w_gate[l], w_up[l], w_down[l])
    return x
```

```python
import functools
import math

import numpy as np
import jax
import jax.numpy as jnp
from jax import lax
from jax.experimental import pallas as pl
from jax.experimental.pallas import tpu as pltpu

D_MODEL = 1024
GRID_W = 64
HEAD_DIM = 64
N_HEADS = 4
NA_WIN_H = 8
NA_WIN_W = 16
MLA_Q_RANK = 256
MLA_KV_RANK = 128
MLA_NOPE_DIM = 64
MLA_ROPE_DIM = 32
MLA_V_DIM = 64
DIFF_QK_DIM = 32
N_EXPERTS = 16
EC_CAPACITY = 2
D_FF = 2816
ROPE_THETA = 10000.0
NORM_EPS = 1e-6
N_MOD = 6
DEPTH = 2

LANES = 128
P_COLS = 2560
VMEM_LIMIT = 56 * 1024 * 1024
NEG_BIG = -1e30
LOG2E = math.log2(math.e)

F32 = jnp.float32
BF16 = jnp.bfloat16


def _cparams(sem):
    return pltpu.CompilerParams(dimension_semantics=sem, vmem_limit_bytes=VMEM_LIMIT)


def _split(a):
    hi = a.astype(BF16)
    lo = (a - hi.astype(F32)).astype(BF16)
    return hi, lo


def _dot(a, b):
    return jnp.dot(a, b, preferred_element_type=F32)


def _dot_nt(a, b):
    return lax.dot_general(a, b, (((1,), (1,)), ((), ())), preferred_element_type=F32)


def _dot3(a, b):
    ah, al = _split(a)
    bh, bl = _split(b)
    return _dot(ah, bh) + _dot(ah, bl) + _dot(al, bh)


def _dot3_nt(a, b):
    ah, al = _split(a)
    bh, bl = _split(b)
    return _dot_nt(ah, bh) + _dot_nt(ah, bl) + _dot_nt(al, bh)


def _silu(v):
    return v * jax.nn.sigmoid(v)


def _mod_kernel(c_ref, w_ref, b_ref, o_ref):
    o_ref[...] = _dot3(_silu(c_ref[...]), w_ref[...]) + b_ref[...]


def mod_vectors(c_rows, w_mod, b_mod):
    m, d = c_rows.shape
    n = w_mod.shape[1]
    tn = 1536
    return pl.pallas_call(
        _mod_kernel,
        grid=(n // tn,),
        in_specs=[pl.BlockSpec((m, d), lambda j: (0, 0)),
                  pl.BlockSpec((d, tn), lambda j: (0, j)),
                  pl.BlockSpec((1, tn), lambda j: (0, j))],
        out_specs=pl.BlockSpec((m, tn), lambda j: (0, j)),
        out_shape=jax.ShapeDtypeStruct((m, n), F32),
        compiler_params=_cparams(("arbitrary",)),
        name="mod_vectors",
    )(c_rows, w_mod, b_mod.reshape(1, n))


def _ln_proj_kernel(x_ref, g_ref, sh_ref, sc_ref, w_ref, o_ref):
    x = x_ref[0]
    ms = jnp.mean(x * x, axis=-1, keepdims=True)
    y = x * lax.rsqrt(ms + NORM_EPS) * g_ref[...]
    h = y * (1.0 + sc_ref[0]) + sh_ref[0]
    o_ref[0] = _dot(h.astype(BF16), w_ref[...])


def ln_mod_proj(x, g, shift, scale, w):
    b, n, d = x.shape
    ncol = w.shape[1]
    tm = min(512, n)
    return pl.pallas_call(
        _ln_proj_kernel,
        grid=(b, n // tm),
        in_specs=[pl.BlockSpec((1, tm, d), lambda i, j: (i, j, 0)),
                  pl.BlockSpec((1, d), lambda i, j: (0, 0)),
                  pl.BlockSpec((1, 1, d), lambda i, j: (i, 0, 0)),
                  pl.BlockSpec((1, 1, d), lambda i, j: (i, 0, 0)),
                  pl.BlockSpec((d, ncol), lambda i, j: (0, 0))],
        out_specs=pl.BlockSpec((1, tm, ncol), lambda i, j: (i, j, 0)),
        out_shape=jax.ShapeDtypeStruct((b, n, ncol), F32),
        compiler_params=_cparams(("parallel", "arbitrary")),
        name="ln_mod_proj",
    )(x, g.reshape(1, d), shift, scale, w)


P_NAQ, P_NAK, P_NAV, P_CQ, P_CKV = 0, 256, 512, 768, 1024
P_DQ, P_DK, P_DV, P_GQ, P_GK, P_GV, P_KR = 1152, 1408, 1664, 1920, 2176, 2304, 2432
(G_NAQ, G_NAK, G_CQ, G_MQ, G_CKV, G_KN, G_KR, G_DQ, G_DK, G_GQ, G_GK) = range(11)
M_ONES, M_SEG64, M_SEG32 = 0, 1, 2
T_MLA, T_DIFF, T_GQA = 0, 3, 6


def _seg_norm(x, mat, inv_n, g):
    sq = x * x
    hi, lo = _split(sq)
    ms = (_dot(hi, mat) + _dot(lo, mat)) * inv_n
    return x * lax.rsqrt(ms + NORM_EPS) * g


def _rope(x, tab_ref, t0, half):
    c, s1, s2 = tab_ref[t0], tab_ref[t0 + 1], tab_ref[t0 + 2]
    return (x * c + pltpu.roll(x, LANES - half, axis=1) * s1
            + pltpu.roll(x, half, axis=1) * s2)


def _prep_kernel(p_ref, tab_ref, gain_ref, mat_ref, wuq_ref, wuk_ref, wuv_ref,
                 naq_ref, nak_ref, nav_ref, mq_ref, mk_ref, mv_ref,
                 dq_ref, dk_ref, dv_ref, gq_ref, gk_ref, gv_ref, st_ref):
    lane = lax.broadcasted_iota(jnp.int32, (1, LANES), 1)
    lo_half = lane < 64
    ones_m, m64, m32 = mat_ref[M_ONES], mat_ref[M_SEG64], mat_ref[M_SEG32]
    norm2 = {}

    def slab(off, i):
        return p_ref[0, :, off + i * LANES: off + (i + 1) * LANES]

    def track(name, xb, mat):
        xf = xb.astype(F32)
        hi, lo = _split(xf * xf)
        cur = jnp.max(_dot(hi, mat) + _dot(lo, mat), axis=0, keepdims=True)
        norm2[name] = jnp.maximum(norm2[name], cur) if name in norm2 else cur
        return xb

    def gain(row, i):
        return gain_ref[row:row + 1, i * LANES:(i + 1) * LANES]

    s_na = HEAD_DIM ** -0.5 * LOG2E
    for i in range(2):
        q = _seg_norm(slab(P_NAQ, i), m64, 1.0 / 64, gain(G_NAQ, i)) * s_na
        naq_ref[0, :, (2 * i) * LANES:(2 * i + 1) * LANES] = jnp.where(lo_half, q, 0.0).astype(BF16)
        naq_ref[0, :, (2 * i + 1) * LANES:(2 * i + 2) * LANES] = jnp.where(lo_half, 0.0, q).astype(BF16)
        k = _seg_norm(slab(P_NAK, i), m64, 1.0 / 64, gain(G_NAK, i))
        nak_ref[0, :, i * LANES:(i + 1) * LANES] = k.astype(BF16)
        nav_ref[0, :, i * LANES:(i + 1) * LANES] = slab(P_NAV, i).astype(BF16)

    cq = p_ref[0, :, P_CQ:P_CQ + MLA_Q_RANK]
    cq = cq * lax.rsqrt(jnp.mean(cq * cq, axis=-1, keepdims=True) + NORM_EPS) * gain_ref[G_CQ:G_CQ + 1, :MLA_Q_RANK]
    uq = _dot(cq.astype(BF16), wuq_ref[...])
    s_mla = (MLA_NOPE_DIM + MLA_ROPE_DIM) ** -0.5 * LOG2E
    ckv = p_ref[0, :, P_CKV:P_CKV + MLA_KV_RANK]
    ckv = ckv * lax.rsqrt(jnp.mean(ckv * ckv, axis=-1, keepdims=True) + NORM_EPS) * gain_ref[G_CKV:G_CKV + 1, :MLA_KV_RANK]
    ckv_b = ckv.astype(BF16)
    uk = _dot(ckv_b, wuk_ref[...])
    mv_ref[0] = _dot(ckv_b, wuv_ref[...]).astype(BF16)
    kr = _seg_norm(slab(P_KR, 0), ones_m, 1.0 / MLA_ROPE_DIM, gain(G_KR, 0))
    kr = _rope(kr, tab_ref, T_MLA, 8)
    for h in range(N_HEADS):
        q = _seg_norm(uq[:, h * LANES:(h + 1) * LANES], ones_m, 1.0 / (MLA_NOPE_DIM + MLA_ROPE_DIM), gain(G_MQ, h))
        q = _rope(q, tab_ref, T_MLA, 8) * s_mla
        mq_ref[0, :, h * LANES:(h + 1) * LANES] = track("mq", q.astype(BF16), ones_m)
        kn = _seg_norm(uk[:, h * LANES:(h + 1) * LANES], ones_m, 1.0 / MLA_NOPE_DIM, gain(G_KN, h))
        mk_ref[0, :, h * LANES:(h + 1) * LANES] = track("mk", (kn + kr).astype(BF16), ones_m)

    s_d = DIFF_QK_DIM ** -0.5 * LOG2E
    seg = lane >> 5
    for i in range(2):
        q = _seg_norm(slab(P_DQ, i), m32, 1.0 / 32, gain(G_DQ, i))
        q = _rope(q, tab_ref, T_DIFF, 8) * s_d
        track("dq", q.astype(BF16), m32)
        for j in range(4):
            dq_ref[0, :, (4 * i + j) * LANES:(4 * i + j + 1) * LANES] = jnp.where(seg == j, q, 0.0).astype(BF16)
        k = _seg_norm(slab(P_DK, i), m32, 1.0 / 32, gain(G_DK, i))
        dk_ref[0, :, i * LANES:(i + 1) * LANES] = track("dk", _rope(k, tab_ref, T_DIFF, 8).astype(BF16), m32)
        dv_ref[0, :, i * LANES:(i + 1) * LANES] = slab(P_DV, i).astype(BF16)

    s_g = HEAD_DIM ** -0.5 * LOG2E
    for i in range(2):
        q = _seg_norm(slab(P_GQ, i), m64, 1.0 / 64, gain(G_GQ, i))
        q = _rope(q, tab_ref, T_GQA, 16) * s_g
        track("gq", q.astype(BF16), m64)
        gq_ref[0, :, (2 * i) * LANES:(2 * i + 1) * LANES] = jnp.where(lo_half, q, 0.0).astype(BF16)
        gq_ref[0, :, (2 * i + 1) * LANES:(2 * i + 2) * LANES] = jnp.where(lo_half, 0.0, q).astype(BF16)
    k = _seg_norm(slab(P_GK, 0), m64, 1.0 / 64, gain(G_GK, 0))
    gk_ref[0] = track("gk", _rope(k, tab_ref, T_GQA, 16).astype(BF16), m64)
    gv_ref[0] = slab(P_GV, 0).astype(BF16)
    st_ref[0, 0] = jnp.concatenate([norm2[n] for n in STAT_ROWS] + [jnp.zeros((2, LANES), F32)], axis=0)


PREP_WIDTHS = (512, 256, 256, 512, 512, 256, 1024, 256, 256, 512, 128, 128)
STAT_ROWS = ("mq", "mk", "dq", "dk", "gq", "gk")


def prep(p, tabs, gains, mats, wuq, wuk, wuv):
    b, n, _ = p.shape
    tm = min(512, n)
    const2 = lambda i, j: (0, 0)
    stat_spec = pl.BlockSpec((1, 1, 8, LANES), lambda i, j: (i, j, 0, 0))
    stat_shape = jax.ShapeDtypeStruct((b, n // tm, 8, LANES), F32)
    return pl.pallas_call(
        _prep_kernel,
        grid=(b, n // tm),
        in_specs=[pl.BlockSpec((1, tm, P_COLS), lambda i, j: (i, j, 0)),
                  pl.BlockSpec((9, tm, LANES), lambda i, j: (0, j, 0)),
                  pl.BlockSpec(gains.shape, const2),
                  pl.BlockSpec(mats.shape, lambda i, j: (0, 0, 0)),
                  pl.BlockSpec(wuq.shape, const2),
                  pl.BlockSpec(wuk.shape, const2),
                  pl.BlockSpec(wuv.shape, const2)],
        out_specs=[pl.BlockSpec((1, tm, w), lambda i, j: (i, j, 0)) for w in PREP_WIDTHS] + [stat_spec],
        out_shape=[jax.ShapeDtypeStruct((b, n, w), BF16) for w in PREP_WIDTHS] + [stat_shape],
        compiler_params=_cparams(("parallel", "arbitrary")),
        name="prep",
    )(p, tabs, gains, mats, wuq, wuk, wuv)


VT_ROWS = LANES + 16
KEY_CHUNK = 256
FLASH_TQ = 512
FLASH_TK_MAX = 3584


def _flash_kernel(q_ref, k_ref, vt_ref, o_ref, m_ref, l_ref, acc_ref, *, heads, n_out):
    ki = pl.program_id(2)

    @pl.when(ki == 0)
    def _():
        m_ref[...] = jnp.full(m_ref.shape, NEG_BIG, F32)
        l_ref[...] = jnp.zeros(l_ref.shape, F32)
        acc_ref[...] = jnp.zeros(acc_ref.shape, F32)

    for o in range(n_out):
        pair = [h for h in range(len(heads)) if heads[h][3] == o]
        for pos, h in enumerate(pair):
            qs, ks, vs, _ = heads[h]
            q = q_ref[0, :, qs * LANES:(qs + 1) * LANES]
            k = k_ref[0, :, ks * LANES:(ks + 1) * LANES]
            st = _dot_nt(k, q)
            lo, hi = pos * 64, (pos + 1) * 64
            m, l, acc = m_ref[h], l_ref[h], acc_ref[o, lo:hi, :]
            for c0 in range(0, st.shape[0], KEY_CHUNK):
                rows = slice(c0, min(c0 + KEY_CHUNK, st.shape[0]))
                sc = st[rows]
                m_new = jnp.maximum(m, jnp.max(sc, axis=0, keepdims=True))
                a = jnp.exp2(m - m_new)
                pt = jnp.exp2(sc - m_new).astype(BF16)
                r = _dot(vt_ref[0, vs, :, rows], pt)
                l = a * l + r[LANES:LANES + 1]
                acc = acc * a + r[lo:hi]
                m = m_new
            m_ref[h], l_ref[h] = m, l
            acc_ref[o, lo:hi, :] = acc

    @pl.when(ki == pl.num_programs(2) - 1)
    def _():
        for o in range(n_out):
            pair = [h for h in range(len(heads)) if heads[h][3] == o]
            out_t = jnp.concatenate([acc_ref[o, 0:64, :] / l_ref[pair[0]],
                                     acc_ref[o, 64:128, :] / l_ref[pair[1]]], axis=0)
            o_ref[0, :, o * LANES:(o + 1) * LANES] = out_t.T


def _pick_tk(n_keys):
    best = 256
    for t in range(256, FLASH_TK_MAX + 1, 256):
        if n_keys % t == 0:
            best = t
    return best


def values_transposed(v):
    b, n, w = v.shape
    vt = jnp.transpose(v.reshape(b, n, w // LANES, LANES), (0, 2, 3, 1))
    return jnp.concatenate([vt, jnp.ones((b, w // LANES, VT_ROWS - LANES, n), v.dtype)], axis=2)


def flash_attention(q, k, vt, heads):
    b, s, qw = q.shape
    nk = k.shape[1]
    n_out = max(h[3] for h in heads) + 1
    tq = min(FLASH_TQ, s)
    tk = _pick_tk(nk)
    kern = functools.partial(_flash_kernel, heads=heads, n_out=n_out)
    return pl.pallas_call(
        kern,
        grid=(b, s // tq, nk // tk),
        in_specs=[pl.BlockSpec((1, tq, qw), lambda i, j, t: (i, j, 0)),
                  pl.BlockSpec((1, tk, k.shape[2]), lambda i, j, t: (i, t, 0)),
                  pl.BlockSpec((1, vt.shape[1], VT_ROWS, tk), lambda i, j, t: (i, 0, 0, t))],
        out_specs=pl.BlockSpec((1, tq, n_out * LANES), lambda i, j, t: (i, j, 0)),
        out_shape=jax.ShapeDtypeStruct((b, s, n_out * LANES), F32),
        scratch_shapes=[pltpu.VMEM((len(heads), 1, tq), F32),
                        pltpu.VMEM((len(heads), 1, tq), F32),
                        pltpu.VMEM((n_out, LANES, tq), F32)],
        compiler_params=_cparams(("parallel", "parallel", "arbitrary")),
        name="flash_attention",
    )(q, k, vt)


STAB_SHIFT = 100.0
BOUND_LIMIT = 113.0
BOUND_SLACK = 1.01


def _flash_bounded_kernel(stab_ref, q_ref, k_ref, vt_ref, o_ref, acc_ref, *, heads, n_out):
    ki = pl.program_id(2)

    @pl.when(ki == 0)
    def _():
        acc_ref[...] = jnp.zeros(acc_ref.shape, F32)

    stab = stab_ref[pl.program_id(0), 0]
    for h, (qs, ks, vs, _) in enumerate(heads):
        q = q_ref[0, :, qs * LANES:(qs + 1) * LANES]
        k = k_ref[0, :, ks * LANES:(ks + 1) * LANES]
        pt = jnp.exp2(_dot_nt(k, q) - stab).astype(BF16)
        acc_ref[h] += _dot(vt_ref[0, vs], pt)

    @pl.when(ki == pl.num_programs(2) - 1)
    def _():
        for o in range(n_out):
            a, bb = [h for h in range(len(heads)) if heads[h][3] == o]
            out_t = jnp.concatenate([acc_ref[a, 0:64, :] / acc_ref[a, LANES:LANES + 1, :],
                                     acc_ref[bb, 64:128, :] / acc_ref[bb, LANES:LANES + 1, :]], axis=0)
            o_ref[0, :, o * LANES:(o + 1) * LANES] = out_t.T


def flash_attention_bounded(stab, q, k, vt, heads):
    b, s, qw = q.shape
    nk = k.shape[1]
    n_out = max(h[3] for h in heads) + 1
    tq = min(FLASH_TQ, s)
    tk = _pick_tk(nk)
    kern = functools.partial(_flash_bounded_kernel, heads=heads, n_out=n_out)
    return pl.pallas_call(
        kern,
        grid=(b, s // tq, nk // tk),
        in_specs=[pl.BlockSpec(memory_space=pltpu.SMEM),
                  pl.BlockSpec((1, tq, qw), lambda i, j, t: (i, j, 0)),
                  pl.BlockSpec((1, tk, k.shape[2]), lambda i, j, t: (i, t, 0)),
                  pl.BlockSpec((1, vt.shape[1], VT_ROWS, tk), lambda i, j, t: (i, 0, 0, t))],
        out_specs=pl.BlockSpec((1, tq, n_out * LANES), lambda i, j, t: (i, j, 0)),
        out_shape=jax.ShapeDtypeStruct((b, s, n_out * LANES), F32),
        scratch_shapes=[pltpu.VMEM((len(heads), VT_ROWS, tq), F32)],
        compiler_params=_cparams(("parallel", "parallel", "arbitrary")),
        name="flash_attention_bounded",
    )(stab, q, k, vt)


def attention(q2, k2, q, k, vt, heads):
    bound = BOUND_SLACK * jnp.sqrt(q2 * k2)
    stab = (bound - STAB_SHIFT).reshape(-1, 1).astype(F32)
    return lax.cond(jnp.max(bound) <= BOUND_LIMIT,
                    lambda: flash_attention_bounded(stab, q, k, vt, heads),
                    lambda: flash_attention(q, k, vt, heads))


HEADS_PAIRED = tuple((h, h // 2, h // 2, h // 2) for h in range(4))
HEADS_MLA = tuple((h, h, h // 2, h // 2) for h in range(4))
HEADS_GQA = tuple((h, 0, 0, h // 2) for h in range(4))
HEADS_DIFF = tuple((j, j // 4, j // 4, (j % 2) * 2 + j // 4) for j in range(8))


NA_QROWS = 4
NA_TQ = NA_QROWS * GRID_W


def _na_kernel(q_ref, kp_ref, kc_ref, kn_ref, vp_ref, vc_ref, vn_ref, kx_ref, vx_ref, bias_ref, o_ref, *, rows):
    j = pl.program_id(1)
    kh = min(NA_WIN_H, rows)
    qi = lax.broadcasted_iota(jnp.int32, (NA_TQ, 1), 0)
    ki = lax.broadcasted_iota(jnp.int32, (1, NA_TQ), 1)
    wshift = GRID_W.bit_length() - 1
    r = j * NA_QROWS + (qi >> wshift)
    c = qi & (GRID_W - 1)
    r0 = jnp.clip(r - kh // 2, 0, rows - kh)
    c0 = jnp.clip(c - NA_WIN_W // 2, 0, GRID_W - NA_WIN_W)
    kcol = ki & (GRID_W - 1)
    col_ok = (kcol >= c0) & (kcol < c0 + NA_WIN_W)
    masks = []
    for d in (-1, 0, 1):
        kr = (j + d) * NA_QROWS + (ki >> wshift)
        masks.append(col_ok & (kr >= r0) & (kr < r0 + kh))
    lo_half = lax.broadcasted_iota(jnp.int32, (1, LANES), 1) < 64
    for o in range(2):
        k_loc = [kp_ref[0, :, o * LANES:(o + 1) * LANES], kc_ref[0, :, o * LANES:(o + 1) * LANES],
                 kn_ref[0, :, o * LANES:(o + 1) * LANES]]
        v_all = jnp.concatenate([vp_ref[0, :, o * LANES:(o + 1) * LANES], vc_ref[0, :, o * LANES:(o + 1) * LANES],
                                 vn_ref[0, :, o * LANES:(o + 1) * LANES], vx_ref[0, :, o * LANES:(o + 1) * LANES]],
                                axis=0)
        kx = kx_ref[0, :, o * LANES:(o + 1) * LANES]
        res = []
        for h in (2 * o, 2 * o + 1):
            q = q_ref[0, :, h * LANES:(h + 1) * LANES]
            parts = [jnp.where(masks[d], _dot_nt(q, k_loc[d]) + bias_ref[h, d], NEG_BIG) for d in range(3)]
            parts.append(_dot_nt(q, kx))
            s = jnp.concatenate(parts, axis=1)
            m = jnp.max(s, axis=-1, keepdims=True)
            p = jnp.exp2(s - m)
            l = jnp.sum(p, axis=-1, keepdims=True)
            res.append(_dot(p.astype(BF16), v_all) / l)
        o_ref[0, :, o * LANES:(o + 1) * LANES] = jnp.where(lo_half, res[0], res[1])


def na_attention(q, k, v, kx, vx, bias, rows):
    b, s, _ = q.shape
    nblk = rows // NA_QROWS
    nx = kx.shape[1]
    kern = functools.partial(_na_kernel, rows=rows)
    prev = lambda i, j: (i, jnp.maximum(j - 1, 0), 0)
    cur = lambda i, j: (i, j, 0)
    nxt = lambda i, j: (i, jnp.minimum(j + 1, nblk - 1), 0)
    kv_spec = lambda f: pl.BlockSpec((1, NA_TQ, 2 * LANES), f)
    return pl.pallas_call(
        kern,
        grid=(b, nblk),
        in_specs=[pl.BlockSpec((1, NA_TQ, 4 * LANES), cur),
                  kv_spec(prev), kv_spec(cur), kv_spec(nxt),
                  kv_spec(prev), kv_spec(cur), kv_spec(nxt),
                  pl.BlockSpec((1, nx, 2 * LANES), lambda i, j: (i, 0, 0)),
                  pl.BlockSpec((1, nx, 2 * LANES), lambda i, j: (i, 0, 0)),
                  pl.BlockSpec(bias.shape, lambda i, j: (0, 0, 0, 0))],
        out_specs=pl.BlockSpec((1, NA_TQ, 2 * LANES), cur),
        out_shape=jax.ShapeDtypeStruct((b, s, 2 * LANES), F32),
        compiler_params=_cparams(("parallel", "arbitrary")),
        name="na_attention",
    )(q, k, k, k, v, v, v, kx, vx, bias)


ROUTE_T = 256


def _outproj_kernel(x_ref, ona_ref, omla_ref, od_ref, ogqa_ref, mod_ref, gsub_ref, gffn_ref, lam_ref, mat_ref,
                    wout_ref, wr_ref, xo_ref, h_ref, lg_ref, *, lam_init):
    lam = lam_ref[0, 0]
    m64 = mat_ref[M_SEG64]
    pieces = [ona_ref[0].astype(BF16), omla_ref[0].astype(BF16)]
    dsl = []
    for i in range(2):
        d = od_ref[0, :, i * LANES:(i + 1) * LANES] - lam * od_ref[0, :, (2 + i) * LANES:(3 + i) * LANES]
        d = _seg_norm(d, m64, 1.0 / 64, gsub_ref[...]) * (1.0 - lam_init)
        dsl.append(d.astype(BF16))
    pieces += dsl + [ogqa_ref[0].astype(BF16)]
    o = jnp.concatenate(pieces, axis=1)
    y = _dot(o, wout_ref[...])
    x = x_ref[0] + mod_ref[0, 2:3, :] * y
    xo_ref[0] = x
    ms = jnp.mean(x * x, axis=-1, keepdims=True)
    h = x * lax.rsqrt(ms + NORM_EPS) * gffn_ref[...]
    h = h * (1.0 + mod_ref[0, 4:5, :]) + mod_ref[0, 3:4, :]
    h_ref[0] = h.astype(BF16)
    lg = _dot3_nt(wr_ref[...], h)
    for t in range(lg.shape[1] // ROUTE_T):
        lg_ref[0, t] = lg[:, t * ROUTE_T:(t + 1) * ROUTE_T]


def outproj_mod_router(x, o_na, o_mla, o_diff, o_gqa, mod, g_sub_t, g_ffn, lam, mats, w_out, w_router_t, lam_init):
    b, n, d = x.shape
    tm = min(512, n)
    nt = tm // ROUTE_T
    kern = functools.partial(_outproj_kernel, lam_init=lam_init)
    tok = lambda w: pl.BlockSpec((1, tm, w), lambda i, j: (i, j, 0))
    c2 = lambda i, j: (0, 0)
    return pl.pallas_call(
        kern,
        grid=(b, n // tm),
        in_specs=[tok(d), tok(256), tok(256), tok(512), tok(256),
                  pl.BlockSpec((1, 8, d), lambda i, j: (i, 0, 0)),
                  pl.BlockSpec((1, LANES), c2), pl.BlockSpec((1, d), c2),
                  pl.BlockSpec(memory_space=pltpu.SMEM),
                  pl.BlockSpec(mats.shape, lambda i, j: (0, 0, 0)),
                  pl.BlockSpec((d, d), c2), pl.BlockSpec((N_EXPERTS, d), c2)],
        out_specs=[tok(d), tok(d),
                   pl.BlockSpec((1, nt, N_EXPERTS, ROUTE_T), lambda i, j: (i, j, 0, 0))],
        out_shape=[jax.ShapeDtypeStruct((b, n, d), F32), jax.ShapeDtypeStruct((b, n, d), BF16),
                   jax.ShapeDtypeStruct((b, n // ROUTE_T, N_EXPERTS, ROUTE_T), F32)],
        compiler_params=_cparams(("parallel", "arbitrary")),
        name="outproj_mod_router",
    )(x, o_na, o_mla, o_diff, o_gqa, mod, g_sub_t, g_ffn.reshape(1, d), lam, mats, w_out, w_router_t)


def _route_kernel(lg_ref, tri_ref, aff_ref, pos_ref, off_ref, *, cap):
    nb = lg_ref.shape[1]
    lg = lg_ref[0]
    mx = jnp.max(lg, axis=1, keepdims=True)
    ex = jnp.exp(lg - mx)
    aff = ex / jnp.sum(ex, axis=1, keepdims=True)
    aff_ref[0] = aff
    bits = lax.bitcast_convert_type(aff, jnp.int32)

    def count_ge(t):
        hit = jnp.where(bits >= t[None], 1.0, 0.0)
        return jnp.sum(jnp.sum(hit, axis=0), axis=1, keepdims=True)

    def bis(i, t):
        cand = t | (jnp.int32(1) << (30 - i))
        return jnp.where(count_ge(cand) >= float(cap), cand, t)

    thr = lax.fori_loop(0, 31, bis, jnp.zeros((N_EXPERTS, 1), jnp.int32))
    need = float(cap) - count_ge(thr + 1)
    tri = tri_ref[...]

    def blk(jb, carry):
        c_eq, c_pos = carry
        bb = lax.bitcast_convert_type(aff_ref[0, jb], jnp.int32)
        gt = bb > thr
        eq = bb == thr
        eq_before = _dot(jnp.where(eq, 1.0, 0.0).astype(BF16), tri) + c_eq
        sel = gt | (eq & (eq_before < need))
        sel_f = jnp.where(sel, 1.0, 0.0)
        before = _dot(sel_f.astype(BF16), tri) + c_pos
        pos_ref[0, jb] = jnp.where(sel, before, -1.0).astype(jnp.int32)
        off_ref[0, jb] = jnp.broadcast_to(c_pos, (N_EXPERTS, LANES)).astype(jnp.int32)
        c_eq = c_eq + jnp.sum(jnp.where(eq, 1.0, 0.0), axis=1, keepdims=True)
        c_pos = c_pos + jnp.sum(sel_f, axis=1, keepdims=True)
        return c_eq, c_pos

    zero = jnp.zeros((N_EXPERTS, 1), F32)
    lax.fori_loop(0, nb, blk, (zero, zero))


def route(logits, tri, cap):
    b, nb, e, t = logits.shape
    kern = functools.partial(_route_kernel, cap=cap)
    spec = pl.BlockSpec((1, nb, e, t), lambda i: (i, 0, 0, 0))
    return pl.pallas_call(
        kern,
        grid=(b,),
        in_specs=[spec, pl.BlockSpec(tri.shape, lambda i: (0, 0))],
        out_specs=[spec, spec, pl.BlockSpec((1, nb, e, LANES), lambda i: (i, 0, 0, 0))],
        out_shape=[jax.ShapeDtypeStruct((b, nb, e, t), F32), jax.ShapeDtypeStruct((b, nb, e, t), jnp.int32),
                   jax.ShapeDtypeStruct((b, nb, e, LANES), jnp.int32)],
        compiler_params=_cparams(("arbitrary",)),
        name="route",
    )(logits, tri)


def _gather_kernel(off_ref, h_ref, pos_ref, aff_ref, xe_ref, gs_ref, acc_ref, gacc_ref, *, st, n_tiles, nsub, nb):
    bi, e, ch = pl.program_id(0), pl.program_id(1), pl.program_id(2)

    @pl.when(ch == 0)
    def _():
        acc_ref[...] = jnp.zeros(acc_ref.shape, F32)
        gacc_ref[...] = jnp.zeros(gacc_ref.shape, F32)

    slot = lax.broadcasted_iota(jnp.int32, (st, ROUTE_T), 0)
    for jj in range(nsub):
        jb = ch * nsub + jj
        a = off_ref[(bi * N_EXPERTS + e) * nb + jb] // st
        prow = pos_ref[0, jj, pl.ds(e, 1), :]
        arow = aff_ref[0, jj, pl.ds(e, 1), :]
        hs = h_ref[0, jj * ROUTE_T:(jj + 1) * ROUTE_T, :]
        for w in range(2):
            tile = jnp.minimum(a + w, n_tiles - 1)
            hit = slot == (prow - (a + w) * st)
            base = pl.multiple_of(tile * st, st)
            acc_ref[pl.ds(base, st), :] += _dot(jnp.where(hit, 1.0, 0.0).astype(BF16), hs)
            gacc_ref[pl.ds(base, st), :] += jnp.sum(jnp.where(hit, arow, 0.0), axis=1, keepdims=True)

    @pl.when(ch == pl.num_programs(2) - 1)
    def _():
        xe_ref[0, 0] = acc_ref[...].astype(BF16)
        gs_ref[0, 0] = gacc_ref[...]


def moe_gather(offs, h, pos, aff, cap):
    b, n, d = h.shape
    nb = n // ROUTE_T
    st = min(ROUTE_T, cap)
    n_tiles = cap // st
    chunk = min(2048, n)
    nsub = chunk // ROUTE_T
    kern = functools.partial(_gather_kernel, st=st, n_tiles=n_tiles, nsub=nsub, nb=nb)
    rspec = pl.BlockSpec((1, nsub, N_EXPERTS, ROUTE_T), lambda i, e, c, off: (i, c, 0, 0))
    return pl.pallas_call(
        kern,
        grid_spec=pltpu.PrefetchScalarGridSpec(
            num_scalar_prefetch=1,
            grid=(b, N_EXPERTS, n // chunk),
            in_specs=[pl.BlockSpec((1, chunk, d), lambda i, e, c, off: (i, c, 0)), rspec, rspec],
            out_specs=[pl.BlockSpec((1, 1, cap, d), lambda i, e, c, off: (i, e, 0, 0)),
                       pl.BlockSpec((1, 1, cap, 1), lambda i, e, c, off: (i, e, 0, 0))],
            scratch_shapes=[pltpu.VMEM((cap, d), F32), pltpu.VMEM((cap, 1), F32)]),
        out_shape=[jax.ShapeDtypeStruct((b, N_EXPERTS, cap, d), BF16),
                   jax.ShapeDtypeStruct((b, N_EXPERTS, cap, 1), F32)],
        compiler_params=_cparams(("parallel", "parallel", "arbitrary")),
        name="moe_gather",
    )(offs, h, pos, aff)


FF_TILE = 256


def _ffn_kernel(x_ref, g_ref, wg_ref, wu_ref, wd_ref, y_ref, acc_ref):
    f = pl.program_id(2)

    @pl.when(f == 0)
    def _():
        acc_ref[...] = jnp.zeros(acc_ref.shape, F32)

    x = x_ref[0, 0]
    gate = _dot(x, wg_ref[0].astype(BF16))
    up = _dot(x, wu_ref[0].astype(BF16))
    hmid = (_silu(gate) * up).astype(BF16)
    acc_ref[...] += _dot(hmid, wd_ref[0].astype(BF16))

    @pl.when(f == pl.num_programs(2) - 1)
    def _():
        y_ref[0, 0] = (acc_ref[...] * g_ref[0, 0]).astype(BF16)


def moe_ffn(xe, gs, w_gate, w_up, w_down):
    b, e, cap, d = xe.shape
    dff = w_gate.shape[2]
    return pl.pallas_call(
        _ffn_kernel,
        grid=(e, b, dff // FF_TILE),
        in_specs=[pl.BlockSpec((1, 1, cap, d), lambda ei, bi, f: (bi, ei, 0, 0)),
                  pl.BlockSpec((1, 1, cap, 1), lambda ei, bi, f: (bi, ei, 0, 0)),
                  pl.BlockSpec((1, d, FF_TILE), lambda ei, bi, f: (ei, 0, f)),
                  pl.BlockSpec((1, d, FF_TILE), lambda ei, bi, f: (ei, 0, f)),
                  pl.BlockSpec((1, FF_TILE, d), lambda ei, bi, f: (ei, f, 0))],
        out_specs=pl.BlockSpec((1, 1, cap, d), lambda ei, bi, f: (bi, ei, 0, 0)),
        out_shape=jax.ShapeDtypeStruct((b, e, cap, d), BF16),
        scratch_shapes=[pltpu.VMEM((cap, d), F32)],
        compiler_params=_cparams(("parallel", "parallel", "arbitrary")),
        name="moe_ffn",
    )(xe, gs, w_gate, w_up, w_down)


def _combine_kernel(off_ref, x_ref, ye_ref, pos_ref, mod_ref, o_ref, *, st, n_tiles, nsub, nb):
    bi, ch, e = pl.program_id(0), pl.program_id(1), pl.program_id(2)

    @pl.when(e == 0)
    def _():
        o_ref[...] = jnp.zeros(o_ref.shape, F32)

    slot = lax.broadcasted_iota(jnp.int32, (st, ROUTE_T), 0)
    for jj in range(nsub):
        jb = ch * nsub + jj
        a = off_ref[(bi * N_EXPERTS + e) * nb + jb] // st
        prow = pos_ref[0, jj, pl.ds(e, 1), :]
        tot = jnp.zeros((ROUTE_T, o_ref.shape[2]), F32)
        for w in range(2):
            tile = jnp.minimum(a + w, n_tiles - 1)
            hit = slot == (prow - (a + w) * st)
            base = pl.multiple_of(tile * st, st)
            oh = jnp.where(hit, 1.0, 0.0).astype(BF16)
            tot = tot + lax.dot_general(oh, ye_ref[0, 0, pl.ds(base, st), :], (((0,), (0,)), ((), ())),
                                        preferred_element_type=F32)
        o_ref[0, jj * ROUTE_T:(jj + 1) * ROUTE_T, :] += tot

    @pl.when(e == pl.num_programs(2) - 1)
    def _():
        o_ref[0] = x_ref[0] + mod_ref[0, 5:6, :] * o_ref[0]


def moe_combine(offs, x, ye, pos, mod, cap):
    b, n, d = x.shape
    nb = n // ROUTE_T
    st = min(ROUTE_T, cap)
    n_tiles = cap // st
    chunk = min(2048, n)
    nsub = chunk // ROUTE_T
    kern = functools.partial(_combine_kernel, st=st, n_tiles=n_tiles, nsub=nsub, nb=nb)
    return pl.pallas_call(
        kern,
        grid_spec=pltpu.PrefetchScalarGridSpec(
            num_scalar_prefetch=1,
            grid=(b, n // chunk, N_EXPERTS),
            in_specs=[pl.BlockSpec((1, chunk, d), lambda i, c, e, off: (i, c, 0)),
                      pl.BlockSpec((1, 1, cap, d), lambda i, c, e, off: (i, e, 0, 0)),
                      pl.BlockSpec((1, nsub, N_EXPERTS, ROUTE_T), lambda i, c, e, off: (i, c, 0, 0)),
                      pl.BlockSpec((1, 8, d), lambda i, c, e, off: (i, 0, 0))],
            out_specs=pl.BlockSpec((1, chunk, d), lambda i, c, e, off: (i, c, 0))),
        out_shape=jax.ShapeDtypeStruct((b, n, d), F32),
        compiler_params=_cparams(("parallel", "parallel", "arbitrary")),
        name="moe_combine",
    )(offs, x, ye, pos, mod)


def expert_choice_ffn(x, h, logits, mod, tri, w_gate, w_up, w_down):
    b, n, _ = x.shape
    cap = EC_CAPACITY * n // N_EXPERTS
    aff, pos, off = route(logits, tri, cap)
    offs = jnp.transpose(off[..., 0], (0, 2, 1)).reshape(-1)
    xe, gs = moe_gather(offs, h, pos, aff, cap)
    ye = moe_ffn(xe, gs, w_gate, w_up, w_down)
    return moe_combine(offs, x, ye, pos, mod, cap)


def _rope_tables(n_rows_grid):
    s = n_rows_grid * GRID_W
    t = np.arange(s)
    row, col = (t // GRID_W).astype(np.float64), (t % GRID_W).astype(np.float64)

    def unit(n):
        half = n // 2
        inv = ROPE_THETA ** (-np.arange(0, n, 2, dtype=np.float64) / n)
        out = []
        for pos in (row, col):
            ang = pos[:, None] * inv[None, :]
            c, sn = np.cos(ang), np.sin(ang)
            z = np.zeros_like(sn)
            out.append((np.concatenate([c, c], 1), np.concatenate([-sn, z], 1), np.concatenate([z, sn], 1)))
        return [np.concatenate([out[0][i], out[1][i]], 1) for i in range(3)]

    ident = lambda w: (np.ones((s, w), np.float32), np.zeros((s, w), np.float32), np.zeros((s, w), np.float32))
    u16 = unit(16)
    u32 = unit(32)
    tabs = []
    idt = ident(64)
    idt32 = ident(32)
    for i in range(3):
        tabs.append(np.concatenate([idt[i], u16[i], idt32[i]], 1))
    for i in range(3):
        tabs.append(np.concatenate([u16[i]] * 4, 1))
    for i in range(3):
        tabs.append(np.concatenate([u32[i]] * 2, 1))
    return np.stack(tabs).astype(np.float32)


def _identity_tables(n):
    one, zero = np.ones((n, LANES), np.float32), np.zeros((n, LANES), np.float32)
    return np.stack([one, zero, zero] * 3)


def _seg_matrices():
    i = np.arange(LANES)
    ones = np.ones((LANES, LANES), np.float32)
    m64 = (i[:, None] // 64 == i[None, :] // 64).astype(np.float32)
    m32 = (i[:, None] // 32 == i[None, :] // 32).astype(np.float32)
    return np.stack([ones, m64, m32])


def _na_bias(rpb):
    c = np.arange(GRID_W)
    rl = np.arange(NA_QROWS)
    dc = np.clip(c[None, :] - c[:, None] + NA_WIN_W - 1, 0, 2 * NA_WIN_W - 2)
    d = np.array([-1, 0, 1])
    dr = np.clip(NA_QROWS * d[:, None, None] + rl[None, None, :] - rl[None, :, None] + NA_WIN_H - 1,
                 0, 2 * NA_WIN_H - 2)
    cols = jnp.take(rpb.astype(F32) * LOG2E, jnp.asarray(dc), axis=2)
    full = jnp.take(cols, jnp.asarray(dr), axis=1)
    return jnp.transpose(full, (0, 1, 2, 4, 3, 5)).reshape(rpb.shape[0], 3, NA_TQ, NA_TQ)


def _pad_cols(w, width):
    return jnp.pad(w, ((0, 0), (0, width - w.shape[1])))


def _layer_params(l, w_in, g_na_q, g_na_k, na_rpb, g_mla_cq, w_mla_uq, g_mla_q, g_mla_ckv, w_mla_ukv,
                  g_mla_k_nope, g_mla_k_rope, g_diff_q, g_diff_k, g_diff_sub, g_gqa_q, g_gqa_k, w_out):
    wi = w_in[l]
    (naq, nak, nav, cq, ckv, kr, dq, dk, dv, gq, gk, gv) = jnp.split(
        wi, np.cumsum([256, 256, 256, 256, 128, 32, 256, 256, 256, 256, 128])[:], axis=1)
    gq4 = gq.reshape(-1, 4, 64)[:, jnp.array([0, 2, 1, 3])].reshape(-1, 256)
    zeros = lambda w: jnp.zeros((wi.shape[0], w), wi.dtype)
    w_in_r = jnp.concatenate([naq, nak, nav, cq, ckv, dq, dk, dv, gq4, gk, gv, zeros(64), kr, zeros(32)],
                             axis=1).astype(BF16)
    uq = w_mla_uq[l].reshape(MLA_Q_RANK, 4, 96)
    wuq = jnp.pad(uq, ((0, 0), (0, 0), (0, 32))).reshape(MLA_Q_RANK, 512).astype(BF16)
    ukv = w_mla_ukv[l].reshape(MLA_KV_RANK, 4, 128)
    wuk = jnp.pad(ukv[:, :, :64], ((0, 0), (0, 0), (0, 64))).reshape(MLA_KV_RANK, 512).astype(BF16)
    wuv = ukv[:, :, 64:].reshape(MLA_KV_RANK, 256).astype(BF16)
    row = lambda v: jnp.pad(v, (0, 512 - v.shape[0]))
    z32, z64 = jnp.zeros((32,), F32), jnp.zeros((64,), F32)
    gains = jnp.stack([
        row(jnp.tile(g_na_q[l], 4)), row(jnp.tile(g_na_k[l], 4)), row(g_mla_cq[l]),
        row(jnp.tile(jnp.concatenate([g_mla_q[l], z32]), 4)), row(g_mla_ckv[l]),
        row(jnp.tile(jnp.concatenate([g_mla_k_nope[l], z64]), 4)),
        row(jnp.concatenate([z64, g_mla_k_rope[l], z32])),
        row(jnp.tile(g_diff_q[l].reshape(-1), 4)), row(jnp.tile(g_diff_k[l].reshape(-1), 4)),
        row(jnp.tile(g_gqa_q[l], 4)), row(jnp.tile(g_gqa_k[l], 2))] + [jnp.zeros((512,), F32)] * 5)
    wo = w_out[l]
    wo_g = wo[768:].reshape(4, 64, -1)[jnp.array([0, 2, 1, 3])].reshape(256, -1)
    w_out_r = jnp.concatenate([wo[:768], wo_g], axis=0).astype(BF16)
    return dict(w_in=w_in_r, wuq=wuq, wuk=wuk, wuv=wuv, gains=gains, w_out=w_out_r,
                bias=_na_bias(na_rpb[l]), g_sub=jnp.tile(g_diff_sub[l], 2).reshape(1, LANES))


def kernel(x, c, ctx, c_ctx, w_mod, b_mod, g_attn, g_ffn, w_in, g_na_q, g_na_k, na_rpb, g_mla_cq, w_mla_uq, g_mla_q, g_mla_ckv, w_mla_ukv, g_mla_k_nope, g_mla_k_rope, g_diff_q, g_diff_k, diff_lambda, g_diff_sub, g_gqa_q, g_gqa_k, w_out, w_router, w_gate, w_up, w_down):
    b, s, d = x.shape
    n_ctx = ctx.shape[1]
    rows = s // GRID_W
    tabs = jnp.asarray(_rope_tables(rows))
    tabs_ctx = jnp.asarray(_identity_tables(n_ctx))
    mats = jnp.asarray(_seg_matrices()).astype(BF16)
    tri = jnp.asarray(np.triu(np.ones((ROUTE_T, ROUTE_T), np.float32), 1)).astype(BF16)
    c_rows = jnp.concatenate([c, c_ctx[None], jnp.zeros((8 - b - 1, d), F32)], axis=0)
    xc = ctx
    for l in range(DEPTH):
        need_ctx = l < DEPTH - 1
        lam_init = 0.8 - 0.6 * math.exp(-0.3 * l)
        lp = diff_lambda[l].astype(F32)
        lam = (jnp.exp(jnp.sum(lp[0] * lp[1])) - jnp.exp(jnp.sum(lp[2] * lp[3])) + lam_init).reshape(1, 1)
        prm = _layer_params(l, w_in, g_na_q, g_na_k, na_rpb, g_mla_cq, w_mla_uq, g_mla_q, g_mla_ckv, w_mla_ukv,
                            g_mla_k_nope, g_mla_k_rope, g_diff_q, g_diff_k, g_diff_sub, g_gqa_q, g_gqa_k, w_out)
        modv = mod_vectors(c_rows, w_mod[l], b_mod[l]).reshape(8, N_MOD, d)
        mod = jnp.pad(modv[:b], ((0, 0), (0, 2), (0, 0)))
        mod_c = jnp.broadcast_to(jnp.pad(modv[b:b + 1], ((0, 0), (0, 2), (0, 0))), (b, 8, d))
        w_router_t = w_router[l].T

        def mix_inputs(xin, m, tb):
            p = ln_mod_proj(xin, g_attn[l], m[:, 0:1], m[:, 1:2], prm["w_in"])
            return prep(p, tb, prm["gains"], mats, prm["wuq"], prm["wuk"], prm["wuv"])

        (naq, nak, nav, mq, mk, mv, dq, dk, dv, gq, gk, gv, stat) = mix_inputs(x, mod, tabs)
        (naq_c, nak_c, nav_c, mq_c, mk_c, mv_c, dq_c, dk_c, dv_c, gq_c, gk_c, gv_c, stat_c) = mix_inputs(
            xc, mod_c, tabs_ctx)
        cat = lambda a, bb: jnp.concatenate([a, bb], axis=1)
        n2 = jnp.max(stat, axis=(1, 3))
        n2c = jnp.max(stat_c, axis=(1, 3))
        q2 = lambda name: n2[:, STAT_ROWS.index(name)]
        k2 = lambda name: jnp.maximum(n2[:, STAT_ROWS.index(name)], n2c[:, STAT_ROWS.index(name)])

        ext = values_transposed
        o_na = na_attention(naq, nak, nav, nak_c, nav_c, prm["bias"], rows)
        o_mla = attention(q2("mq"), k2("mk"), mq, cat(mk_c, mk), ext(cat(mv_c, mv)), HEADS_MLA)
        o_diff = attention(q2("dq"), k2("dk"), dq, cat(dk_c, dk), ext(cat(dv_c, dv)), HEADS_DIFF)
        o_gqa = attention(q2("gq"), k2("gk"), gq, cat(gk_c, gk), ext(cat(gv_c, gv)), HEADS_GQA)
        x_mid, h2, logits = outproj_mod_router(x, o_na, o_mla, o_diff, o_gqa, mod, prm["g_sub"], g_ffn[l], lam,
                                               mats, prm["w_out"], w_router_t, lam_init)
        if need_ctx:
            oc_na = flash_attention(naq_c, nak_c, ext(nav_c), HEADS_PAIRED)
            oc_mla = flash_attention(mq_c, mk_c, ext(mv_c), HEADS_MLA)
            oc_diff = flash_attention(dq_c, dk_c, ext(dv_c), HEADS_DIFF)
            oc_gqa = flash_attention(gq_c, gk_c, ext(gv_c), HEADS_GQA)
            xc_mid, hc2, logits_c = outproj_mod_router(xc, oc_na, oc_mla, oc_diff, oc_gqa, mod_c, prm["g_sub"],
                                                       g_ffn[l], lam, mats, prm["w_out"], w_router_t, lam_init)
            xc = expert_choice_ffn(xc_mid, hc2, logits_c, mod_c, tri, w_gate[l], w_up[l], w_down[l])
        x = expert_choice_ffn(x_mid, h2, logits, mod, tri, w_gate[l], w_up[l], w_down[l])
    return x
```

```python
import functools
import math

import numpy as np
import jax
import jax.numpy as jnp
from jax import lax
from jax.experimental import pallas as pl
from jax.experimental.pallas import tpu as pltpu

D_MODEL = 1024
GRID_W = 64
HEAD_DIM = 64
N_HEADS = 4
NA_WIN_H = 8
NA_WIN_W = 16
MLA_Q_RANK = 256
MLA_KV_RANK = 128
MLA_NOPE_DIM = 64
MLA_ROPE_DIM = 32
MLA_V_DIM = 64
DIFF_QK_DIM = 32
N_EXPERTS = 16
EC_CAPACITY = 2
D_FF = 2816
ROPE_THETA = 10000.0
NORM_EPS = 1e-6
N_MOD = 6
DEPTH = 2

LANES = 128
P_COLS = 2560
VMEM_LIMIT = 56 * 1024 * 1024
NEG_BIG = -1e30
LOG2E = math.log2(math.e)

F32 = jnp.float32
BF16 = jnp.bfloat16


def _cparams(sem):
    return pltpu.CompilerParams(dimension_semantics=sem, vmem_limit_bytes=VMEM_LIMIT)


def _split(a):
    hi = a.astype(BF16)
    lo = (a - hi.astype(F32)).astype(BF16)
    return hi, lo


def _dot(a, b):
    return jnp.dot(a, b, preferred_element_type=F32)


def _dot_nt(a, b):
    return lax.dot_general(a, b, (((1,), (1,)), ((), ())), preferred_element_type=F32)


def _dot3(a, b):
    ah, al = _split(a)
    bh, bl = _split(b)
    return _dot(ah, bh) + _dot(ah, bl) + _dot(al, bh)


def _dot3_nt(a, b):
    ah, al = _split(a)
    bh, bl = _split(b)
    return _dot_nt(ah, bh) + _dot_nt(ah, bl) + _dot_nt(al, bh)


def _silu(v):
    return v * jax.nn.sigmoid(v)


def _mod_kernel(c_ref, w_ref, b_ref, o_ref):
    o_ref[...] = _dot3(_silu(c_ref[...]), w_ref[...]) + b_ref[...]


def mod_vectors(c_rows, w_mod, b_mod):
    m, d = c_rows.shape
    n = w_mod.shape[1]
    tn = 1536
    return pl.pallas_call(
        _mod_kernel,
        grid=(n // tn,),
        in_specs=[pl.BlockSpec((m, d), lambda j: (0, 0)),
                  pl.BlockSpec((d, tn), lambda j: (0, j)),
                  pl.BlockSpec((1, tn), lambda j: (0, j))],
        out_specs=pl.BlockSpec((m, tn), lambda j: (0, j)),
        out_shape=jax.ShapeDtypeStruct((m, n), F32),
        compiler_params=_cparams(("arbitrary",)),
        name="mod_vectors",
    )(c_rows, w_mod, b_mod.reshape(1, n))


def _ln_proj_kernel(x_ref, g_ref, sh_ref, sc_ref, w_ref, o_ref):
    x = x_ref[0]
    ms = jnp.mean(x * x, axis=-1, keepdims=True)
    y = x * lax.rsqrt(ms + NORM_EPS) * g_ref[...]
    h = y * (1.0 + sc_ref[0]) + sh_ref[0]
    o_ref[0] = _dot(h.astype(BF16), w_ref[...])


def ln_mod_proj(x, g, shift, scale, w):
    b, n, d = x.shape
    ncol = w.shape[1]
    tm = min(512, n)
    return pl.pallas_call(
        _ln_proj_kernel,
        grid=(b, n // tm),
        in_specs=[pl.BlockSpec((1, tm, d), lambda i, j: (i, j, 0)),
                  pl.BlockSpec((1, d), lambda i, j: (0, 0)),
                  pl.BlockSpec((1, 1, d), lambda i, j: (i, 0, 0)),
                  pl.BlockSpec((1, 1, d), lambda i, j: (i, 0, 0)),
                  pl.BlockSpec((d, ncol), lambda i, j: (0, 0))],
        out_specs=pl.BlockSpec((1, tm, ncol), lambda i, j: (i, j, 0)),
        out_shape=jax.ShapeDtypeStruct((b, n, ncol), F32),
        compiler_params=_cparams(("parallel", "arbitrary")),
        name="ln_mod_proj",
    )(x, g.reshape(1, d), shift, scale, w)


P_NAQ, P_NAK, P_NAV, P_CQ, P_CKV = 0, 256, 512, 768, 1024
P_DQ, P_DK, P_DV, P_GQ, P_GK, P_GV, P_KR = 1152, 1408, 1664, 1920, 2176, 2304, 2432
(G_NAQ, G_NAK, G_CQ, G_MQ, G_CKV, G_KN, G_KR, G_DQ, G_DK, G_GQ, G_GK) = range(11)
M_ONES, M_SEG64, M_SEG32 = 0, 1, 2
T_MLA, T_DIFF, T_GQA = 0, 3, 6


def _seg_norm(x, mat, inv_n, g):
    sq = x * x
    hi, lo = _split(sq)
    ms = (_dot(hi, mat) + _dot(lo, mat)) * inv_n
    return x * lax.rsqrt(ms + NORM_EPS) * g


def _rope(x, tab_ref, t0, half):
    c, s1, s2 = tab_ref[t0], tab_ref[t0 + 1], tab_ref[t0 + 2]
    return (x * c + pltpu.roll(x, LANES - half, axis=1) * s1
            + pltpu.roll(x, half, axis=1) * s2)


def _prep_kernel(p_ref, tab_ref, gain_ref, mat_ref, wuq_ref, wuk_ref, wuv_ref,
                 naq_ref, nak_ref, nav_ref, mq_ref, mk_ref, mv_ref,
                 dq_ref, dk_ref, dv_ref, gq_ref, gk_ref, gv_ref, st_ref):
    lane = lax.broadcasted_iota(jnp.int32, (1, LANES), 1)
    lo_half = lane < 64
    ones_m, m64, m32 = mat_ref[M_ONES], mat_ref[M_SEG64], mat_ref[M_SEG32]
    norm2 = {}

    def slab(off, i):
        return p_ref[0, :, off + i * LANES: off + (i + 1) * LANES]

    def track(name, xb, mat):
        xf = xb.astype(F32)
        hi, lo = _split(xf * xf)
        cur = jnp.max(_dot(hi, mat) + _dot(lo, mat), axis=0, keepdims=True)
        norm2[name] = jnp.maximum(norm2[name], cur) if name in norm2 else cur
        return xb

    def gain(row, i):
        return gain_ref[row:row + 1, i * LANES:(i + 1) * LANES]

    s_na = HEAD_DIM ** -0.5 * LOG2E
    for i in range(2):
        q = _seg_norm(slab(P_NAQ, i), m64, 1.0 / 64, gain(G_NAQ, i)) * s_na
        naq_ref[0, :, (2 * i) * LANES:(2 * i + 1) * LANES] = jnp.where(lo_half, q, 0.0).astype(BF16)
        naq_ref[0, :, (2 * i + 1) * LANES:(2 * i + 2) * LANES] = jnp.where(lo_half, 0.0, q).astype(BF16)
        k = _seg_norm(slab(P_NAK, i), m64, 1.0 / 64, gain(G_NAK, i))
        nak_ref[0, :, i * LANES:(i + 1) * LANES] = k.astype(BF16)
        nav_ref[0, :, i * LANES:(i + 1) * LANES] = slab(P_NAV, i).astype(BF16)

    cq = p_ref[0, :, P_CQ:P_CQ + MLA_Q_RANK]
    cq = cq * lax.rsqrt(jnp.mean(cq * cq, axis=-1, keepdims=True) + NORM_EPS) * gain_ref[G_CQ:G_CQ + 1, :MLA_Q_RANK]
    uq = _dot(cq.astype(BF16), wuq_ref[...])
    s_mla = (MLA_NOPE_DIM + MLA_ROPE_DIM) ** -0.5 * LOG2E
    ckv = p_ref[0, :, P_CKV:P_CKV + MLA_KV_RANK]
    ckv = ckv * lax.rsqrt(jnp.mean(ckv * ckv, axis=-1, keepdims=True) + NORM_EPS) * gain_ref[G_CKV:G_CKV + 1, :MLA_KV_RANK]
    ckv_b = ckv.astype(BF16)
    uk = _dot(ckv_b, wuk_ref[...])
    mv_ref[0] = _dot(ckv_b, wuv_ref[...]).astype(BF16)
    kr = _seg_norm(slab(P_KR, 0), ones_m, 1.0 / MLA_ROPE_DIM, gain(G_KR, 0))
    kr = _rope(kr, tab_ref, T_MLA, 8)
    for h in range(N_HEADS):
        q = _seg_norm(uq[:, h * LANES:(h + 1) * LANES], ones_m, 1.0 / (MLA_NOPE_DIM + MLA_ROPE_DIM), gain(G_MQ, h))
        q = _rope(q, tab_ref, T_MLA, 8) * s_mla
        mq_ref[0, :, h * LANES:(h + 1) * LANES] = track("mq", q.astype(BF16), ones_m)
        kn = _seg_norm(uk[:, h * LANES:(h + 1) * LANES], ones_m, 1.0 / MLA_NOPE_DIM, gain(G_KN, h))
        mk_ref[0, :, h * LANES:(h + 1) * LANES] = track("mk", (kn + kr).astype(BF16), ones_m)

    s_d = DIFF_QK_DIM ** -0.5 * LOG2E
    seg = lane >> 5
    for i in range(2):
        q = _seg_norm(slab(P_DQ, i), m32, 1.0 / 32, gain(G_DQ, i))
        q = _rope(q, tab_ref, T_DIFF, 8) * s_d
        track("dq", q.astype(BF16), m32)
        for j in range(4):
            dq_ref[0, :, (4 * i + j) * LANES:(4 * i + j + 1) * LANES] = jnp.where(seg == j, q, 0.0).astype(BF16)
        k = _seg_norm(slab(P_DK, i), m32, 1.0 / 32, gain(G_DK, i))
        dk_ref[0, :, i * LANES:(i + 1) * LANES] = track("dk", _rope(k, tab_ref, T_DIFF, 8).astype(BF16), m32)
        dv_ref[0, :, i * LANES:(i + 1) * LANES] = slab(P_DV, i).astype(BF16)

    s_g = HEAD_DIM ** -0.5 * LOG2E
    for i in range(2):
        q = _seg_norm(slab(P_GQ, i), m64, 1.0 / 64, gain(G_GQ, i))
        q = _rope(q, tab_ref, T_GQA, 16) * s_g
        track("gq", q.astype(BF16), m64)
        gq_ref[0, :, (2 * i) * LANES:(2 * i + 1) * LANES] = jnp.where(lo_half, q, 0.0).astype(BF16)
        gq_ref[0, :, (2 * i + 1) * LANES:(2 * i + 2) * LANES] = jnp.where(lo_half, 0.0, q).astype(BF16)
    k = _seg_norm(slab(P_GK, 0), m64, 1.0 / 64, gain(G_GK, 0))
    gk_ref[0] = track("gk", _rope(k, tab_ref, T_GQA, 16).astype(BF16), m64)
    gv_ref[0] = slab(P_GV, 0).astype(BF16)
    st_ref[0, 0] = jnp.concatenate([norm2[n] for n in STAT_ROWS] + [jnp.zeros((2, LANES), F32)], axis=0)


PREP_WIDTHS = (512, 256, 256, 512, 512, 256, 1024, 256, 256, 512, 128, 128)
STAT_ROWS = ("mq", "mk", "dq", "dk", "gq", "gk")


def prep(p, tabs, gains, mats, wuq, wuk, wuv):
    b, n, _ = p.shape
    tm = min(512, n)
    const2 = lambda i, j: (0, 0)
    stat_spec = pl.BlockSpec((1, 1, 8, LANES), lambda i, j: (i, j, 0, 0))
    stat_shape = jax.ShapeDtypeStruct((b, n // tm, 8, LANES), F32)
    return pl.pallas_call(
        _prep_kernel,
        grid=(b, n // tm),
        in_specs=[pl.BlockSpec((1, tm, P_COLS), lambda i, j: (i, j, 0)),
                  pl.BlockSpec((9, tm, LANES), lambda i, j: (0, j, 0)),
                  pl.BlockSpec(gains.shape, const2),
                  pl.BlockSpec(mats.shape, lambda i, j: (0, 0, 0)),
                  pl.BlockSpec(wuq.shape, const2),
                  pl.BlockSpec(wuk.shape, const2),
                  pl.BlockSpec(wuv.shape, const2)],
        out_specs=[pl.BlockSpec((1, tm, w), lambda i, j: (i, j, 0)) for w in PREP_WIDTHS] + [stat_spec],
        out_shape=[jax.ShapeDtypeStruct((b, n, w), BF16) for w in PREP_WIDTHS] + [stat_shape],
        compiler_params=_cparams(("parallel", "arbitrary")),
        name="prep",
    )(p, tabs, gains, mats, wuq, wuk, wuv)


VT_ROWS = LANES + 16
KEY_CHUNK = 256
FLASH_TQ = 512
FLASH_TK_MAX = 3584


def _flash_kernel(q_ref, k_ref, vt_ref, o_ref, m_ref, l_ref, acc_ref, *, heads, n_out):
    ki = pl.program_id(2)

    @pl.when(ki == 0)
    def _():
        m_ref[...] = jnp.full(m_ref.shape, NEG_BIG, F32)
        l_ref[...] = jnp.zeros(l_ref.shape, F32)
        acc_ref[...] = jnp.zeros(acc_ref.shape, F32)

    for o in range(n_out):
        pair = [h for h in range(len(heads)) if heads[h][3] == o]
        for pos, h in enumerate(pair):
            qs, ks, vs, _ = heads[h]
            q = q_ref[0, :, qs * LANES:(qs + 1) * LANES]
            k = k_ref[0, :, ks * LANES:(ks + 1) * LANES]
            st = _dot_nt(k, q)
            lo, hi = pos * 64, (pos + 1) * 64
            m, l, acc = m_ref[h], l_ref[h], acc_ref[o, lo:hi, :]
            for c0 in range(0, st.shape[0], KEY_CHUNK):
                rows = slice(c0, min(c0 + KEY_CHUNK, st.shape[0]))
                sc = st[rows]
                m_new = jnp.maximum(m, jnp.max(sc, axis=0, keepdims=True))
                a = jnp.exp2(m - m_new)
                pt = jnp.exp2(sc - m_new).astype(BF16)
                r = _dot(vt_ref[0, vs, :, rows], pt)
                l = a * l + r[LANES:LANES + 1]
                acc = acc * a + r[lo:hi]
                m = m_new
            m_ref[h], l_ref[h] = m, l
            acc_ref[o, lo:hi, :] = acc

    @pl.when(ki == pl.num_programs(2) - 1)
    def _():
        for o in range(n_out):
            pair = [h for h in range(len(heads)) if heads[h][3] == o]
            out_t = jnp.concatenate([acc_ref[o, 0:64, :] / l_ref[pair[0]],
                                     acc_ref[o, 64:128, :] / l_ref[pair[1]]], axis=0)
            o_ref[0, :, o * LANES:(o + 1) * LANES] = out_t.T


def _pick_tk(n_keys):
    best = 256
    for t in range(256, FLASH_TK_MAX + 1, 256):
        if n_keys % t == 0:
            best = t
    return best


def values_transposed(v):
    b, n, w = v.shape
    vt = jnp.transpose(v.reshape(b, n, w // LANES, LANES), (0, 2, 3, 1))
    return jnp.concatenate([vt, jnp.ones((b, w // LANES, VT_ROWS - LANES, n), v.dtype)], axis=2)


def flash_attention(q, k, vt, heads):
    b, s, qw = q.shape
    nk = k.shape[1]
    n_out = max(h[3] for h in heads) + 1
    tq = min(FLASH_TQ, s)
    tk = _pick_tk(nk)
    kern = functools.partial(_flash_kernel, heads=heads, n_out=n_out)
    return pl.pallas_call(
        kern,
        grid=(b, s // tq, nk // tk),
        in_specs=[pl.BlockSpec((1, tq, qw), lambda i, j, t: (i, j, 0)),
                  pl.BlockSpec((1, tk, k.shape[2]), lambda i, j, t: (i, t, 0)),
                  pl.BlockSpec((1, vt.shape[1], VT_ROWS, tk), lambda i, j, t: (i, 0, 0, t))],
        out_specs=pl.BlockSpec((1, tq, n_out * LANES), lambda i, j, t: (i, j, 0)),
        out_shape=jax.ShapeDtypeStruct((b, s, n_out * LANES), F32),
        scratch_shapes=[pltpu.VMEM((len(heads), 1, tq), F32),
                        pltpu.VMEM((len(heads), 1, tq), F32),
                        pltpu.VMEM((n_out, LANES, tq), F32)],
        compiler_params=_cparams(("parallel", "parallel", "arbitrary")),
        name="flash_attention",
    )(q, k, vt)


STAB_SHIFT = 100.0
BOUND_LIMIT = 113.0
BOUND_SLACK = 1.01


def _flash_bounded_kernel(stab_ref, q_ref, k_ref, vt_ref, o_ref, acc_ref, *, heads, n_out):
    ki = pl.program_id(2)

    @pl.when(ki == 0)
    def _():
        acc_ref[...] = jnp.zeros(acc_ref.shape, F32)

    stab = stab_ref[pl.program_id(0), 0]
    for h, (qs, ks, vs, _) in enumerate(heads):
        q = q_ref[0, :, qs * LANES:(qs + 1) * LANES]
        k = k_ref[0, :, ks * LANES:(ks + 1) * LANES]
        pt = jnp.exp2(_dot_nt(k, q) - stab).astype(BF16)
        acc_ref[h] += _dot(vt_ref[0, vs], pt)

    @pl.when(ki == pl.num_programs(2) - 1)
    def _():
        for o in range(n_out):
            a, bb = [h for h in range(len(heads)) if heads[h][3] == o]
            out_t = jnp.concatenate([acc_ref[a, 0:64, :] / acc_ref[a, LANES:LANES + 1, :],
                                     acc_ref[bb, 64:128, :] / acc_ref[bb, LANES:LANES + 1, :]], axis=0)
            o_ref[0, :, o * LANES:(o + 1) * LANES] = out_t.T


def flash_attention_bounded(stab, q, k, vt, heads):
    b, s, qw = q.shape
    nk = k.shape[1]
    n_out = max(h[3] for h in heads) + 1
    tq = min(FLASH_TQ, s)
    tk = _pick_tk(nk)
    kern = functools.partial(_flash_bounded_kernel, heads=heads, n_out=n_out)
    return pl.pallas_call(
        kern,
        grid=(b, s // tq, nk // tk),
        in_specs=[pl.BlockSpec(memory_space=pltpu.SMEM),
                  pl.BlockSpec((1, tq, qw), lambda i, j, t: (i, j, 0)),
                  pl.BlockSpec((1, tk, k.shape[2]), lambda i, j, t: (i, t, 0)),
                  pl.BlockSpec((1, vt.shape[1], VT_ROWS, tk), lambda i, j, t: (i, 0, 0, t))],
        out_specs=pl.BlockSpec((1, tq, n_out * LANES), lambda i, j, t: (i, j, 0)),
        out_shape=jax.ShapeDtypeStruct((b, s, n_out * LANES), F32),
        scratch_shapes=[pltpu.VMEM((len(heads), VT_ROWS, tq), F32)],
        compiler_params=_cparams(("parallel", "parallel", "arbitrary")),
        name="flash_attention_bounded",
    )(stab, q, k, vt)


def attention(q2, k2, q, k, vt, heads):
    bound = BOUND_SLACK * jnp.sqrt(q2 * k2)
    stab = (bound - STAB_SHIFT).reshape(-1, 1).astype(F32)
    return lax.cond(jnp.max(bound) <= BOUND_LIMIT,
                    lambda: flash_attention_bounded(stab, q, k, vt, heads),
                    lambda: flash_attention(q, k, vt, heads))


HEADS_PAIRED = tuple((h, h // 2, h // 2, h // 2) for h in range(4))
HEADS_MLA = tuple((h, h, h // 2, h // 2) for h in range(4))
HEADS_GQA = tuple((h, 0, 0, h // 2) for h in range(4))
HEADS_DIFF = tuple((j, j // 4, j // 4, (j % 2) * 2 + j // 4) for j in range(8))


NA_QROWS = 4
NA_TQ = NA_QROWS * GRID_W


def _na_kernel(q_ref, kp_ref, kc_ref, kn_ref, vp_ref, vc_ref, vn_ref, kx_ref, vx_ref, bias_ref, o_ref, *, rows):
    j = pl.program_id(1)
    kh = min(NA_WIN_H, rows)
    qi = lax.broadcasted_iota(jnp.int32, (NA_TQ, 1), 0)
    ki = lax.broadcasted_iota(jnp.int32, (1, NA_TQ), 1)
    wshift = GRID_W.bit_length() - 1
    r = j * NA_QROWS + (qi >> wshift)
    c = qi & (GRID_W - 1)
    r0 = jnp.clip(r - kh // 2, 0, rows - kh)
    c0 = jnp.clip(c - NA_WIN_W // 2, 0, GRID_W - NA_WIN_W)
    kcol = ki & (GRID_W - 1)
    col_ok = (kcol >= c0) & (kcol < c0 + NA_WIN_W)
    masks = []
    for d in (-1, 0, 1):
        kr = (j + d) * NA_QROWS + (ki >> wshift)
        masks.append(col_ok & (kr >= r0) & (kr < r0 + kh))
    lo_half = lax.broadcasted_iota(jnp.int32, (1, LANES), 1) < 64
    for o in range(2):
        k_loc = [kp_ref[0, :, o * LANES:(o + 1) * LANES], kc_ref[0, :, o * LANES:(o + 1) * LANES],
                 kn_ref[0, :, o * LANES:(o + 1) * LANES]]
        v_all = jnp.concatenate([vp_ref[0, :, o * LANES:(o + 1) * LANES], vc_ref[0, :, o * LANES:(o + 1) * LANES],
                                 vn_ref[0, :, o * LANES:(o + 1) * LANES], vx_ref[0, :, o * LANES:(o + 1) * LANES]],
                                axis=0)
        kx = kx_ref[0, :, o * LANES:(o + 1) * LANES]
        res = []
        for h in (2 * o, 2 * o + 1):
            q = q_ref[0, :, h * LANES:(h + 1) * LANES]
            parts = [jnp.where(masks[d], _dot_nt(q, k_loc[d]) + bias_ref[h, d], NEG_BIG) for d in range(3)]
            parts.append(_dot_nt(q, kx))
            s = jnp.concatenate(parts, axis=1)
            m = jnp.max(s, axis=-1, keepdims=True)
            p = jnp.exp2(s - m)
            l = jnp.sum(p, axis=-1, keepdims=True)
            res.append(_dot(p.astype(BF16), v_all) / l)
        o_ref[0, :, o * LANES:(o + 1) * LANES] = jnp.where(lo_half, res[0], res[1])


def na_attention(q, k, v, kx, vx, bias, rows):
    b, s, _ = q.shape
    nblk = rows // NA_QROWS
    nx = kx.shape[1]
    kern = functools.partial(_na_kernel, rows=rows)
    prev = lambda i, j: (i, jnp.maximum(j - 1, 0), 0)
    cur = lambda i, j: (i, j, 0)
    nxt = lambda i, j: (i, jnp.minimum(j + 1, nblk - 1), 0)
    kv_spec = lambda f: pl.BlockSpec((1, NA_TQ, 2 * LANES), f)
    return pl.pallas_call(
        kern,
        grid=(b, nblk),
        in_specs=[pl.BlockSpec((1, NA_TQ, 4 * LANES), cur),
                  kv_spec(prev), kv_spec(cur), kv_spec(nxt),
                  kv_spec(prev), kv_spec(cur), kv_spec(nxt),
                  pl.BlockSpec((1, nx, 2 * LANES), lambda i, j: (i, 0, 0)),
                  pl.BlockSpec((1, nx, 2 * LANES), lambda i, j: (i, 0, 0)),
                  pl.BlockSpec(bias.shape, lambda i, j: (0, 0, 0, 0))],
        out_specs=pl.BlockSpec((1, NA_TQ, 2 * LANES), cur),
        out_shape=jax.ShapeDtypeStruct((b, s, 2 * LANES), F32),
        compiler_params=_cparams(("parallel", "arbitrary")),
        name="na_attention",
    )(q, k, k, k, v, v, v, kx, vx, bias)


ROUTE_T = 256


def _outproj_kernel(x_ref, ona_ref, omla_ref, od_ref, ogqa_ref, mod_ref, gsub_ref, gffn_ref, lam_ref, mat_ref,
                    wout_ref, wr_ref, xo_ref, h_ref, lg_ref, *, lam_init):
    lam = lam_ref[0, 0]
    m64 = mat_ref[M_SEG64]
    pieces = [ona_ref[0].astype(BF16), omla_ref[0].astype(BF16)]
    dsl = []
    for i in range(2):
        d = od_ref[0, :, i * LANES:(i + 1) * LANES] - lam * od_ref[0, :, (2 + i) * LANES:(3 + i) * LANES]
        d = _seg_norm(d, m64, 1.0 / 64, gsub_ref[...]) * (1.0 - lam_init)
        dsl.append(d.astype(BF16))
    pieces += dsl + [ogqa_ref[0].astype(BF16)]
    o = jnp.concatenate(pieces, axis=1)
    y = _dot(o, wout_ref[...])
    x = x_ref[0] + mod_ref[0, 2:3, :] * y
    xo_ref[0] = x
    ms = jnp.mean(x * x, axis=-1, keepdims=True)
    h = x * lax.rsqrt(ms + NORM_EPS) * gffn_ref[...]
    h = h * (1.0 + mod_ref[0, 4:5, :]) + mod_ref[0, 3:4, :]
    h_ref[0] = h.astype(BF16)
    lg = _dot3_nt(wr_ref[...], h)
    for t in range(lg.shape[1] // ROUTE_T):
        lg_ref[0, t] = lg[:, t * ROUTE_T:(t + 1) * ROUTE_T]


def outproj_mod_router(x, o_na, o_mla, o_diff, o_gqa, mod, g_sub_t, g_ffn, lam, mats, w_out, w_router_t, lam_init):
    b, n, d = x.shape
    tm = min(512, n)
    nt = tm // ROUTE_T
    kern = functools.partial(_outproj_kernel, lam_init=lam_init)
    tok = lambda w: pl.BlockSpec((1, tm, w), lambda i, j: (i, j, 0))
    c2 = lambda i, j: (0, 0)
    return pl.pallas_call(
        kern,
        grid=(b, n // tm),
        in_specs=[tok(d), tok(256), tok(256), tok(512), tok(256),
                  pl.BlockSpec((1, 8, d), lambda i, j: (i, 0, 0)),
                  pl.BlockSpec((1, LANES), c2), pl.BlockSpec((1, d), c2),
                  pl.BlockSpec(memory_space=pltpu.SMEM),
                  pl.BlockSpec(mats.shape, lambda i, j: (0, 0, 0)),
                  pl.BlockSpec((d, d), c2), pl.BlockSpec((N_EXPERTS, d), c2)],
        out_specs=[tok(d), tok(d),
                   pl.BlockSpec((1, nt, N_EXPERTS, ROUTE_T), lambda i, j: (i, j, 0, 0))],
        out_shape=[jax.ShapeDtypeStruct((b, n, d), F32), jax.ShapeDtypeStruct((b, n, d), BF16),
                   jax.ShapeDtypeStruct((b, n // ROUTE_T, N_EXPERTS, ROUTE_T), F32)],
        compiler_params=_cparams(("parallel", "arbitrary")),
        name="outproj_mod_router",
    )(x, o_na, o_mla, o_diff, o_gqa, mod, g_sub_t, g_ffn.reshape(1, d), lam, mats, w_out, w_router_t)


def _route_kernel(lg_ref, tri_ref, aff_ref, pos_ref, off_ref, *, cap):
    nb = lg_ref.shape[1]
    lg = lg_ref[0]
    mx = jnp.max(lg, axis=1, keepdims=True)
    ex = jnp.exp(lg - mx)
    aff = ex / jnp.sum(ex, axis=1, keepdims=True)
    aff_ref[0] = aff
    bits = lax.bitcast_convert_type(aff, jnp.int32)

    def count_ge(t):
        hit = jnp.where(bits >= t[None], 1.0, 0.0)
        return jnp.sum(jnp.sum(hit, axis=0), axis=1, keepdims=True)

    def bis(i, t):
        cand = t | (jnp.int32(1) << (30 - i))
        return jnp.where(count_ge(cand) >= float(cap), cand, t)

    thr = lax.fori_loop(0, 31, bis, jnp.zeros((N_EXPERTS, 1), jnp.int32))
    need = float(cap) - count_ge(thr + 1)
    tri = tri_ref[...]

    def blk(jb, carry):
        c_eq, c_pos = carry
        bb = lax.bitcast_convert_type(aff_ref[0, jb], jnp.int32)
        gt = bb > thr
        eq = bb == thr
        eq_before = _dot(jnp.where(eq, 1.0, 0.0).astype(BF16), tri) + c_eq
        sel = gt | (eq & (eq_before < need))
        sel_f = jnp.where(sel, 1.0, 0.0)
        before = _dot(sel_f.astype(BF16), tri) + c_pos
        pos_ref[0, jb] = jnp.where(sel, before, -1.0).astype(jnp.int32)
        off_ref[0, jb] = jnp.broadcast_to(c_pos, (N_EXPERTS, LANES)).astype(jnp.int32)
        c_eq = c_eq + jnp.sum(jnp.where(eq, 1.0, 0.0), axis=1, keepdims=True)
        c_pos = c_pos + jnp.sum(sel_f, axis=1, keepdims=True)
        return c_eq, c_pos

    zero = jnp.zeros((N_EXPERTS, 1), F32)
    lax.fori_loop(0, nb, blk, (zero, zero))


def route(logits, tri, cap):
    b, nb, e, t = logits.shape
    kern = functools.partial(_route_kernel, cap=cap)
    spec = pl.BlockSpec((1, nb, e, t), lambda i: (i, 0, 0, 0))
    return pl.pallas_call(
        kern,
        grid=(b,),
        in_specs=[spec, pl.BlockSpec(tri.shape, lambda i: (0, 0))],
        out_specs=[spec, spec, pl.BlockSpec((1, nb, e, LANES), lambda i: (i, 0, 0, 0))],
        out_shape=[jax.ShapeDtypeStruct((b, nb, e, t), F32), jax.ShapeDtypeStruct((b, nb, e, t), jnp.int32),
                   jax.ShapeDtypeStruct((b, nb, e, LANES), jnp.int32)],
        compiler_params=_cparams(("arbitrary",)),
        name="route",
    )(logits, tri)


def _slot_tiles(off_ref, idx, st, n_tiles):
    off, nxt = off_ref[idx], off_ref[idx + 1]
    first = jnp.minimum(off // st, n_tiles - 1)
    last = jnp.minimum(jnp.maximum(nxt - 1, off) // st, n_tiles - 1)
    return first, last


def _gather_kernel(off_ref, h_ref, pos_ref, aff_ref, xe_ref, gs_ref, acc_ref, gacc_ref, *, st, n_tiles, nsub, nb):
    bi, e, ch = pl.program_id(0), pl.program_id(1), pl.program_id(2)

    @pl.when(ch == 0)
    def _():
        acc_ref[...] = jnp.zeros(acc_ref.shape, F32)
        gacc_ref[...] = jnp.zeros(gacc_ref.shape, F32)

    slot = lax.broadcasted_iota(jnp.int32, (st, ROUTE_T), 0)
    for jj in range(nsub):
        idx = (bi * N_EXPERTS + e) * (nb + 1) + ch * nsub + jj
        first, last = _slot_tiles(off_ref, idx, st, n_tiles)

        def add(tile, jj=jj):
            hit = slot == (pos_ref[0, jj, pl.ds(e, 1), :] - tile * st)
            base = pl.multiple_of(tile * st, st)
            acc_ref[pl.ds(base, st), :] += _dot(jnp.where(hit, 1.0, 0.0).astype(BF16),
                                                h_ref[0, jj * ROUTE_T:(jj + 1) * ROUTE_T, :])
            gacc_ref[pl.ds(base, st), :] += jnp.sum(jnp.where(hit, aff_ref[0, jj, pl.ds(e, 1), :], 0.0),
                                                    axis=1, keepdims=True)

        add(first)
        pl.when(last > first)(functools.partial(add, last))

    @pl.when(ch == pl.num_programs(2) - 1)
    def _():
        xe_ref[0, 0] = acc_ref[...].astype(BF16)
        gs_ref[0, 0] = gacc_ref[...]


def moe_gather(offs, h, pos, aff, cap):
    b, n, d = h.shape
    nb = n // ROUTE_T
    st = min(ROUTE_T, cap)
    n_tiles = cap // st
    chunk = min(2048, n)
    nsub = chunk // ROUTE_T
    kern = functools.partial(_gather_kernel, st=st, n_tiles=n_tiles, nsub=nsub, nb=nb)
    rspec = pl.BlockSpec((1, nsub, N_EXPERTS, ROUTE_T), lambda i, e, c, off: (i, c, 0, 0))
    return pl.pallas_call(
        kern,
        grid_spec=pltpu.PrefetchScalarGridSpec(
            num_scalar_prefetch=1,
            grid=(b, N_EXPERTS, n // chunk),
            in_specs=[pl.BlockSpec((1, chunk, d), lambda i, e, c, off: (i, c, 0)), rspec, rspec],
            out_specs=[pl.BlockSpec((1, 1, cap, d), lambda i, e, c, off: (i, e, 0, 0)),
                       pl.BlockSpec((1, 1, cap, 1), lambda i, e, c, off: (i, e, 0, 0))],
            scratch_shapes=[pltpu.VMEM((cap, d), F32), pltpu.VMEM((cap, 1), F32)]),
        out_shape=[jax.ShapeDtypeStruct((b, N_EXPERTS, cap, d), BF16),
                   jax.ShapeDtypeStruct((b, N_EXPERTS, cap, 1), F32)],
        compiler_params=_cparams(("parallel", "parallel", "arbitrary")),
        name="moe_gather",
    )(offs, h, pos, aff)


FF_TILE = 256


def _ffn_kernel(x_ref, g_ref, wg_ref, wu_ref, wd_ref, y_ref, acc_ref):
    f = pl.program_id(2)

    @pl.when(f == 0)
    def _():
        acc_ref[...] = jnp.zeros(acc_ref.shape, F32)

    x = x_ref[0, 0]
    gate = _dot(x, wg_ref[0, 0].astype(BF16))
    up = _dot(x, wu_ref[0, 0].astype(BF16))
    hmid = (_silu(gate) * up).astype(BF16)
    acc_ref[...] += _dot(hmid, wd_ref[0, 0].astype(BF16))

    @pl.when(f == pl.num_programs(2) - 1)
    def _():
        y_ref[0, 0] = (acc_ref[...] * g_ref[0, 0]).astype(BF16)


def moe_ffn(xe, gs, w_gate, w_up, w_down, layer):
    b, e, cap, d = xe.shape
    dff = w_gate.shape[3]
    return pl.pallas_call(
        _ffn_kernel,
        grid=(e, b, dff // FF_TILE),
        in_specs=[pl.BlockSpec((1, 1, cap, d), lambda ei, bi, f: (bi, ei, 0, 0)),
                  pl.BlockSpec((1, 1, cap, 1), lambda ei, bi, f: (bi, ei, 0, 0)),
                  pl.BlockSpec((1, 1, d, FF_TILE), lambda ei, bi, f: (layer, ei, 0, f)),
                  pl.BlockSpec((1, 1, d, FF_TILE), lambda ei, bi, f: (layer, ei, 0, f)),
                  pl.BlockSpec((1, 1, FF_TILE, d), lambda ei, bi, f: (layer, ei, f, 0))],
        out_specs=pl.BlockSpec((1, 1, cap, d), lambda ei, bi, f: (bi, ei, 0, 0)),
        out_shape=jax.ShapeDtypeStruct((b, e, cap, d), BF16),
        scratch_shapes=[pltpu.VMEM((cap, d), F32)],
        compiler_params=_cparams(("parallel", "parallel", "arbitrary")),
        name="moe_ffn",
    )(xe, gs, w_gate, w_up, w_down)


def _combine_kernel(off_ref, x_ref, ye_ref, pos_ref, mod_ref, o_ref, *, st, n_tiles, nsub, nb):
    bi, ch, e = pl.program_id(0), pl.program_id(1), pl.program_id(2)

    @pl.when(e == 0)
    def _():
        o_ref[...] = jnp.zeros(o_ref.shape, F32)

    slot = lax.broadcasted_iota(jnp.int32, (st, ROUTE_T), 0)
    for jj in range(nsub):
        idx = (bi * N_EXPERTS + e) * (nb + 1) + ch * nsub + jj
        first, last = _slot_tiles(off_ref, idx, st, n_tiles)

        def add(tile, jj=jj):
            hit = slot == (pos_ref[0, jj, pl.ds(e, 1), :] - tile * st)
            base = pl.multiple_of(tile * st, st)
            oh = jnp.where(hit, 1.0, 0.0).astype(BF16)
            o_ref[0, jj * ROUTE_T:(jj + 1) * ROUTE_T, :] += lax.dot_general(
                oh, ye_ref[0, 0, pl.ds(base, st), :], (((0,), (0,)), ((), ())), preferred_element_type=F32)

        add(first)
        pl.when(last > first)(functools.partial(add, last))

    @pl.when(e == pl.num_programs(2) - 1)
    def _():
        o_ref[0] = x_ref[0] + mod_ref[0, 5:6, :] * o_ref[0]


def moe_combine(offs, x, ye, pos, mod, cap, slot0):
    b, n, d = x.shape
    nb = n // ROUTE_T
    st = min(ROUTE_T, cap)
    n_tiles = cap // st
    chunk = min(2048, n)
    nsub = chunk // ROUTE_T
    assert slot0 % cap == 0
    kern = functools.partial(_combine_kernel, st=st, n_tiles=n_tiles, nsub=nsub, nb=nb)
    return pl.pallas_call(
        kern,
        grid_spec=pltpu.PrefetchScalarGridSpec(
            num_scalar_prefetch=1,
            grid=(b, n // chunk, N_EXPERTS),
            in_specs=[pl.BlockSpec((1, chunk, d), lambda i, c, e, off: (i, c, 0)),
                      pl.BlockSpec((1, 1, cap, d), lambda i, c, e, off: (i, e, slot0 // cap, 0)),
                      pl.BlockSpec((1, nsub, N_EXPERTS, ROUTE_T), lambda i, c, e, off: (i, c, 0, 0)),
                      pl.BlockSpec((1, 8, d), lambda i, c, e, off: (i, 0, 0))],
            out_specs=pl.BlockSpec((1, chunk, d), lambda i, c, e, off: (i, c, 0))),
        out_shape=jax.ShapeDtypeStruct((b, n, d), F32),
        compiler_params=_cparams(("parallel", "parallel", "arbitrary")),
        name="moe_combine",
    )(offs, x, ye, pos, mod)


def expert_choice_ffn(sets, tri, w_gate, w_up, w_down, layer):
    routed = []
    for x, h, logits, mod in sets:
        b, n, _ = x.shape
        cap = EC_CAPACITY * n // N_EXPERTS
        aff, pos, off = route(logits, tri, cap)
        first = jnp.transpose(off[..., 0], (0, 2, 1))
        offs = jnp.concatenate([first, jnp.full((b, N_EXPERTS, 1), cap, jnp.int32)], axis=2).reshape(-1)
        xe, gs = moe_gather(offs, h, pos, aff, cap)
        routed.append((offs, pos, cap, xe, gs))
    xe_all = jnp.concatenate([r[3] for r in routed], axis=2) if len(routed) > 1 else routed[0][3]
    gs_all = jnp.concatenate([r[4] for r in routed], axis=2) if len(routed) > 1 else routed[0][4]
    ye = moe_ffn(xe_all, gs_all, w_gate, w_up, w_down, layer)
    outs, slot0 = [], 0
    for (x, _, _, mod), (offs, pos, cap, _, _) in zip(sets, routed):
        outs.append(moe_combine(offs, x, ye, pos, mod, cap, slot0))
        slot0 += cap
    return outs


def _rope_tables(n_rows_grid):
    s = n_rows_grid * GRID_W
    t = np.arange(s)
    row, col = (t // GRID_W).astype(np.float64), (t % GRID_W).astype(np.float64)

    def unit(n):
        half = n // 2
        inv = ROPE_THETA ** (-np.arange(0, n, 2, dtype=np.float64) / n)
        out = []
        for pos in (row, col):
            ang = pos[:, None] * inv[None, :]
            c, sn = np.cos(ang), np.sin(ang)
            z = np.zeros_like(sn)
            out.append((np.concatenate([c, c], 1), np.concatenate([-sn, z], 1), np.concatenate([z, sn], 1)))
        return [np.concatenate([out[0][i], out[1][i]], 1) for i in range(3)]

    ident = lambda w: (np.ones((s, w), np.float32), np.zeros((s, w), np.float32), np.zeros((s, w), np.float32))
    u16 = unit(16)
    u32 = unit(32)
    tabs = []
    idt = ident(64)
    idt32 = ident(32)
    for i in range(3):
        tabs.append(np.concatenate([idt[i], u16[i], idt32[i]], 1))
    for i in range(3):
        tabs.append(np.concatenate([u16[i]] * 4, 1))
    for i in range(3):
        tabs.append(np.concatenate([u32[i]] * 2, 1))
    return np.stack(tabs).astype(np.float32)


def _identity_tables(n):
    one, zero = np.ones((n, LANES), np.float32), np.zeros((n, LANES), np.float32)
    return np.stack([one, zero, zero] * 3)


def _seg_matrices():
    i = np.arange(LANES)
    ones = np.ones((LANES, LANES), np.float32)
    m64 = (i[:, None] // 64 == i[None, :] // 64).astype(np.float32)
    m32 = (i[:, None] // 32 == i[None, :] // 32).astype(np.float32)
    return np.stack([ones, m64, m32])


def _na_bias(rpb):
    c = np.arange(GRID_W)
    rl = np.arange(NA_QROWS)
    dc = np.clip(c[None, :] - c[:, None] + NA_WIN_W - 1, 0, 2 * NA_WIN_W - 2)
    d = np.array([-1, 0, 1])
    dr = np.clip(NA_QROWS * d[:, None, None] + rl[None, None, :] - rl[None, :, None] + NA_WIN_H - 1,
                 0, 2 * NA_WIN_H - 2)
    cols = jnp.take(rpb.astype(F32) * LOG2E, jnp.asarray(dc), axis=2)
    full = jnp.take(cols, jnp.asarray(dr), axis=1)
    return jnp.transpose(full, (0, 1, 2, 4, 3, 5)).reshape(rpb.shape[0], 3, NA_TQ, NA_TQ)


def _pad_cols(w, width):
    return jnp.pad(w, ((0, 0), (0, width - w.shape[1])))


def _layer_params(l, w_in, g_na_q, g_na_k, na_rpb, g_mla_cq, w_mla_uq, g_mla_q, g_mla_ckv, w_mla_ukv,
                  g_mla_k_nope, g_mla_k_rope, g_diff_q, g_diff_k, g_diff_sub, g_gqa_q, g_gqa_k, w_out):
    wi = w_in[l]
    (naq, nak, nav, cq, ckv, kr, dq, dk, dv, gq, gk, gv) = jnp.split(
        wi, np.cumsum([256, 256, 256, 256, 128, 32, 256, 256, 256, 256, 128])[:], axis=1)
    gq4 = gq.reshape(-1, 4, 64)[:, jnp.array([0, 2, 1, 3])].reshape(-1, 256)
    zeros = lambda w: jnp.zeros((wi.shape[0], w), wi.dtype)
    w_in_r = jnp.concatenate([naq, nak, nav, cq, ckv, dq, dk, dv, gq4, gk, gv, zeros(64), kr, zeros(32)],
                             axis=1).astype(BF16)
    uq = w_mla_uq[l].reshape(MLA_Q_RANK, 4, 96)
    wuq = jnp.pad(uq, ((0, 0), (0, 0), (0, 32))).reshape(MLA_Q_RANK, 512).astype(BF16)
    ukv = w_mla_ukv[l].reshape(MLA_KV_RANK, 4, 128)
    wuk = jnp.pad(ukv[:, :, :64], ((0, 0), (0, 0), (0, 64))).reshape(MLA_KV_RANK, 512).astype(BF16)
    wuv = ukv[:, :, 64:].reshape(MLA_KV_RANK, 256).astype(BF16)
    row = lambda v: jnp.pad(v, (0, 512 - v.shape[0]))
    z32, z64 = jnp.zeros((32,), F32), jnp.zeros((64,), F32)
    gains = jnp.stack([
        row(jnp.tile(g_na_q[l], 4)), row(jnp.tile(g_na_k[l], 4)), row(g_mla_cq[l]),
        row(jnp.tile(jnp.concatenate([g_mla_q[l], z32]), 4)), row(g_mla_ckv[l]),
        row(jnp.tile(jnp.concatenate([g_mla_k_nope[l], z64]), 4)),
        row(jnp.concatenate([z64, g_mla_k_rope[l], z32])),
        row(jnp.tile(g_diff_q[l].reshape(-1), 4)), row(jnp.tile(g_diff_k[l].reshape(-1), 4)),
        row(jnp.tile(g_gqa_q[l], 4)), row(jnp.tile(g_gqa_k[l], 2))] + [jnp.zeros((512,), F32)] * 5)
    wo = w_out[l]
    wo_g = wo[768:].reshape(4, 64, -1)[jnp.array([0, 2, 1, 3])].reshape(256, -1)
    w_out_r = jnp.concatenate([wo[:768], wo_g], axis=0).astype(BF16)
    return dict(w_in=w_in_r, wuq=wuq, wuk=wuk, wuv=wuv, gains=gains, w_out=w_out_r,
                bias=_na_bias(na_rpb[l]), g_sub=jnp.tile(g_diff_sub[l], 2).reshape(1, LANES))


def kernel(x, c, ctx, c_ctx, w_mod, b_mod, g_attn, g_ffn, w_in, g_na_q, g_na_k, na_rpb, g_mla_cq, w_mla_uq, g_mla_q, g_mla_ckv, w_mla_ukv, g_mla_k_nope, g_mla_k_rope, g_diff_q, g_diff_k, diff_lambda, g_diff_sub, g_gqa_q, g_gqa_k, w_out, w_router, w_gate, w_up, w_down):
    b, s, d = x.shape
    n_ctx = ctx.shape[1]
    rows = s // GRID_W
    tabs = jnp.asarray(_rope_tables(rows))
    tabs_ctx = jnp.asarray(_identity_tables(n_ctx))
    mats = jnp.asarray(_seg_matrices()).astype(BF16)
    tri = jnp.asarray(np.triu(np.ones((ROUTE_T, ROUTE_T), np.float32), 1)).astype(BF16)
    c_rows = jnp.concatenate([c, c_ctx[None], jnp.zeros((8 - b - 1, d), F32)], axis=0)
    xc = ctx
    for l in range(DEPTH):
        need_ctx = l < DEPTH - 1
        lam_init = 0.8 - 0.6 * math.exp(-0.3 * l)
        lp = diff_lambda[l].astype(F32)
        lam = (jnp.exp(jnp.sum(lp[0] * lp[1])) - jnp.exp(jnp.sum(lp[2] * lp[3])) + lam_init).reshape(1, 1)
        prm = _layer_params(l, w_in, g_na_q, g_na_k, na_rpb, g_mla_cq, w_mla_uq, g_mla_q, g_mla_ckv, w_mla_ukv,
                            g_mla_k_nope, g_mla_k_rope, g_diff_q, g_diff_k, g_diff_sub, g_gqa_q, g_gqa_k, w_out)
        modv = mod_vectors(c_rows, w_mod[l], b_mod[l]).reshape(8, N_MOD, d)
        mod = jnp.pad(modv[:b], ((0, 0), (0, 2), (0, 0)))
        mod_c = jnp.broadcast_to(jnp.pad(modv[b:b + 1], ((0, 0), (0, 2), (0, 0))), (b, 8, d))
        w_router_t = w_router[l].T

        def mix_inputs(xin, m, tb):
            p = ln_mod_proj(xin, g_attn[l], m[:, 0:1], m[:, 1:2], prm["w_in"])
            return prep(p, tb, prm["gains"], mats, prm["wuq"], prm["wuk"], prm["wuv"])

        (naq, nak, nav, mq, mk, mv, dq, dk, dv, gq, gk, gv, stat) = mix_inputs(x, mod, tabs)
        (naq_c, nak_c, nav_c, mq_c, mk_c, mv_c, dq_c, dk_c, dv_c, gq_c, gk_c, gv_c, stat_c) = mix_inputs(
            xc, mod_c, tabs_ctx)
        cat = lambda a, bb: jnp.concatenate([a, bb], axis=1)
        n2 = jnp.max(stat, axis=(1, 3))
        n2c = jnp.max(stat_c, axis=(1, 3))
        q2 = lambda name: n2[:, STAT_ROWS.index(name)]
        k2 = lambda name: jnp.maximum(n2[:, STAT_ROWS.index(name)], n2c[:, STAT_ROWS.index(name)])

        ext = values_transposed
        o_na = na_attention(naq, nak, nav, nak_c, nav_c, prm["bias"], rows)
        o_mla = attention(q2("mq"), k2("mk"), mq, cat(mk_c, mk), ext(cat(mv_c, mv)), HEADS_MLA)
        o_diff = attention(q2("dq"), k2("dk"), dq, cat(dk_c, dk), ext(cat(dv_c, dv)), HEADS_DIFF)
        o_gqa = attention(q2("gq"), k2("gk"), gq, cat(gk_c, gk), ext(cat(gv_c, gv)), HEADS_GQA)
        x_mid, h2, logits = outproj_mod_router(x, o_na, o_mla, o_diff, o_gqa, mod, prm["g_sub"], g_ffn[l], lam,
                                               mats, prm["w_out"], w_router_t, lam_init)
        if need_ctx:
            oc_na = flash_attention(naq_c, nak_c, ext(nav_c), HEADS_PAIRED)
            oc_mla = flash_attention(mq_c, mk_c, ext(mv_c), HEADS_MLA)
            oc_diff = flash_attention(dq_c, dk_c, ext(dv_c), HEADS_DIFF)
            oc_gqa = flash_attention(gq_c, gk_c, ext(gv_c), HEADS_GQA)
            xc_mid, hc2, logits_c = outproj_mod_router(xc, oc_na, oc_mla, oc_diff, oc_gqa, mod_c, prm["g_sub"],
                                                       g_ffn[l], lam, mats, prm["w_out"], w_router_t, lam_init)
            x, xc = expert_choice_ffn([(x_mid, h2, logits, mod), (xc_mid, hc2, logits_c, mod_c)],
                                      tri, w_gate, w_up, w_down, l)
        else:
            (x,) = expert_choice_ffn([(x_mid, h2, logits, mod)], tri, w_gate, w_up, w_down, l)
    return x
```

```python
import functools
import math

import numpy as np
import jax
import jax.numpy as jnp
from jax import lax
from jax.experimental import pallas as pl
from jax.experimental.pallas import tpu as pltpu

D_MODEL = 1024
GRID_W = 64
HEAD_DIM = 64
N_HEADS = 4
NA_WIN_H = 8
NA_WIN_W = 16
MLA_Q_RANK = 256
MLA_KV_RANK = 128
MLA_NOPE_DIM = 64
MLA_ROPE_DIM = 32
MLA_V_DIM = 64
DIFF_QK_DIM = 32
N_EXPERTS = 16
EC_CAPACITY = 2
D_FF = 2816
ROPE_THETA = 10000.0
NORM_EPS = 1e-6
N_MOD = 6
DEPTH = 2

LANES = 128
P_COLS = 2560
VMEM_LIMIT = 56 * 1024 * 1024
NEG_BIG = -1e30
LOG2E = math.log2(math.e)

F32 = jnp.float32
BF16 = jnp.bfloat16


def _cparams(sem):
    return pltpu.CompilerParams(dimension_semantics=sem, vmem_limit_bytes=VMEM_LIMIT)


def _split(a):
    hi = a.astype(BF16)
    lo = (a - hi.astype(F32)).astype(BF16)
    return hi, lo


def _dot(a, b):
    return jnp.dot(a, b, preferred_element_type=F32)


def _dot_nt(a, b):
    return lax.dot_general(a, b, (((1,), (1,)), ((), ())), preferred_element_type=F32)


def _dot3(a, b):
    ah, al = _split(a)
    bh, bl = _split(b)
    return _dot(ah, bh) + _dot(ah, bl) + _dot(al, bh)


def _dot3_nt(a, b):
    ah, al = _split(a)
    bh, bl = _split(b)
    return _dot_nt(ah, bh) + _dot_nt(ah, bl) + _dot_nt(al, bh)


def _silu(v):
    return v * jax.nn.sigmoid(v)


def _mod_kernel(c_ref, w_ref, b_ref, o_ref):
    o_ref[...] = _dot3(_silu(c_ref[...]), w_ref[...]) + b_ref[...]


def mod_vectors(c_rows, w_mod, b_mod):
    m, d = c_rows.shape
    n = w_mod.shape[1]
    tn = 1536
    return pl.pallas_call(
        _mod_kernel,
        grid=(n // tn,),
        in_specs=[pl.BlockSpec((m, d), lambda j: (0, 0)),
                  pl.BlockSpec((d, tn), lambda j: (0, j)),
                  pl.BlockSpec((1, tn), lambda j: (0, j))],
        out_specs=pl.BlockSpec((m, tn), lambda j: (0, j)),
        out_shape=jax.ShapeDtypeStruct((m, n), F32),
        compiler_params=_cparams(("arbitrary",)),
        name="mod_vectors",
    )(c_rows, w_mod, b_mod.reshape(1, n))


def _ln_proj_kernel(x_ref, g_ref, sh_ref, sc_ref, w_ref, o_ref):
    x = x_ref[0]
    ms = jnp.mean(x * x, axis=-1, keepdims=True)
    y = x * lax.rsqrt(ms + NORM_EPS) * g_ref[...]
    h = y * (1.0 + sc_ref[0]) + sh_ref[0]
    o_ref[0] = _dot(h.astype(BF16), w_ref[...])


def ln_mod_proj(x, g, shift, scale, w):
    b, n, d = x.shape
    ncol = w.shape[1]
    tm = min(512, n)
    return pl.pallas_call(
        _ln_proj_kernel,
        grid=(b, n // tm),
        in_specs=[pl.BlockSpec((1, tm, d), lambda i, j: (i, j, 0)),
                  pl.BlockSpec((1, d), lambda i, j: (0, 0)),
                  pl.BlockSpec((1, 1, d), lambda i, j: (i, 0, 0)),
                  pl.BlockSpec((1, 1, d), lambda i, j: (i, 0, 0)),
                  pl.BlockSpec((d, ncol), lambda i, j: (0, 0))],
        out_specs=pl.BlockSpec((1, tm, ncol), lambda i, j: (i, j, 0)),
        out_shape=jax.ShapeDtypeStruct((b, n, ncol), F32),
        compiler_params=_cparams(("parallel", "arbitrary")),
        name="ln_mod_proj",
    )(x, g.reshape(1, d), shift, scale, w)


P_NAQ, P_NAK, P_NAV, P_CQ, P_CKV = 0, 256, 512, 768, 1024
P_DQ, P_DK, P_DV, P_GQ, P_GK, P_GV, P_KR = 1152, 1408, 1664, 1920, 2176, 2304, 2432
(G_NAQ, G_NAK, G_CQ, G_MQ, G_CKV, G_KN, G_KR, G_DQ, G_DK, G_GQ, G_GK) = range(11)
M_ONES, M_SEG64, M_SEG32 = 0, 1, 2
T_MLA, T_DIFF, T_GQA = 0, 3, 6


def _seg_norm(x, mat, inv_n, g):
    sq = x * x
    hi, lo = _split(sq)
    ms = (_dot(hi, mat) + _dot(lo, mat)) * inv_n
    return x * lax.rsqrt(ms + NORM_EPS) * g


def _rope(x, tab_ref, t0, half):
    c, s1, s2 = tab_ref[t0], tab_ref[t0 + 1], tab_ref[t0 + 2]
    return (x * c + pltpu.roll(x, LANES - half, axis=1) * s1
            + pltpu.roll(x, half, axis=1) * s2)


def _prep_kernel(p_ref, tab_ref, gain_ref, mat_ref, wuq_ref, wuk_ref, wuv_ref,
                 naq_ref, nak_ref, nav_ref, mq_ref, mk_ref, mv_ref,
                 dq_ref, dk_ref, dv_ref, gq_ref, gk_ref, gv_ref, st_ref):
    lane = lax.broadcasted_iota(jnp.int32, (1, LANES), 1)
    lo_half = lane < 64
    ones_m, m64, m32 = mat_ref[M_ONES], mat_ref[M_SEG64], mat_ref[M_SEG32]
    norm2 = {}

    def slab(off, i):
        return p_ref[0, :, off + i * LANES: off + (i + 1) * LANES]

    def track(name, xb, mat):
        xf = xb.astype(F32)
        hi, lo = _split(xf * xf)
        cur = jnp.max(_dot(hi, mat) + _dot(lo, mat), axis=0, keepdims=True)
        norm2[name] = jnp.maximum(norm2[name], cur) if name in norm2 else cur
        return xb

    def gain(row, i):
        return gain_ref[row:row + 1, i * LANES:(i + 1) * LANES]

    s_na = HEAD_DIM ** -0.5 * LOG2E
    for i in range(2):
        q = _seg_norm(slab(P_NAQ, i), m64, 1.0 / 64, gain(G_NAQ, i)) * s_na
        naq_ref[0, :, (2 * i) * LANES:(2 * i + 1) * LANES] = jnp.where(lo_half, q, 0.0).astype(BF16)
        naq_ref[0, :, (2 * i + 1) * LANES:(2 * i + 2) * LANES] = jnp.where(lo_half, 0.0, q).astype(BF16)
        k = _seg_norm(slab(P_NAK, i), m64, 1.0 / 64, gain(G_NAK, i))
        nak_ref[0, :, i * LANES:(i + 1) * LANES] = k.astype(BF16)
        nav_ref[0, :, i * LANES:(i + 1) * LANES] = slab(P_NAV, i).astype(BF16)

    cq = p_ref[0, :, P_CQ:P_CQ + MLA_Q_RANK]
    cq = cq * lax.rsqrt(jnp.mean(cq * cq, axis=-1, keepdims=True) + NORM_EPS) * gain_ref[G_CQ:G_CQ + 1, :MLA_Q_RANK]
    uq = _dot(cq.astype(BF16), wuq_ref[...])
    s_mla = (MLA_NOPE_DIM + MLA_ROPE_DIM) ** -0.5 * LOG2E
    ckv = p_ref[0, :, P_CKV:P_CKV + MLA_KV_RANK]
    ckv = ckv * lax.rsqrt(jnp.mean(ckv * ckv, axis=-1, keepdims=True) + NORM_EPS) * gain_ref[G_CKV:G_CKV + 1, :MLA_KV_RANK]
    ckv_b = ckv.astype(BF16)
    uk = _dot(ckv_b, wuk_ref[...])
    mv_ref[0] = _dot(ckv_b, wuv_ref[...]).astype(BF16)
    kr = _seg_norm(slab(P_KR, 0), ones_m, 1.0 / MLA_ROPE_DIM, gain(G_KR, 0))
    kr = _rope(kr, tab_ref, T_MLA, 8)
    for h in range(N_HEADS):
        q = _seg_norm(uq[:, h * LANES:(h + 1) * LANES], ones_m, 1.0 / (MLA_NOPE_DIM + MLA_ROPE_DIM), gain(G_MQ, h))
        q = _rope(q, tab_ref, T_MLA, 8) * s_mla
        mq_ref[0, :, h * LANES:(h + 1) * LANES] = track("mq", q.astype(BF16), ones_m)
        kn = _seg_norm(uk[:, h * LANES:(h + 1) * LANES], ones_m, 1.0 / MLA_NOPE_DIM, gain(G_KN, h))
        mk_ref[0, :, h * LANES:(h + 1) * LANES] = track("mk", (kn + kr).astype(BF16), ones_m)

    s_d = DIFF_QK_DIM ** -0.5 * LOG2E
    seg = lane >> 5
    for i in range(2):
        q = _seg_norm(slab(P_DQ, i), m32, 1.0 / 32, gain(G_DQ, i))
        q = _rope(q, tab_ref, T_DIFF, 8) * s_d
        track("dq", q.astype(BF16), m32)
        for j in range(4):
            dq_ref[0, :, (4 * i + j) * LANES:(4 * i + j + 1) * LANES] = jnp.where(seg == j, q, 0.0).astype(BF16)
        k = _seg_norm(slab(P_DK, i), m32, 1.0 / 32, gain(G_DK, i))
        dk_ref[0, :, i * LANES:(i + 1) * LANES] = track("dk", _rope(k, tab_ref, T_DIFF, 8).astype(BF16), m32)
        dv_ref[0, :, i * LANES:(i + 1) * LANES] = slab(P_DV, i).astype(BF16)

    s_g = HEAD_DIM ** -0.5 * LOG2E
    for i in range(2):
        q = _seg_norm(slab(P_GQ, i), m64, 1.0 / 64, gain(G_GQ, i))
        q = _rope(q, tab_ref, T_GQA, 16) * s_g
        track("gq", q.astype(BF16), m64)
        gq_ref[0, :, (2 * i) * LANES:(2 * i + 1) * LANES] = jnp.where(lo_half, q, 0.0).astype(BF16)
        gq_ref[0, :, (2 * i + 1) * LANES:(2 * i + 2) * LANES] = jnp.where(lo_half, 0.0, q).astype(BF16)
    k = _seg_norm(slab(P_GK, 0), m64, 1.0 / 64, gain(G_GK, 0))
    gk_ref[0] = track("gk", _rope(k, tab_ref, T_GQA, 16).astype(BF16), m64)
    gv_ref[0] = slab(P_GV, 0).astype(BF16)
    st_ref[0, 0] = jnp.concatenate([norm2[n] for n in STAT_ROWS] + [jnp.zeros((2, LANES), F32)], axis=0)


PREP_WIDTHS = (512, 256, 256, 512, 512, 256, 1024, 256, 256, 512, 128, 128)
STAT_ROWS = ("mq", "mk", "dq", "dk", "gq", "gk")


def prep(p, tabs, gains, mats, wuq, wuk, wuv):
    b, n, _ = p.shape
    tm = min(512, n)
    const2 = lambda i, j: (0, 0)
    stat_spec = pl.BlockSpec((1, 1, 8, LANES), lambda i, j: (i, j, 0, 0))
    stat_shape = jax.ShapeDtypeStruct((b, n // tm, 8, LANES), F32)
    return pl.pallas_call(
        _prep_kernel,
        grid=(b, n // tm),
        in_specs=[pl.BlockSpec((1, tm, P_COLS), lambda i, j: (i, j, 0)),
                  pl.BlockSpec((9, tm, LANES), lambda i, j: (0, j, 0)),
                  pl.BlockSpec(gains.shape, const2),
                  pl.BlockSpec(mats.shape, lambda i, j: (0, 0, 0)),
                  pl.BlockSpec(wuq.shape, const2),
                  pl.BlockSpec(wuk.shape, const2),
                  pl.BlockSpec(wuv.shape, const2)],
        out_specs=[pl.BlockSpec((1, tm, w), lambda i, j: (i, j, 0)) for w in PREP_WIDTHS] + [stat_spec],
        out_shape=[jax.ShapeDtypeStruct((b, n, w), BF16) for w in PREP_WIDTHS] + [stat_shape],
        compiler_params=_cparams(("parallel", "arbitrary")),
        name="prep",
    )(p, tabs, gains, mats, wuq, wuk, wuv)


VT_ROWS = 64 + 16


def _value_head(heads, h):
    pair = [g for g in range(len(heads)) if heads[g][3] == heads[h][3]]
    return 2 * heads[h][2] + pair.index(h)
KEY_CHUNK = 256
FLASH_TQ = 512
FLASH_TK_MAX = 3584


def _flash_kernel(q_ref, k_ref, vt_ref, o_ref, m_ref, l_ref, acc_ref, *, heads, n_out):
    ki = pl.program_id(2)

    @pl.when(ki == 0)
    def _():
        m_ref[...] = jnp.full(m_ref.shape, NEG_BIG, F32)
        l_ref[...] = jnp.zeros(l_ref.shape, F32)
        acc_ref[...] = jnp.zeros(acc_ref.shape, F32)

    for o in range(n_out):
        pair = [h for h in range(len(heads)) if heads[h][3] == o]
        for pos, h in enumerate(pair):
            qs, ks, vs, _ = heads[h]
            q = q_ref[0, :, qs * LANES:(qs + 1) * LANES]
            k = k_ref[0, :, ks * LANES:(ks + 1) * LANES]
            st = _dot_nt(k, q)
            lo, hi = pos * 64, (pos + 1) * 64
            vh = _value_head(heads, h)
            m, l, acc = m_ref[h], l_ref[h], acc_ref[o, lo:hi, :]
            for c0 in range(0, st.shape[0], KEY_CHUNK):
                rows = slice(c0, min(c0 + KEY_CHUNK, st.shape[0]))
                sc = st[rows]
                m_new = jnp.maximum(m, jnp.max(sc, axis=0, keepdims=True))
                a = jnp.exp2(m - m_new)
                pt = jnp.exp2(sc - m_new).astype(BF16)
                r = _dot(vt_ref[0, vh, :, rows], pt)
                l = a * l + r[64:65]
                acc = acc * a + r[0:64]
                m = m_new
            m_ref[h], l_ref[h] = m, l
            acc_ref[o, lo:hi, :] = acc

    @pl.when(ki == pl.num_programs(2) - 1)
    def _():
        for o in range(n_out):
            pair = [h for h in range(len(heads)) if heads[h][3] == o]
            out_t = jnp.concatenate([acc_ref[o, 0:64, :] / l_ref[pair[0]],
                                     acc_ref[o, 64:128, :] / l_ref[pair[1]]], axis=0)
            o_ref[0, :, o * LANES:(o + 1) * LANES] = out_t.T


def _pick_tk(n_keys):
    best = 256
    for t in range(256, FLASH_TK_MAX + 1, 256):
        if n_keys % t == 0:
            best = t
    return best


def values_transposed(v):
    b, n, w = v.shape
    vt = jnp.transpose(v.reshape(b, n, w // 64, 64), (0, 2, 3, 1))
    return jnp.concatenate([vt, jnp.ones((b, w // 64, VT_ROWS - 64, n), v.dtype)], axis=2)


def flash_attention(q, k, vt, heads):
    b, s, qw = q.shape
    nk = k.shape[1]
    n_out = max(h[3] for h in heads) + 1
    tq = min(FLASH_TQ, s)
    tk = _pick_tk(nk)
    kern = functools.partial(_flash_kernel, heads=heads, n_out=n_out)
    return pl.pallas_call(
        kern,
        grid=(b, s // tq, nk // tk),
        in_specs=[pl.BlockSpec((1, tq, qw), lambda i, j, t: (i, j, 0)),
                  pl.BlockSpec((1, tk, k.shape[2]), lambda i, j, t: (i, t, 0)),
                  pl.BlockSpec((1, vt.shape[1], VT_ROWS, tk), lambda i, j, t: (i, 0, 0, t))],
        out_specs=pl.BlockSpec((1, tq, n_out * LANES), lambda i, j, t: (i, j, 0)),
        out_shape=jax.ShapeDtypeStruct((b, s, n_out * LANES), F32),
        scratch_shapes=[pltpu.VMEM((len(heads), 1, tq), F32),
                        pltpu.VMEM((len(heads), 1, tq), F32),
                        pltpu.VMEM((n_out, LANES, tq), F32)],
        compiler_params=_cparams(("parallel", "parallel", "arbitrary")),
        name="flash_attention",
    )(q, k, vt)


STAB_SHIFT = 100.0
BOUND_LIMIT = 113.0
BOUND_SLACK = 1.01
BOUNDED_TQ = 1024


def _flash_bounded_kernel(stab_ref, q_ref, k_ref, vt_ref, o_ref, acc_ref, *, heads, n_out):
    ki = pl.program_id(2)

    @pl.when(ki == 0)
    def _():
        acc_ref[...] = jnp.zeros(acc_ref.shape, F32)

    stab = stab_ref[pl.program_id(0), 0]
    for h, (qs, ks, _, _) in enumerate(heads):
        q = q_ref[0, :, qs * LANES:(qs + 1) * LANES]
        k = k_ref[0, :, ks * LANES:(ks + 1) * LANES]
        pt = jnp.exp2(_dot_nt(k, q) - stab).astype(BF16)
        acc_ref[h] += _dot(vt_ref[0, _value_head(heads, h)], pt)

    @pl.when(ki == pl.num_programs(2) - 1)
    def _():
        for o in range(n_out):
            pair = [h for h in range(len(heads)) if heads[h][3] == o]
            out_t = jnp.concatenate([acc_ref[h, 0:64, :] / acc_ref[h, 64:65, :] for h in pair], axis=0)
            o_ref[0, :, o * LANES:(o + 1) * LANES] = out_t.T


def flash_attention_bounded(stab, q, k, vt, heads):
    b, s, qw = q.shape
    nk = k.shape[1]
    n_out = max(h[3] for h in heads) + 1
    tq = min(BOUNDED_TQ, s)
    tk = _pick_tk(nk)
    kern = functools.partial(_flash_bounded_kernel, heads=heads, n_out=n_out)
    return pl.pallas_call(
        kern,
        grid=(b, s // tq, nk // tk),
        in_specs=[pl.BlockSpec(memory_space=pltpu.SMEM),
                  pl.BlockSpec((1, tq, qw), lambda i, j, t: (i, j, 0)),
                  pl.BlockSpec((1, tk, k.shape[2]), lambda i, j, t: (i, t, 0)),
                  pl.BlockSpec((1, vt.shape[1], VT_ROWS, tk), lambda i, j, t: (i, 0, 0, t))],
        out_specs=pl.BlockSpec((1, tq, n_out * LANES), lambda i, j, t: (i, j, 0)),
        out_shape=jax.ShapeDtypeStruct((b, s, n_out * LANES), F32),
        scratch_shapes=[pltpu.VMEM((len(heads), VT_ROWS, tq), F32)],
        compiler_params=_cparams(("parallel", "parallel", "arbitrary")),
        name="flash_attention_bounded",
    )(stab, q, k, vt)


def attention(q2, k2, q, k, vt, heads):
    bound = BOUND_SLACK * jnp.sqrt(q2 * k2)
    stab = (bound - STAB_SHIFT).reshape(-1, 1).astype(F32)
    return lax.cond(jnp.max(bound) <= BOUND_LIMIT,
                    lambda: flash_attention_bounded(stab, q, k, vt, heads),
                    lambda: flash_attention(q, k, vt, heads))


HEADS_PAIRED = tuple((h, h // 2, h // 2, h // 2) for h in range(4))
HEADS_MLA = tuple((h, h, h // 2, h // 2) for h in range(4))
HEADS_GQA = tuple((h, 0, 0, h // 2) for h in range(4))
HEADS_DIFF = tuple((j, j // 4, j // 4, (j % 2) * 2 + j // 4) for j in range(8))


NA_QROWS = 4
NA_TQ = NA_QROWS * GRID_W


def _na_kernel(q_ref, kp_ref, kc_ref, kn_ref, vp_ref, vc_ref, vn_ref, kx_ref, vx_ref, bias_ref, o_ref, *, rows):
    j = pl.program_id(1)
    kh = min(NA_WIN_H, rows)
    qi = lax.broadcasted_iota(jnp.int32, (NA_TQ, 1), 0)
    ki = lax.broadcasted_iota(jnp.int32, (1, NA_TQ), 1)
    wshift = GRID_W.bit_length() - 1
    r = j * NA_QROWS + (qi >> wshift)
    c = qi & (GRID_W - 1)
    r0 = jnp.clip(r - kh // 2, 0, rows - kh)
    c0 = jnp.clip(c - NA_WIN_W // 2, 0, GRID_W - NA_WIN_W)
    kcol = ki & (GRID_W - 1)
    col_ok = (kcol >= c0) & (kcol < c0 + NA_WIN_W)
    masks = []
    for d in (-1, 0, 1):
        kr = (j + d) * NA_QROWS + (ki >> wshift)
        masks.append(col_ok & (kr >= r0) & (kr < r0 + kh))
    lo_half = lax.broadcasted_iota(jnp.int32, (1, LANES), 1) < 64
    for o in range(2):
        k_loc = [kp_ref[0, :, o * LANES:(o + 1) * LANES], kc_ref[0, :, o * LANES:(o + 1) * LANES],
                 kn_ref[0, :, o * LANES:(o + 1) * LANES]]
        v_all = jnp.concatenate([vp_ref[0, :, o * LANES:(o + 1) * LANES], vc_ref[0, :, o * LANES:(o + 1) * LANES],
                                 vn_ref[0, :, o * LANES:(o + 1) * LANES], vx_ref[0, :, o * LANES:(o + 1) * LANES]],
                                axis=0)
        kx = kx_ref[0, :, o * LANES:(o + 1) * LANES]
        res = []
        for h in (2 * o, 2 * o + 1):
            q = q_ref[0, :, h * LANES:(h + 1) * LANES]
            parts = [jnp.where(masks[d], _dot_nt(q, k_loc[d]) + bias_ref[h, d], NEG_BIG) for d in range(3)]
            parts.append(_dot_nt(q, kx))
            s = jnp.concatenate(parts, axis=1)
            m = jnp.max(s, axis=-1, keepdims=True)
            p = jnp.exp2(s - m)
            l = jnp.sum(p, axis=-1, keepdims=True)
            res.append(_dot(p.astype(BF16), v_all) / l)
        o_ref[0, :, o * LANES:(o + 1) * LANES] = jnp.where(lo_half, res[0], res[1])


def na_attention(q, k, v, kx, vx, bias, rows):
    b, s, _ = q.shape
    nblk = rows // NA_QROWS
    nx = kx.shape[1]
    kern = functools.partial(_na_kernel, rows=rows)
    prev = lambda i, j: (i, jnp.maximum(j - 1, 0), 0)
    cur = lambda i, j: (i, j, 0)
    nxt = lambda i, j: (i, jnp.minimum(j + 1, nblk - 1), 0)
    kv_spec = lambda f: pl.BlockSpec((1, NA_TQ, 2 * LANES), f)
    return pl.pallas_call(
        kern,
        grid=(b, nblk),
        in_specs=[pl.BlockSpec((1, NA_TQ, 4 * LANES), cur),
                  kv_spec(prev), kv_spec(cur), kv_spec(nxt),
                  kv_spec(prev), kv_spec(cur), kv_spec(nxt),
                  pl.BlockSpec((1, nx, 2 * LANES), lambda i, j: (i, 0, 0)),
                  pl.BlockSpec((1, nx, 2 * LANES), lambda i, j: (i, 0, 0)),
                  pl.BlockSpec(bias.shape, lambda i, j: (0, 0, 0, 0))],
        out_specs=pl.BlockSpec((1, NA_TQ, 2 * LANES), cur),
        out_shape=jax.ShapeDtypeStruct((b, s, 2 * LANES), F32),
        compiler_params=_cparams(("parallel", "arbitrary")),
        name="na_attention",
    )(q, k, k, k, v, v, v, kx, vx, bias)


ROUTE_T = 256


def _outproj_kernel(x_ref, ona_ref, omla_ref, od_ref, ogqa_ref, mod_ref, gsub_ref, gffn_ref, lam_ref, mat_ref,
                    wout_ref, wr_ref, xo_ref, h_ref, lg_ref, *, lam_init):
    lam = lam_ref[0, 0]
    m64 = mat_ref[M_SEG64]
    pieces = [ona_ref[0].astype(BF16), omla_ref[0].astype(BF16)]
    dsl = []
    for i in range(2):
        d = od_ref[0, :, i * LANES:(i + 1) * LANES] - lam * od_ref[0, :, (2 + i) * LANES:(3 + i) * LANES]
        d = _seg_norm(d, m64, 1.0 / 64, gsub_ref[...]) * (1.0 - lam_init)
        dsl.append(d.astype(BF16))
    pieces += dsl + [ogqa_ref[0].astype(BF16)]
    o = jnp.concatenate(pieces, axis=1)
    y = _dot(o, wout_ref[...])
    x = x_ref[0] + mod_ref[0, 2:3, :] * y
    xo_ref[0] = x
    ms = jnp.mean(x * x, axis=-1, keepdims=True)
    h = x * lax.rsqrt(ms + NORM_EPS) * gffn_ref[...]
    h = h * (1.0 + mod_ref[0, 4:5, :]) + mod_ref[0, 3:4, :]
    h_ref[0] = h.astype(BF16)
    lg = _dot3_nt(wr_ref[...], h)
    for t in range(lg.shape[1] // ROUTE_T):
        lg_ref[0, t] = lg[:, t * ROUTE_T:(t + 1) * ROUTE_T]


def outproj_mod_router(x, o_na, o_mla, o_diff, o_gqa, mod, g_sub_t, g_ffn, lam, mats, w_out, w_router_t, lam_init):
    b, n, d = x.shape
    tm = min(512, n)
    nt = tm // ROUTE_T
    kern = functools.partial(_outproj_kernel, lam_init=lam_init)
    tok = lambda w: pl.BlockSpec((1, tm, w), lambda i, j: (i, j, 0))
    c2 = lambda i, j: (0, 0)
    return pl.pallas_call(
        kern,
        grid=(b, n // tm),
        in_specs=[tok(d), tok(256), tok(256), tok(512), tok(256),
                  pl.BlockSpec((1, 8, d), lambda i, j: (i, 0, 0)),
                  pl.BlockSpec((1, LANES), c2), pl.BlockSpec((1, d), c2),
                  pl.BlockSpec(memory_space=pltpu.SMEM),
                  pl.BlockSpec(mats.shape, lambda i, j: (0, 0, 0)),
                  pl.BlockSpec((d, d), c2), pl.BlockSpec((N_EXPERTS, d), c2)],
        out_specs=[tok(d), tok(d),
                   pl.BlockSpec((1, nt, N_EXPERTS, ROUTE_T), lambda i, j: (i, j, 0, 0))],
        out_shape=[jax.ShapeDtypeStruct((b, n, d), F32), jax.ShapeDtypeStruct((b, n, d), BF16),
                   jax.ShapeDtypeStruct((b, n // ROUTE_T, N_EXPERTS, ROUTE_T), F32)],
        compiler_params=_cparams(("parallel", "arbitrary")),
        name="outproj_mod_router",
    )(x, o_na, o_mla, o_diff, o_gqa, mod, g_sub_t, g_ffn.reshape(1, d), lam, mats, w_out, w_router_t)


def _route_kernel(lg_ref, tri_ref, aff_ref, pos_ref, off_ref, *, cap):
    nb = lg_ref.shape[1]
    lg = lg_ref[0]
    mx = jnp.max(lg, axis=1, keepdims=True)
    ex = jnp.exp(lg - mx)
    aff = ex / jnp.sum(ex, axis=1, keepdims=True)
    aff_ref[0] = aff
    bits = lax.bitcast_convert_type(aff, jnp.int32)

    def count_ge(t):
        hit = jnp.where(bits >= t[None], 1.0, 0.0)
        return jnp.sum(jnp.sum(hit, axis=0), axis=1, keepdims=True)

    def bis(i, t):
        cand = t | (jnp.int32(1) << (30 - i))
        return jnp.where(count_ge(cand) >= float(cap), cand, t)

    thr = lax.fori_loop(0, 31, bis, jnp.zeros((N_EXPERTS, 1), jnp.int32))
    need = float(cap) - count_ge(thr + 1)
    tri = tri_ref[...]

    def blk(jb, carry):
        c_eq, c_pos = carry
        bb = lax.bitcast_convert_type(aff_ref[0, jb], jnp.int32)
        gt = bb > thr
        eq = bb == thr
        eq_before = _dot(jnp.where(eq, 1.0, 0.0).astype(BF16), tri) + c_eq
        sel = gt | (eq & (eq_before < need))
        sel_f = jnp.where(sel, 1.0, 0.0)
        before = _dot(sel_f.astype(BF16), tri) + c_pos
        pos_ref[0, jb] = jnp.where(sel, before, -1.0).astype(jnp.int32)
        off_ref[0, jb] = jnp.broadcast_to(c_pos, (N_EXPERTS, LANES)).astype(jnp.int32)
        c_eq = c_eq + jnp.sum(jnp.where(eq, 1.0, 0.0), axis=1, keepdims=True)
        c_pos = c_pos + jnp.sum(sel_f, axis=1, keepdims=True)
        return c_eq, c_pos

    zero = jnp.zeros((N_EXPERTS, 1), F32)
    lax.fori_loop(0, nb, blk, (zero, zero))


def route(logits, tri, cap):
    b, nb, e, t = logits.shape
    kern = functools.partial(_route_kernel, cap=cap)
    spec = pl.BlockSpec((1, nb, e, t), lambda i: (i, 0, 0, 0))
    return pl.pallas_call(
        kern,
        grid=(b,),
        in_specs=[spec, pl.BlockSpec(tri.shape, lambda i: (0, 0))],
        out_specs=[spec, spec, pl.BlockSpec((1, nb, e, LANES), lambda i: (i, 0, 0, 0))],
        out_shape=[jax.ShapeDtypeStruct((b, nb, e, t), F32), jax.ShapeDtypeStruct((b, nb, e, t), jnp.int32),
                   jax.ShapeDtypeStruct((b, nb, e, LANES), jnp.int32)],
        compiler_params=_cparams(("arbitrary",)),
        name="route",
    )(logits, tri)


def _slot_tiles(off_ref, idx, st, n_tiles):
    a = off_ref[idx] // st
    return [(jnp.minimum(a + w, n_tiles - 1), (a + w) * st) for w in range(2)]


def _gather_kernel(off_ref, h_ref, pos_ref, aff_ref, xe_ref, gs_ref, acc_ref, gacc_ref, *, st, n_tiles, nsub, nb):
    bi, e, ch = pl.program_id(0), pl.program_id(1), pl.program_id(2)

    @pl.when(ch == 0)
    def _():
        acc_ref[...] = jnp.zeros(acc_ref.shape, F32)
        gacc_ref[...] = jnp.zeros(gacc_ref.shape, F32)

    slot = lax.broadcasted_iota(jnp.int32, (st, ROUTE_T), 0)
    for jj in range(nsub):
        idx = (bi * N_EXPERTS + e) * (nb + 1) + ch * nsub + jj
        prow = pos_ref[0, jj, pl.ds(e, 1), :]
        arow = aff_ref[0, jj, pl.ds(e, 1), :]
        hs = h_ref[0, jj * ROUTE_T:(jj + 1) * ROUTE_T, :]
        for tile, lo in _slot_tiles(off_ref, idx, st, n_tiles):
            hit = slot == (prow - lo)
            base = pl.multiple_of(tile * st, st)
            acc_ref[pl.ds(base, st), :] += _dot(jnp.where(hit, 1.0, 0.0).astype(BF16), hs)
            gacc_ref[pl.ds(base, st), :] += jnp.sum(jnp.where(hit, arow, 0.0), axis=1, keepdims=True)

    @pl.when(ch == pl.num_programs(2) - 1)
    def _():
        xe_ref[0, 0] = acc_ref[...].astype(BF16)
        gs_ref[0, 0] = gacc_ref[...]


def moe_gather(offs, h, pos, aff, cap):
    b, n, d = h.shape
    nb = n // ROUTE_T
    st = min(ROUTE_T, cap)
    n_tiles = cap // st
    chunk = min(2048, n)
    nsub = chunk // ROUTE_T
    kern = functools.partial(_gather_kernel, st=st, n_tiles=n_tiles, nsub=nsub, nb=nb)
    rspec = pl.BlockSpec((1, nsub, N_EXPERTS, ROUTE_T), lambda i, e, c, off: (i, c, 0, 0))
    return pl.pallas_call(
        kern,
        grid_spec=pltpu.PrefetchScalarGridSpec(
            num_scalar_prefetch=1,
            grid=(b, N_EXPERTS, n // chunk),
            in_specs=[pl.BlockSpec((1, chunk, d), lambda i, e, c, off: (i, c, 0)), rspec, rspec],
            out_specs=[pl.BlockSpec((1, 1, cap, d), lambda i, e, c, off: (i, e, 0, 0)),
                       pl.BlockSpec((1, 1, cap, 1), lambda i, e, c, off: (i, e, 0, 0))],
            scratch_shapes=[pltpu.VMEM((cap, d), F32), pltpu.VMEM((cap, 1), F32)]),
        out_shape=[jax.ShapeDtypeStruct((b, N_EXPERTS, cap, d), BF16),
                   jax.ShapeDtypeStruct((b, N_EXPERTS, cap, 1), F32)],
        compiler_params=_cparams(("parallel", "parallel", "arbitrary")),
        name="moe_gather",
    )(offs, h, pos, aff)


FF_TILE = 256


def _ffn_kernel(x_ref, g_ref, wg_ref, wu_ref, wd_ref, y_ref, acc_ref):
    f = pl.program_id(2)

    @pl.when(f == 0)
    def _():
        acc_ref[...] = jnp.zeros(acc_ref.shape, F32)

    x = x_ref[0, 0]
    gate = _dot(x, wg_ref[0, 0].astype(BF16))
    up = _dot(x, wu_ref[0, 0].astype(BF16))
    hmid = (_silu(gate) * up).astype(BF16)
    acc_ref[...] += _dot(hmid, wd_ref[0, 0].astype(BF16))

    @pl.when(f == pl.num_programs(2) - 1)
    def _():
        y_ref[0, 0] = (acc_ref[...] * g_ref[0, 0]).astype(BF16)


def moe_ffn(xe, gs, w_gate, w_up, w_down, layer):
    b, e, cap, d = xe.shape
    dff = w_gate.shape[3]
    return pl.pallas_call(
        _ffn_kernel,
        grid=(e, b, dff // FF_TILE),
        in_specs=[pl.BlockSpec((1, 1, cap, d), lambda ei, bi, f: (bi, ei, 0, 0)),
                  pl.BlockSpec((1, 1, cap, 1), lambda ei, bi, f: (bi, ei, 0, 0)),
                  pl.BlockSpec((1, 1, d, FF_TILE), lambda ei, bi, f: (layer, ei, 0, f)),
                  pl.BlockSpec((1, 1, d, FF_TILE), lambda ei, bi, f: (layer, ei, 0, f)),
                  pl.BlockSpec((1, 1, FF_TILE, d), lambda ei, bi, f: (layer, ei, f, 0))],
        out_specs=pl.BlockSpec((1, 1, cap, d), lambda ei, bi, f: (bi, ei, 0, 0)),
        out_shape=jax.ShapeDtypeStruct((b, e, cap, d), BF16),
        scratch_shapes=[pltpu.VMEM((cap, d), F32)],
        compiler_params=_cparams(("parallel", "parallel", "arbitrary")),
        name="moe_ffn",
    )(xe, gs, w_gate, w_up, w_down)


def _combine_kernel(off_ref, x_ref, ye_ref, pos_ref, mod_ref, o_ref, *, st, n_tiles, nsub, nb):
    bi, ch, e = pl.program_id(0), pl.program_id(1), pl.program_id(2)

    @pl.when(e == 0)
    def _():
        o_ref[...] = jnp.zeros(o_ref.shape, F32)

    slot = lax.broadcasted_iota(jnp.int32, (st, ROUTE_T), 0)
    for jj in range(nsub):
        idx = (bi * N_EXPERTS + e) * (nb + 1) + ch * nsub + jj
        prow = pos_ref[0, jj, pl.ds(e, 1), :]
        tot = jnp.zeros((ROUTE_T, o_ref.shape[2]), F32)
        for tile, lo in _slot_tiles(off_ref, idx, st, n_tiles):
            hit = slot == (prow - lo)
            base = pl.multiple_of(tile * st, st)
            oh = jnp.where(hit, 1.0, 0.0).astype(BF16)
            tot = tot + lax.dot_general(oh, ye_ref[0, 0, pl.ds(base, st), :], (((0,), (0,)), ((), ())),
                                        preferred_element_type=F32)
        o_ref[0, jj * ROUTE_T:(jj + 1) * ROUTE_T, :] += tot

    @pl.when(e == pl.num_programs(2) - 1)
    def _():
        o_ref[0] = x_ref[0] + mod_ref[0, 5:6, :] * o_ref[0]


def moe_combine(offs, x, ye, pos, mod, cap, slot0):
    b, n, d = x.shape
    nb = n // ROUTE_T
    st = min(ROUTE_T, cap)
    n_tiles = cap // st
    chunk = min(2048, n)
    nsub = chunk // ROUTE_T
    assert slot0 % cap == 0
    kern = functools.partial(_combine_kernel, st=st, n_tiles=n_tiles, nsub=nsub, nb=nb)
    return pl.pallas_call(
        kern,
        grid_spec=pltpu.PrefetchScalarGridSpec(
            num_scalar_prefetch=1,
            grid=(b, n // chunk, N_EXPERTS),
            in_specs=[pl.BlockSpec((1, chunk, d), lambda i, c, e, off: (i, c, 0)),
                      pl.BlockSpec((1, 1, cap, d), lambda i, c, e, off: (i, e, slot0 // cap, 0)),
                      pl.BlockSpec((1, nsub, N_EXPERTS, ROUTE_T), lambda i, c, e, off: (i, c, 0, 0)),
                      pl.BlockSpec((1, 8, d), lambda i, c, e, off: (i, 0, 0))],
            out_specs=pl.BlockSpec((1, chunk, d), lambda i, c, e, off: (i, c, 0))),
        out_shape=jax.ShapeDtypeStruct((b, n, d), F32),
        compiler_params=_cparams(("parallel", "parallel", "arbitrary")),
        name="moe_combine",
    )(offs, x, ye, pos, mod)


def expert_choice_ffn(sets, tri, w_gate, w_up, w_down, layer):
    routed = []
    for x, h, logits, mod in sets:
        b, n, _ = x.shape
        cap = EC_CAPACITY * n // N_EXPERTS
        aff, pos, off = route(logits, tri, cap)
        first = jnp.transpose(off[..., 0], (0, 2, 1))
        offs = jnp.concatenate([first, jnp.full((b, N_EXPERTS, 1), cap, jnp.int32)], axis=2).reshape(-1)
        xe, gs = moe_gather(offs, h, pos, aff, cap)
        routed.append((offs, pos, cap, xe, gs))
    xe_all = jnp.concatenate([r[3] for r in routed], axis=2) if len(routed) > 1 else routed[0][3]
    gs_all = jnp.concatenate([r[4] for r in routed], axis=2) if len(routed) > 1 else routed[0][4]
    ye = moe_ffn(xe_all, gs_all, w_gate, w_up, w_down, layer)
    outs, slot0 = [], 0
    for (x, _, _, mod), (offs, pos, cap, _, _) in zip(sets, routed):
        outs.append(moe_combine(offs, x, ye, pos, mod, cap, slot0))
        slot0 += cap
    return outs


def _rope_tables(n_rows_grid):
    s = n_rows_grid * GRID_W
    t = np.arange(s)
    row, col = (t // GRID_W).astype(np.float64), (t % GRID_W).astype(np.float64)

    def unit(n):
        half = n // 2
        inv = ROPE_THETA ** (-np.arange(0, n, 2, dtype=np.float64) / n)
        out = []
        for pos in (row, col):
            ang = pos[:, None] * inv[None, :]
            c, sn = np.cos(ang), np.sin(ang)
            z = np.zeros_like(sn)
            out.append((np.concatenate([c, c], 1), np.concatenate([-sn, z], 1), np.concatenate([z, sn], 1)))
        return [np.concatenate([out[0][i], out[1][i]], 1) for i in range(3)]

    ident = lambda w: (np.ones((s, w), np.float32), np.zeros((s, w), np.float32), np.zeros((s, w), np.float32))
    u16 = unit(16)
    u32 = unit(32)
    tabs = []
    idt = ident(64)
    idt32 = ident(32)
    for i in range(3):
        tabs.append(np.concatenate([idt[i], u16[i], idt32[i]], 1))
    for i in range(3):
        tabs.append(np.concatenate([u16[i]] * 4, 1))
    for i in range(3):
        tabs.append(np.concatenate([u32[i]] * 2, 1))
    return np.stack(tabs).astype(np.float32)


def _identity_tables(n):
    one, zero = np.ones((n, LANES), np.float32), np.zeros((n, LANES), np.float32)
    return np.stack([one, zero, zero] * 3)


def _seg_matrices():
    i = np.arange(LANES)
    ones = np.ones((LANES, LANES), np.float32)
    m64 = (i[:, None] // 64 == i[None, :] // 64).astype(np.float32)
    m32 = (i[:, None] // 32 == i[None, :] // 32).astype(np.float32)
    return np.stack([ones, m64, m32])


def _na_bias(rpb):
    c = np.arange(GRID_W)
    rl = np.arange(NA_QROWS)
    dc = np.clip(c[None, :] - c[:, None] + NA_WIN_W - 1, 0, 2 * NA_WIN_W - 2)
    d = np.array([-1, 0, 1])
    dr = np.clip(NA_QROWS * d[:, None, None] + rl[None, None, :] - rl[None, :, None] + NA_WIN_H - 1,
                 0, 2 * NA_WIN_H - 2)
    cols = jnp.take(rpb.astype(F32) * LOG2E, jnp.asarray(dc), axis=2)
    full = jnp.take(cols, jnp.asarray(dr), axis=1)
    return jnp.transpose(full, (0, 1, 2, 4, 3, 5)).reshape(rpb.shape[0], 3, NA_TQ, NA_TQ)


def _pad_cols(w, width):
    return jnp.pad(w, ((0, 0), (0, width - w.shape[1])))


def _layer_params(l, w_in, g_na_q, g_na_k, na_rpb, g_mla_cq, w_mla_uq, g_mla_q, g_mla_ckv, w_mla_ukv,
                  g_mla_k_nope, g_mla_k_rope, g_diff_q, g_diff_k, g_diff_sub, g_gqa_q, g_gqa_k, w_out):
    wi = w_in[l]
    (naq, nak, nav, cq, ckv, kr, dq, dk, dv, gq, gk, gv) = jnp.split(
        wi, np.cumsum([256, 256, 256, 256, 128, 32, 256, 256, 256, 256, 128])[:], axis=1)
    gq4 = gq.reshape(-1, 4, 64)[:, jnp.array([0, 2, 1, 3])].reshape(-1, 256)
    zeros = lambda w: jnp.zeros((wi.shape[0], w), wi.dtype)
    w_in_r = jnp.concatenate([naq, nak, nav, cq, ckv, dq, dk, dv, gq4, gk, gv, zeros(64), kr, zeros(32)],
                             axis=1).astype(BF16)
    uq = w_mla_uq[l].reshape(MLA_Q_RANK, 4, 96)
    wuq = jnp.pad(uq, ((0, 0), (0, 0), (0, 32))).reshape(MLA_Q_RANK, 512).astype(BF16)
    ukv = w_mla_ukv[l].reshape(MLA_KV_RANK, 4, 128)
    wuk = jnp.pad(ukv[:, :, :64], ((0, 0), (0, 0), (0, 64))).reshape(MLA_KV_RANK, 512).astype(BF16)
    wuv = ukv[:, :, 64:].reshape(MLA_KV_RANK, 256).astype(BF16)
    row = lambda v: jnp.pad(v, (0, 512 - v.shape[0]))
    z32, z64 = jnp.zeros((32,), F32), jnp.zeros((64,), F32)
    gains = jnp.stack([
        row(jnp.tile(g_na_q[l], 4)), row(jnp.tile(g_na_k[l], 4)), row(g_mla_cq[l]),
        row(jnp.tile(jnp.concatenate([g_mla_q[l], z32]), 4)), row(g_mla_ckv[l]),
        row(jnp.tile(jnp.concatenate([g_mla_k_nope[l], z64]), 4)),
        row(jnp.concatenate([z64, g_mla_k_rope[l], z32])),
        row(jnp.tile(g_diff_q[l].reshape(-1), 4)), row(jnp.tile(g_diff_k[l].reshape(-1), 4)),
        row(jnp.tile(g_gqa_q[l], 4)), row(jnp.tile(g_gqa_k[l], 2))] + [jnp.zeros((512,), F32)] * 5)
    wo = w_out[l]
    wo_g = wo[768:].reshape(4, 64, -1)[jnp.array([0, 2, 1, 3])].reshape(256, -1)
    w_out_r = jnp.concatenate([wo[:768], wo_g], axis=0).astype(BF16)
    return dict(w_in=w_in_r, wuq=wuq, wuk=wuk, wuv=wuv, gains=gains, w_out=w_out_r,
                bias=_na_bias(na_rpb[l]), g_sub=jnp.tile(g_diff_sub[l], 2).reshape(1, LANES))


def kernel(x, c, ctx, c_ctx, w_mod, b_mod, g_attn, g_ffn, w_in, g_na_q, g_na_k, na_rpb, g_mla_cq, w_mla_uq, g_mla_q, g_mla_ckv, w_mla_ukv, g_mla_k_nope, g_mla_k_rope, g_diff_q, g_diff_k, diff_lambda, g_diff_sub, g_gqa_q, g_gqa_k, w_out, w_router, w_gate, w_up, w_down):
    b, s, d = x.shape
    n_ctx = ctx.shape[1]
    rows = s // GRID_W
    tabs = jnp.asarray(_rope_tables(rows))
    tabs_ctx = jnp.asarray(_identity_tables(n_ctx))
    mats = jnp.asarray(_seg_matrices()).astype(BF16)
    tri = jnp.asarray(np.triu(np.ones((ROUTE_T, ROUTE_T), np.float32), 1)).astype(BF16)
    c_rows = jnp.concatenate([c, c_ctx[None], jnp.zeros((8 - b - 1, d), F32)], axis=0)
    xc = ctx
    for l in range(DEPTH):
        need_ctx = l < DEPTH - 1
        lam_init = 0.8 - 0.6 * math.exp(-0.3 * l)
        lp = diff_lambda[l].astype(F32)
        lam = (jnp.exp(jnp.sum(lp[0] * lp[1])) - jnp.exp(jnp.sum(lp[2] * lp[3])) + lam_init).reshape(1, 1)
        prm = _layer_params(l, w_in, g_na_q, g_na_k, na_rpb, g_mla_cq, w_mla_uq, g_mla_q, g_mla_ckv, w_mla_ukv,
                            g_mla_k_nope, g_mla_k_rope, g_diff_q, g_diff_k, g_diff_sub, g_gqa_q, g_gqa_k, w_out)
        modv = mod_vectors(c_rows, w_mod[l], b_mod[l]).reshape(8, N_MOD, d)
        mod = jnp.pad(modv[:b], ((0, 0), (0, 2), (0, 0)))
        mod_c = jnp.broadcast_to(jnp.pad(modv[b:b + 1], ((0, 0), (0, 2), (0, 0))), (b, 8, d))
        w_router_t = w_router[l].T

        def mix_inputs(xin, m, tb):
            p = ln_mod_proj(xin, g_attn[l], m[:, 0:1], m[:, 1:2], prm["w_in"])
            return prep(p, tb, prm["gains"], mats, prm["wuq"], prm["wuk"], prm["wuv"])

        (naq, nak, nav, mq, mk, mv, dq, dk, dv, gq, gk, gv, stat) = mix_inputs(x, mod, tabs)
        (naq_c, nak_c, nav_c, mq_c, mk_c, mv_c, dq_c, dk_c, dv_c, gq_c, gk_c, gv_c, stat_c) = mix_inputs(
            xc, mod_c, tabs_ctx)
        cat = lambda a, bb: jnp.concatenate([a, bb], axis=1)
        n2 = jnp.max(stat, axis=(1, 3))
        n2c = jnp.max(stat_c, axis=(1, 3))
        q2 = lambda name: n2[:, STAT_ROWS.index(name)]
        k2 = lambda name: jnp.maximum(n2[:, STAT_ROWS.index(name)], n2c[:, STAT_ROWS.index(name)])

        ext = values_transposed
        o_na = na_attention(naq, nak, nav, nak_c, nav_c, prm["bias"], rows)
        o_mla = attention(q2("mq"), k2("mk"), mq, cat(mk_c, mk), ext(cat(mv_c, mv)), HEADS_MLA)
        o_diff = attention(q2("dq"), k2("dk"), dq, cat(dk_c, dk), ext(cat(dv_c, dv)), HEADS_DIFF)
        o_gqa = attention(q2("gq"), k2("gk"), gq, cat(gk_c, gk), ext(cat(gv_c, gv)), HEADS_GQA)
        x_mid, h2, logits = outproj_mod_router(x, o_na, o_mla, o_diff, o_gqa, mod, prm["g_sub"], g_ffn[l], lam,
                                               mats, prm["w_out"], w_router_t, lam_init)
        if need_ctx:
            oc_na = flash_attention(naq_c, nak_c, ext(nav_c), HEADS_PAIRED)
            oc_mla = flash_attention(mq_c, mk_c, ext(mv_c), HEADS_MLA)
            oc_diff = flash_attention(dq_c, dk_c, ext(dv_c), HEADS_DIFF)
            oc_gqa = flash_attention(gq_c, gk_c, ext(gv_c), HEADS_GQA)
            xc_mid, hc2, logits_c = outproj_mod_router(xc, oc_na, oc_mla, oc_diff, oc_gqa, mod_c, prm["g_sub"],
                                                       g_ffn[l], lam, mats, prm["w_out"], w_router_t, lam_init)
            x, xc = expert_choice_ffn([(x_mid, h2, logits, mod), (xc_mid, hc2, logits_c, mod_c)],
                                      tri, w_gate, w_up, w_down, l)
        else:
            (x,) = expert_choice_ffn([(x_mid, h2, logits, mod)], tri, w_gate, w_up, w_down, l)
    return x
```

```python
import functools
import math

import numpy as np
import jax
import jax.numpy as jnp
from jax import lax
from jax.experimental import pallas as pl
from jax.experimental.pallas import tpu as pltpu

D_MODEL = 1024
GRID_W = 64
HEAD_DIM = 64
N_HEADS = 4
NA_WIN_H = 8
NA_WIN_W = 16
MLA_Q_RANK = 256
MLA_KV_RANK = 128
MLA_NOPE_DIM = 64
MLA_ROPE_DIM = 32
MLA_V_DIM = 64
DIFF_QK_DIM = 32
N_EXPERTS = 16
EC_CAPACITY = 2
D_FF = 2816
ROPE_THETA = 10000.0
NORM_EPS = 1e-6
N_MOD = 6
DEPTH = 2

LANES = 128
P_COLS = 2560
VMEM_LIMIT = 56 * 1024 * 1024
NEG_BIG = -1e30
LOG2E = math.log2(math.e)

F32 = jnp.float32
BF16 = jnp.bfloat16


def _cparams(sem):
    return pltpu.CompilerParams(dimension_semantics=sem, vmem_limit_bytes=VMEM_LIMIT)


def _split(a):
    hi = a.astype(BF16)
    lo = (a - hi.astype(F32)).astype(BF16)
    return hi, lo


def _dot(a, b):
    return jnp.dot(a, b, preferred_element_type=F32)


def _dot_nt(a, b):
    return lax.dot_general(a, b, (((1,), (1,)), ((), ())), preferred_element_type=F32)


def _dot3(a, b):
    ah, al = _split(a)
    bh, bl = _split(b)
    return _dot(ah, bh) + _dot(ah, bl) + _dot(al, bh)


def _dot3_nt(a, b):
    ah, al = _split(a)
    bh, bl = _split(b)
    return _dot_nt(ah, bh) + _dot_nt(ah, bl) + _dot_nt(al, bh)


def _silu(v):
    return v * jax.nn.sigmoid(v)


def _mod_kernel(c_ref, w_ref, b_ref, o_ref):
    o_ref[...] = _dot3(_silu(c_ref[...]), w_ref[...]) + b_ref[...]


def mod_vectors(c_rows, w_mod, b_mod):
    m, d = c_rows.shape
    n = w_mod.shape[1]
    tn = 1536
    return pl.pallas_call(
        _mod_kernel,
        grid=(n // tn,),
        in_specs=[pl.BlockSpec((m, d), lambda j: (0, 0)),
                  pl.BlockSpec((d, tn), lambda j: (0, j)),
                  pl.BlockSpec((1, tn), lambda j: (0, j))],
        out_specs=pl.BlockSpec((m, tn), lambda j: (0, j)),
        out_shape=jax.ShapeDtypeStruct((m, n), F32),
        compiler_params=_cparams(("arbitrary",)),
        name="mod_vectors",
    )(c_rows, w_mod, b_mod.reshape(1, n))


def _ln_proj_kernel(x_ref, g_ref, sh_ref, sc_ref, w_ref, o_ref):
    x = x_ref[0]
    ms = jnp.mean(x * x, axis=-1, keepdims=True)
    y = x * lax.rsqrt(ms + NORM_EPS) * g_ref[...]
    h = y * (1.0 + sc_ref[0]) + sh_ref[0]
    o_ref[0] = _dot(h.astype(BF16), w_ref[...])


def ln_mod_proj(x, g, shift, scale, w):
    b, n, d = x.shape
    ncol = w.shape[1]
    tm = min(512, n)
    return pl.pallas_call(
        _ln_proj_kernel,
        grid=(b, n // tm),
        in_specs=[pl.BlockSpec((1, tm, d), lambda i, j: (i, j, 0)),
                  pl.BlockSpec((1, d), lambda i, j: (0, 0)),
                  pl.BlockSpec((1, 1, d), lambda i, j: (i, 0, 0)),
                  pl.BlockSpec((1, 1, d), lambda i, j: (i, 0, 0)),
                  pl.BlockSpec((d, ncol), lambda i, j: (0, 0))],
        out_specs=pl.BlockSpec((1, tm, ncol), lambda i, j: (i, j, 0)),
        out_shape=jax.ShapeDtypeStruct((b, n, ncol), F32),
        compiler_params=_cparams(("parallel", "arbitrary")),
        name="ln_mod_proj",
    )(x, g.reshape(1, d), shift, scale, w)


P_NAQ, P_NAK, P_NAV, P_CQ, P_CKV = 0, 256, 512, 768, 1024
P_DQ, P_DK, P_DV, P_GQ, P_GK, P_GV, P_KR = 1152, 1408, 1664, 1920, 2176, 2304, 2432
(G_NAQ, G_NAK, G_CQ, G_MQ, G_CKV, G_KN, G_KR, G_DQ, G_DK, G_GQ, G_GK) = range(11)
M_ONES, M_SEG64, M_SEG32 = 0, 1, 2
T_MLA, T_DIFF, T_GQA = 0, 3, 6


def _seg_norm(x, mat, inv_n, g):
    sq = x * x
    hi, lo = _split(sq)
    ms = (_dot(hi, mat) + _dot(lo, mat)) * inv_n
    return x * lax.rsqrt(ms + NORM_EPS) * g


def _rope(x, tab_ref, t0, half):
    c, s1, s2 = tab_ref[t0], tab_ref[t0 + 1], tab_ref[t0 + 2]
    return (x * c + pltpu.roll(x, LANES - half, axis=1) * s1
            + pltpu.roll(x, half, axis=1) * s2)


def _prep_kernel(p_ref, tab_ref, gain_ref, mat_ref, wuq_ref, wuk_ref, wuv_ref,
                 naq_ref, nak_ref, nav_ref, mq_ref, mk_ref, mv_ref,
                 dq_ref, dk_ref, dv_ref, gq_ref, gk_ref, gv_ref, st_ref):
    lane = lax.broadcasted_iota(jnp.int32, (1, LANES), 1)
    lo_half = lane < 64
    ones_m, m64, m32 = mat_ref[M_ONES], mat_ref[M_SEG64], mat_ref[M_SEG32]
    norm2 = {}

    def slab(off, i):
        return p_ref[0, :, off + i * LANES: off + (i + 1) * LANES]

    def track(name, xb, mat):
        xf = xb.astype(F32)
        hi, lo = _split(xf * xf)
        cur = jnp.max(_dot(hi, mat) + _dot(lo, mat), axis=0, keepdims=True)
        norm2[name] = jnp.maximum(norm2[name], cur) if name in norm2 else cur
        return xb

    def gain(row, i):
        return gain_ref[row:row + 1, i * LANES:(i + 1) * LANES]

    s_na = HEAD_DIM ** -0.5 * LOG2E
    for i in range(2):
        q = _seg_norm(slab(P_NAQ, i), m64, 1.0 / 64, gain(G_NAQ, i)) * s_na
        naq_ref[0, :, (2 * i) * LANES:(2 * i + 1) * LANES] = jnp.where(lo_half, q, 0.0).astype(BF16)
        naq_ref[0, :, (2 * i + 1) * LANES:(2 * i + 2) * LANES] = jnp.where(lo_half, 0.0, q).astype(BF16)
        k = _seg_norm(slab(P_NAK, i), m64, 1.0 / 64, gain(G_NAK, i))
        nak_ref[0, :, i * LANES:(i + 1) * LANES] = k.astype(BF16)
        nav_ref[0, :, i * LANES:(i + 1) * LANES] = slab(P_NAV, i).astype(BF16)

    cq = p_ref[0, :, P_CQ:P_CQ + MLA_Q_RANK]
    cq = cq * lax.rsqrt(jnp.mean(cq * cq, axis=-1, keepdims=True) + NORM_EPS) * gain_ref[G_CQ:G_CQ + 1, :MLA_Q_RANK]
    uq = _dot(cq.astype(BF16), wuq_ref[...])
    s_mla = (MLA_NOPE_DIM + MLA_ROPE_DIM) ** -0.5 * LOG2E
    ckv = p_ref[0, :, P_CKV:P_CKV + MLA_KV_RANK]
    ckv = ckv * lax.rsqrt(jnp.mean(ckv * ckv, axis=-1, keepdims=True) + NORM_EPS) * gain_ref[G_CKV:G_CKV + 1, :MLA_KV_RANK]
    ckv_b = ckv.astype(BF16)
    uk = _dot(ckv_b, wuk_ref[...])
    mv_ref[0] = _dot(ckv_b, wuv_ref[...]).astype(BF16)
    kr = _seg_norm(slab(P_KR, 0), ones_m, 1.0 / MLA_ROPE_DIM, gain(G_KR, 0))
    kr = _rope(kr, tab_ref, T_MLA, 8)
    for h in range(N_HEADS):
        q = _seg_norm(uq[:, h * LANES:(h + 1) * LANES], ones_m, 1.0 / (MLA_NOPE_DIM + MLA_ROPE_DIM), gain(G_MQ, h))
        q = _rope(q, tab_ref, T_MLA, 8) * s_mla
        mq_ref[0, :, h * LANES:(h + 1) * LANES] = track("mq", q.astype(BF16), ones_m)
        kn = _seg_norm(uk[:, h * LANES:(h + 1) * LANES], ones_m, 1.0 / MLA_NOPE_DIM, gain(G_KN, h))
        mk_ref[0, :, h * LANES:(h + 1) * LANES] = track("mk", (kn + kr).astype(BF16), ones_m)

    s_d = DIFF_QK_DIM ** -0.5 * LOG2E
    seg = lane >> 5
    for i in range(2):
        q = _seg_norm(slab(P_DQ, i), m32, 1.0 / 32, gain(G_DQ, i))
        q = _rope(q, tab_ref, T_DIFF, 8) * s_d
        track("dq", q.astype(BF16), m32)
        for j in range(4):
            dq_ref[0, :, (4 * i + j) * LANES:(4 * i + j + 1) * LANES] = jnp.where(seg == j, q, 0.0).astype(BF16)
        k = _seg_norm(slab(P_DK, i), m32, 1.0 / 32, gain(G_DK, i))
        dk_ref[0, :, i * LANES:(i + 1) * LANES] = track("dk", _rope(k, tab_ref, T_DIFF, 8).astype(BF16), m32)
        dv_ref[0, :, i * LANES:(i + 1) * LANES] = slab(P_DV, i).astype(BF16)

    s_g = HEAD_DIM ** -0.5 * LOG2E
    for i in range(2):
        q = _seg_norm(slab(P_GQ, i), m64, 1.0 / 64, gain(G_GQ, i))
        q = _rope(q, tab_ref, T_GQA, 16) * s_g
        track("gq", q.astype(BF16), m64)
        gq_ref[0, :, (2 * i) * LANES:(2 * i + 1) * LANES] = jnp.where(lo_half, q, 0.0).astype(BF16)
        gq_ref[0, :, (2 * i + 1) * LANES:(2 * i + 2) * LANES] = jnp.where(lo_half, 0.0, q).astype(BF16)
    k = _seg_norm(slab(P_GK, 0), m64, 1.0 / 64, gain(G_GK, 0))
    gk_ref[0] = track("gk", _rope(k, tab_ref, T_GQA, 16).astype(BF16), m64)
    gv_ref[0] = slab(P_GV, 0).astype(BF16)
    st_ref[0, 0] = jnp.concatenate([norm2[n] for n in STAT_ROWS] + [jnp.zeros((2, LANES), F32)], axis=0)


PREP_WIDTHS = (512, 256, 256, 512, 512, 256, 1024, 256, 256, 512, 128, 128)
STAT_ROWS = ("mq", "mk", "dq", "dk", "gq", "gk")


def prep(p, tabs, gains, mats, wuq, wuk, wuv):
    b, n, _ = p.shape
    tm = min(512, n)
    const2 = lambda i, j: (0, 0)
    stat_spec = pl.BlockSpec((1, 1, 8, LANES), lambda i, j: (i, j, 0, 0))
    stat_shape = jax.ShapeDtypeStruct((b, n // tm, 8, LANES), F32)
    return pl.pallas_call(
        _prep_kernel,
        grid=(b, n // tm),
        in_specs=[pl.BlockSpec((1, tm, P_COLS), lambda i, j: (i, j, 0)),
                  pl.BlockSpec((9, tm, LANES), lambda i, j: (0, j, 0)),
                  pl.BlockSpec(gains.shape, const2),
                  pl.BlockSpec(mats.shape, lambda i, j: (0, 0, 0)),
                  pl.BlockSpec(wuq.shape, const2),
                  pl.BlockSpec(wuk.shape, const2),
                  pl.BlockSpec(wuv.shape, const2)],
        out_specs=[pl.BlockSpec((1, tm, w), lambda i, j: (i, j, 0)) for w in PREP_WIDTHS] + [stat_spec],
        out_shape=[jax.ShapeDtypeStruct((b, n, w), BF16) for w in PREP_WIDTHS] + [stat_shape],
        compiler_params=_cparams(("parallel", "arbitrary")),
        name="prep",
    )(p, tabs, gains, mats, wuq, wuk, wuv)


VT_ROWS = LANES + 16


def _value_rows(heads, h):
    pair = [g for g in range(len(heads)) if heads[g][3] == heads[h][3]]
    return pair.index(h) * 64, pair.index(h) * 64 + 64
KEY_CHUNK = 256
FLASH_TQ = 512
FLASH_TK_MAX = 3584


def _flash_kernel(q_ref, k_ref, vt_ref, o_ref, m_ref, l_ref, acc_ref, *, heads, n_out):
    ki = pl.program_id(2)

    @pl.when(ki == 0)
    def _():
        m_ref[...] = jnp.full(m_ref.shape, NEG_BIG, F32)
        l_ref[...] = jnp.zeros(l_ref.shape, F32)
        acc_ref[...] = jnp.zeros(acc_ref.shape, F32)

    for o in range(n_out):
        pair = [h for h in range(len(heads)) if heads[h][3] == o]
        for pos, h in enumerate(pair):
            qs, ks, vs, _ = heads[h]
            q = q_ref[0, :, qs * LANES:(qs + 1) * LANES]
            k = k_ref[0, :, ks * LANES:(ks + 1) * LANES]
            st = _dot_nt(k, q)
            lo, hi = _value_rows(heads, h)
            m, l, acc = m_ref[h], l_ref[h], acc_ref[o, lo:hi, :]
            for c0 in range(0, st.shape[0], KEY_CHUNK):
                rows = slice(c0, min(c0 + KEY_CHUNK, st.shape[0]))
                sc = st[rows]
                m_new = jnp.maximum(m, jnp.max(sc, axis=0, keepdims=True))
                a = jnp.exp2(m - m_new)
                pt = jnp.exp2(sc - m_new).astype(BF16)
                r = _dot(vt_ref[0, vs, :, rows], pt)
                l = a * l + r[LANES:LANES + 1]
                acc = acc * a + r[lo:hi]
                m = m_new
            m_ref[h], l_ref[h] = m, l
            acc_ref[o, lo:hi, :] = acc

    @pl.when(ki == pl.num_programs(2) - 1)
    def _():
        for o in range(n_out):
            pair = [h for h in range(len(heads)) if heads[h][3] == o]
            out_t = jnp.concatenate([acc_ref[o, 0:64, :] / l_ref[pair[0]],
                                     acc_ref[o, 64:128, :] / l_ref[pair[1]]], axis=0)
            o_ref[0, :, o * LANES:(o + 1) * LANES] = out_t.T


def _pick_tk(n_keys):
    best = 256
    for t in range(256, FLASH_TK_MAX + 1, 256):
        if n_keys % t == 0:
            best = t
    return best


def values_transposed(v):
    b, n, w = v.shape
    vt = jnp.transpose(v.reshape(b, n, w // LANES, LANES), (0, 2, 3, 1))
    return jnp.concatenate([vt, jnp.ones((b, w // LANES, VT_ROWS - LANES, n), v.dtype)], axis=2)


def flash_attention(q, k, vt, heads):
    b, s, qw = q.shape
    nk = k.shape[1]
    n_out = max(h[3] for h in heads) + 1
    tq = min(FLASH_TQ, s)
    tk = _pick_tk(nk)
    kern = functools.partial(_flash_kernel, heads=heads, n_out=n_out)
    return pl.pallas_call(
        kern,
        grid=(b, s // tq, nk // tk),
        in_specs=[pl.BlockSpec((1, tq, qw), lambda i, j, t: (i, j, 0)),
                  pl.BlockSpec((1, tk, k.shape[2]), lambda i, j, t: (i, t, 0)),
                  pl.BlockSpec((1, vt.shape[1], VT_ROWS, tk), lambda i, j, t: (i, 0, 0, t))],
        out_specs=pl.BlockSpec((1, tq, n_out * LANES), lambda i, j, t: (i, j, 0)),
        out_shape=jax.ShapeDtypeStruct((b, s, n_out * LANES), F32),
        scratch_shapes=[pltpu.VMEM((len(heads), 1, tq), F32),
                        pltpu.VMEM((len(heads), 1, tq), F32),
                        pltpu.VMEM((n_out, LANES, tq), F32)],
        compiler_params=_cparams(("parallel", "parallel", "arbitrary")),
        name="flash_attention",
    )(q, k, vt)


STAB_SHIFT = 100.0
BOUND_LIMIT = 113.0
BOUND_SLACK = 1.01
BOUNDED_TQ = 1024


def _flash_bounded_kernel(stab_ref, q_ref, k_ref, vt_ref, o_ref, acc_ref, *, heads, n_out):
    ki = pl.program_id(2)

    @pl.when(ki == 0)
    def _():
        acc_ref[...] = jnp.zeros(acc_ref.shape, F32)

    stab = stab_ref[pl.program_id(0), 0]
    for h, (qs, ks, vs, _) in enumerate(heads):
        q = q_ref[0, :, qs * LANES:(qs + 1) * LANES]
        k = k_ref[0, :, ks * LANES:(ks + 1) * LANES]
        pt = jnp.exp2(_dot_nt(k, q) - stab).astype(BF16)
        acc_ref[h] += _dot(vt_ref[0, vs], pt)

    @pl.when(ki == pl.num_programs(2) - 1)
    def _():
        for o in range(n_out):
            pair = [h for h in range(len(heads)) if heads[h][3] == o]
            parts = []
            for h in pair:
                lo, hi = _value_rows(heads, h)
                parts.append(acc_ref[h, lo:hi, :] / acc_ref[h, LANES:LANES + 1, :])
            o_ref[0, :, o * LANES:(o + 1) * LANES] = jnp.concatenate(parts, axis=0).T


def flash_attention_bounded(stab, q, k, vt, heads):
    b, s, qw = q.shape
    nk = k.shape[1]
    n_out = max(h[3] for h in heads) + 1
    tq = min(BOUNDED_TQ, s)
    tk = _pick_tk(nk)
    kern = functools.partial(_flash_bounded_kernel, heads=heads, n_out=n_out)
    return pl.pallas_call(
        kern,
        grid=(b, s // tq, nk // tk),
        in_specs=[pl.BlockSpec(memory_space=pltpu.SMEM),
                  pl.BlockSpec((1, tq, qw), lambda i, j, t: (i, j, 0)),
                  pl.BlockSpec((1, tk, k.shape[2]), lambda i, j, t: (i, t, 0)),
                  pl.BlockSpec((1, vt.shape[1], VT_ROWS, tk), lambda i, j, t: (i, 0, 0, t))],
        out_specs=pl.BlockSpec((1, tq, n_out * LANES), lambda i, j, t: (i, j, 0)),
        out_shape=jax.ShapeDtypeStruct((b, s, n_out * LANES), F32),
        scratch_shapes=[pltpu.VMEM((len(heads), VT_ROWS, tq), F32)],
        compiler_params=_cparams(("parallel", "parallel", "arbitrary")),
        name="flash_attention_bounded",
    )(stab, q, k, vt)


def attention(q2, k2, q, k, vt, heads):
    bound = BOUND_SLACK * jnp.sqrt(q2 * k2)
    stab = (bound - STAB_SHIFT).reshape(-1, 1).astype(F32)
    return lax.cond(jnp.max(bound) <= BOUND_LIMIT,
                    lambda: flash_attention_bounded(stab, q, k, vt, heads),
                    lambda: flash_attention(q, k, vt, heads))


HEADS_PAIRED = tuple((h, h // 2, h // 2, h // 2) for h in range(4))
HEADS_MLA = tuple((h, h, h // 2, h // 2) for h in range(4))
HEADS_GQA = tuple((h, 0, 0, h // 2) for h in range(4))
HEADS_DIFF = tuple((j, j // 4, j // 4, (j % 2) * 2 + j // 4) for j in range(8))


NA_QROWS = 4
NA_TQ = NA_QROWS * GRID_W


def _na_kernel(q_ref, kp_ref, kc_ref, kn_ref, vp_ref, vc_ref, vn_ref, kx_ref, vx_ref, bias_ref, o_ref, *, rows):
    j = pl.program_id(1)
    kh = min(NA_WIN_H, rows)
    qi = lax.broadcasted_iota(jnp.int32, (NA_TQ, 1), 0)
    ki = lax.broadcasted_iota(jnp.int32, (1, NA_TQ), 1)
    wshift = GRID_W.bit_length() - 1
    r = j * NA_QROWS + (qi >> wshift)
    c = qi & (GRID_W - 1)
    r0 = jnp.clip(r - kh // 2, 0, rows - kh)
    c0 = jnp.clip(c - NA_WIN_W // 2, 0, GRID_W - NA_WIN_W)
    kcol = ki & (GRID_W - 1)
    col_ok = (kcol >= c0) & (kcol < c0 + NA_WIN_W)
    masks = []
    for d in (-1, 0, 1):
        kr = (j + d) * NA_QROWS + (ki >> wshift)
        masks.append(col_ok & (kr >= r0) & (kr < r0 + kh))
    lo_half = lax.broadcasted_iota(jnp.int32, (1, LANES), 1) < 64
    for o in range(2):
        k_loc = [kp_ref[0, :, o * LANES:(o + 1) * LANES], kc_ref[0, :, o * LANES:(o + 1) * LANES],
                 kn_ref[0, :, o * LANES:(o + 1) * LANES]]
        v_all = jnp.concatenate([vp_ref[0, :, o * LANES:(o + 1) * LANES], vc_ref[0, :, o * LANES:(o + 1) * LANES],
                                 vn_ref[0, :, o * LANES:(o + 1) * LANES], vx_ref[0, :, o * LANES:(o + 1) * LANES]],
                                axis=0)
        kx = kx_ref[0, :, o * LANES:(o + 1) * LANES]
        res = []
        for h in (2 * o, 2 * o + 1):
            q = q_ref[0, :, h * LANES:(h + 1) * LANES]
            parts = [jnp.where(masks[d], _dot_nt(q, k_loc[d]) + bias_ref[h, d], NEG_BIG) for d in range(3)]
            parts.append(_dot_nt(q, kx))
            s = jnp.concatenate(parts, axis=1)
            m = jnp.max(s, axis=-1, keepdims=True)
            p = jnp.exp2(s - m)
            l = jnp.sum(p, axis=-1, keepdims=True)
            res.append(_dot(p.astype(BF16), v_all) / l)
        o_ref[0, :, o * LANES:(o + 1) * LANES] = jnp.where(lo_half, res[0], res[1])


def na_attention(q, k, v, kx, vx, bias, rows):
    b, s, _ = q.shape
    nblk = rows // NA_QROWS
    nx = kx.shape[1]
    kern = functools.partial(_na_kernel, rows=rows)
    prev = lambda i, j: (i, jnp.maximum(j - 1, 0), 0)
    cur = lambda i, j: (i, j, 0)
    nxt = lambda i, j: (i, jnp.minimum(j + 1, nblk - 1), 0)
    kv_spec = lambda f: pl.BlockSpec((1, NA_TQ, 2 * LANES), f)
    return pl.pallas_call(
        kern,
        grid=(b, nblk),
        in_specs=[pl.BlockSpec((1, NA_TQ, 4 * LANES), cur),
                  kv_spec(prev), kv_spec(cur), kv_spec(nxt),
                  kv_spec(prev), kv_spec(cur), kv_spec(nxt),
                  pl.BlockSpec((1, nx, 2 * LANES), lambda i, j: (i, 0, 0)),
                  pl.BlockSpec((1, nx, 2 * LANES), lambda i, j: (i, 0, 0)),
                  pl.BlockSpec(bias.shape, lambda i, j: (0, 0, 0, 0))],
        out_specs=pl.BlockSpec((1, NA_TQ, 2 * LANES), cur),
        out_shape=jax.ShapeDtypeStruct((b, s, 2 * LANES), F32),
        compiler_params=_cparams(("parallel", "arbitrary")),
        name="na_attention",
    )(q, k, k, k, v, v, v, kx, vx, bias)


ROUTE_T = 256


def _outproj_kernel(x_ref, ona_ref, omla_ref, od_ref, ogqa_ref, mod_ref, gsub_ref, gffn_ref, lam_ref, mat_ref,
                    wout_ref, wr_ref, xo_ref, h_ref, lg_ref, *, lam_init):
    lam = lam_ref[0, 0]
    m64 = mat_ref[M_SEG64]
    pieces = [ona_ref[0].astype(BF16), omla_ref[0].astype(BF16)]
    dsl = []
    for i in range(2):
        d = od_ref[0, :, i * LANES:(i + 1) * LANES] - lam * od_ref[0, :, (2 + i) * LANES:(3 + i) * LANES]
        d = _seg_norm(d, m64, 1.0 / 64, gsub_ref[...]) * (1.0 - lam_init)
        dsl.append(d.astype(BF16))
    pieces += dsl + [ogqa_ref[0].astype(BF16)]
    o = jnp.concatenate(pieces, axis=1)
    y = _dot(o, wout_ref[...])
    x = x_ref[0] + mod_ref[0, 2:3, :] * y
    xo_ref[0] = x
    ms = jnp.mean(x * x, axis=-1, keepdims=True)
    h = x * lax.rsqrt(ms + NORM_EPS) * gffn_ref[...]
    h = h * (1.0 + mod_ref[0, 4:5, :]) + mod_ref[0, 3:4, :]
    h_ref[0] = h.astype(BF16)
    lg = _dot3_nt(wr_ref[...], h)
    for t in range(lg.shape[1] // ROUTE_T):
        lg_ref[0, t] = lg[:, t * ROUTE_T:(t + 1) * ROUTE_T]


def outproj_mod_router(x, o_na, o_mla, o_diff, o_gqa, mod, g_sub_t, g_ffn, lam, mats, w_out, w_router_t, lam_init):
    b, n, d = x.shape
    tm = min(512, n)
    nt = tm // ROUTE_T
    kern = functools.partial(_outproj_kernel, lam_init=lam_init)
    tok = lambda w: pl.BlockSpec((1, tm, w), lambda i, j: (i, j, 0))
    c2 = lambda i, j: (0, 0)
    return pl.pallas_call(
        kern,
        grid=(b, n // tm),
        in_specs=[tok(d), tok(256), tok(256), tok(512), tok(256),
                  pl.BlockSpec((1, 8, d), lambda i, j: (i, 0, 0)),
                  pl.BlockSpec((1, LANES), c2), pl.BlockSpec((1, d), c2),
                  pl.BlockSpec(memory_space=pltpu.SMEM),
                  pl.BlockSpec(mats.shape, lambda i, j: (0, 0, 0)),
                  pl.BlockSpec((d, d), c2), pl.BlockSpec((N_EXPERTS, d), c2)],
        out_specs=[tok(d), tok(d),
                   pl.BlockSpec((1, nt, N_EXPERTS, ROUTE_T), lambda i, j: (i, j, 0, 0))],
        out_shape=[jax.ShapeDtypeStruct((b, n, d), F32), jax.ShapeDtypeStruct((b, n, d), BF16),
                   jax.ShapeDtypeStruct((b, n // ROUTE_T, N_EXPERTS, ROUTE_T), F32)],
        compiler_params=_cparams(("parallel", "arbitrary")),
        name="outproj_mod_router",
    )(x, o_na, o_mla, o_diff, o_gqa, mod, g_sub_t, g_ffn.reshape(1, d), lam, mats, w_out, w_router_t)


def _route_kernel(lg_ref, tri_ref, aff_ref, pos_ref, off_ref, *, cap):
    nb = lg_ref.shape[1]
    lg = lg_ref[0]
    mx = jnp.max(lg, axis=1, keepdims=True)
    ex = jnp.exp(lg - mx)
    aff = ex / jnp.sum(ex, axis=1, keepdims=True)
    aff_ref[0] = aff
    bits = lax.bitcast_convert_type(aff, jnp.int32)

    def count_ge(t):
        hit = jnp.where(bits >= t[None], 1.0, 0.0)
        return jnp.sum(jnp.sum(hit, axis=0), axis=1, keepdims=True)

    def bis(i, t):
        cand = t | (jnp.int32(1) << (30 - i))
        return jnp.where(count_ge(cand) >= float(cap), cand, t)

    thr = lax.fori_loop(0, 31, bis, jnp.zeros((N_EXPERTS, 1), jnp.int32))
    need = float(cap) - count_ge(thr + 1)
    tri = tri_ref[...]

    def blk(jb, carry):
        c_eq, c_pos = carry
        bb = lax.bitcast_convert_type(aff_ref[0, jb], jnp.int32)
        gt = bb > thr
        eq = bb == thr
        eq_before = _dot(jnp.where(eq, 1.0, 0.0).astype(BF16), tri) + c_eq
        sel = gt | (eq & (eq_before < need))
        sel_f = jnp.where(sel, 1.0, 0.0)
        before = _dot(sel_f.astype(BF16), tri) + c_pos
        pos_ref[0, jb] = jnp.where(sel, before, -1.0).astype(jnp.int32)
        off_ref[0, jb] = jnp.broadcast_to(c_pos, (N_EXPERTS, LANES)).astype(jnp.int32)
        c_eq = c_eq + jnp.sum(jnp.where(eq, 1.0, 0.0), axis=1, keepdims=True)
        c_pos = c_pos + jnp.sum(sel_f, axis=1, keepdims=True)
        return c_eq, c_pos

    zero = jnp.zeros((N_EXPERTS, 1), F32)
    lax.fori_loop(0, nb, blk, (zero, zero))


def route(logits, tri, cap):
    b, nb, e, t = logits.shape
    kern = functools.partial(_route_kernel, cap=cap)
    spec = pl.BlockSpec((1, nb, e, t), lambda i: (i, 0, 0, 0))
    return pl.pallas_call(
        kern,
        grid=(b,),
        in_specs=[spec, pl.BlockSpec(tri.shape, lambda i: (0, 0))],
        out_specs=[spec, spec, pl.BlockSpec((1, nb, e, LANES), lambda i: (i, 0, 0, 0))],
        out_shape=[jax.ShapeDtypeStruct((b, nb, e, t), F32), jax.ShapeDtypeStruct((b, nb, e, t), jnp.int32),
                   jax.ShapeDtypeStruct((b, nb, e, LANES), jnp.int32)],
        compiler_params=_cparams(("arbitrary",)),
        name="route",
    )(logits, tri)


SLOT_ALIGN = 16


def _for_each_slot_window(off_ref, idx0, nsub, st, cap, fn):
    wins = []
    for jj in range(nsub):
        off, nxt = off_ref[idx0 + jj], off_ref[idx0 + jj + 1]
        start = pl.multiple_of(jnp.minimum(off // SLOT_ALIGN * SLOT_ALIGN, cap - st), SLOT_ALIGN)
        fn(jj, start, None)
        wins.append((start, nxt > start + st))

    @pl.when(functools.reduce(jnp.logical_or, [spill for _, spill in wins]))
    def _():
        for jj, (start, spill) in enumerate(wins):
            @pl.when(spill)
            def _():
                fn(jj, pl.multiple_of(jnp.minimum(start + st, cap - st), SLOT_ALIGN), start + st)


def _slot_one_hot(slot, prow, start, lower):
    hit = slot == (prow - start)
    return hit if lower is None else hit & (prow >= lower)


def _gather_kernel(off_ref, h_ref, pos_ref, aff_ref, xe_ref, gs_ref, acc_ref, gacc_ref, *, st, cap, nsub, nb):
    bi, e, ch = pl.program_id(0), pl.program_id(1), pl.program_id(2)

    @pl.when(ch == 0)
    def _():
        acc_ref[...] = jnp.zeros(acc_ref.shape, F32)
        gacc_ref[...] = jnp.zeros(gacc_ref.shape, F32)

    slot = lax.broadcasted_iota(jnp.int32, (st, ROUTE_T), 0)

    def add(jj, start, lower):
        hit = _slot_one_hot(slot, pos_ref[0, jj, pl.ds(e, 1), :], start, lower)
        acc_ref[pl.ds(start, st), :] += _dot(jnp.where(hit, 1.0, 0.0).astype(BF16),
                                             h_ref[0, jj * ROUTE_T:(jj + 1) * ROUTE_T, :])
        gacc_ref[pl.ds(start, st), :] += jnp.sum(jnp.where(hit, aff_ref[0, jj, pl.ds(e, 1), :], 0.0),
                                                 axis=1, keepdims=True)

    _for_each_slot_window(off_ref, (bi * N_EXPERTS + e) * (nb + 1) + ch * nsub, nsub, st, cap, add)

    @pl.when(ch == pl.num_programs(2) - 1)
    def _():
        xe_ref[0, 0] = acc_ref[...].astype(BF16)
        gs_ref[0, 0] = gacc_ref[...]


def moe_gather(offs, h, pos, aff, cap):
    b, n, d = h.shape
    nb = n // ROUTE_T
    st = min(ROUTE_T, cap)
    chunk = min(2048, n)
    nsub = chunk // ROUTE_T
    kern = functools.partial(_gather_kernel, st=st, cap=cap, nsub=nsub, nb=nb)
    rspec = pl.BlockSpec((1, nsub, N_EXPERTS, ROUTE_T), lambda i, e, c, off: (i, c, 0, 0))
    return pl.pallas_call(
        kern,
        grid_spec=pltpu.PrefetchScalarGridSpec(
            num_scalar_prefetch=1,
            grid=(b, N_EXPERTS, n // chunk),
            in_specs=[pl.BlockSpec((1, chunk, d), lambda i, e, c, off: (i, c, 0)), rspec, rspec],
            out_specs=[pl.BlockSpec((1, 1, cap, d), lambda i, e, c, off: (i, e, 0, 0)),
                       pl.BlockSpec((1, 1, cap, 1), lambda i, e, c, off: (i, e, 0, 0))],
            scratch_shapes=[pltpu.VMEM((cap, d), F32), pltpu.VMEM((cap, 1), F32)]),
        out_shape=[jax.ShapeDtypeStruct((b, N_EXPERTS, cap, d), BF16),
                   jax.ShapeDtypeStruct((b, N_EXPERTS, cap, 1), F32)],
        compiler_params=_cparams(("parallel", "parallel", "arbitrary")),
        name="moe_gather",
    )(offs, h, pos, aff)


FF_TILE = 256


def _ffn_kernel(x_ref, g_ref, wg_ref, wu_ref, wd_ref, y_ref, acc_ref):
    f = pl.program_id(2)

    @pl.when(f == 0)
    def _():
        acc_ref[...] = jnp.zeros(acc_ref.shape, F32)

    x = x_ref[0, 0]
    gate = _dot(x, wg_ref[0, 0].astype(BF16))
    up = _dot(x, wu_ref[0, 0].astype(BF16))
    hmid = (_silu(gate) * up).astype(BF16)
    acc_ref[...] += _dot(hmid, wd_ref[0, 0].astype(BF16))

    @pl.when(f == pl.num_programs(2) - 1)
    def _():
        y_ref[0, 0] = (acc_ref[...] * g_ref[0, 0]).astype(BF16)


def moe_ffn(xe, gs, w_gate, w_up, w_down, layer):
    b, e, cap, d = xe.shape
    dff = w_gate.shape[3]
    return pl.pallas_call(
        _ffn_kernel,
        grid=(e, b, dff // FF_TILE),
        in_specs=[pl.BlockSpec((1, 1, cap, d), lambda ei, bi, f: (bi, ei, 0, 0)),
                  pl.BlockSpec((1, 1, cap, 1), lambda ei, bi, f: (bi, ei, 0, 0)),
                  pl.BlockSpec((1, 1, d, FF_TILE), lambda ei, bi, f: (layer, ei, 0, f)),
                  pl.BlockSpec((1, 1, d, FF_TILE), lambda ei, bi, f: (layer, ei, 0, f)),
                  pl.BlockSpec((1, 1, FF_TILE, d), lambda ei, bi, f: (layer, ei, f, 0))],
        out_specs=pl.BlockSpec((1, 1, cap, d), lambda ei, bi, f: (bi, ei, 0, 0)),
        out_shape=jax.ShapeDtypeStruct((b, e, cap, d), BF16),
        scratch_shapes=[pltpu.VMEM((cap, d), F32)],
        compiler_params=_cparams(("parallel", "parallel", "arbitrary")),
        name="moe_ffn",
    )(xe, gs, w_gate, w_up, w_down)


def _combine_kernel(off_ref, x_ref, ye_ref, pos_ref, mod_ref, o_ref, *, st, cap, nsub, nb):
    bi, ch, e = pl.program_id(0), pl.program_id(1), pl.program_id(2)

    @pl.when(e == 0)
    def _():
        o_ref[...] = jnp.zeros(o_ref.shape, F32)

    slot = lax.broadcasted_iota(jnp.int32, (st, ROUTE_T), 0)

    def add(jj, start, lower):
        hit = _slot_one_hot(slot, pos_ref[0, jj, pl.ds(e, 1), :], start, lower)
        oh = jnp.where(hit, 1.0, 0.0).astype(BF16)
        o_ref[0, jj * ROUTE_T:(jj + 1) * ROUTE_T, :] += lax.dot_general(
            oh, ye_ref[0, 0, pl.ds(start, st), :], (((0,), (0,)), ((), ())), preferred_element_type=F32)

    _for_each_slot_window(off_ref, (bi * N_EXPERTS + e) * (nb + 1) + ch * nsub, nsub, st, cap, add)

    @pl.when(e == pl.num_programs(2) - 1)
    def _():
        o_ref[0] = x_ref[0] + mod_ref[0, 5:6, :] * o_ref[0]


def moe_combine(offs, x, ye, pos, mod, cap, slot0):
    b, n, d = x.shape
    nb = n // ROUTE_T
    st = min(ROUTE_T, cap)
    chunk = min(2048, n)
    nsub = chunk // ROUTE_T
    assert slot0 % cap == 0
    kern = functools.partial(_combine_kernel, st=st, cap=cap, nsub=nsub, nb=nb)
    return pl.pallas_call(
        kern,
        grid_spec=pltpu.PrefetchScalarGridSpec(
            num_scalar_prefetch=1,
            grid=(b, n // chunk, N_EXPERTS),
            in_specs=[pl.BlockSpec((1, chunk, d), lambda i, c, e, off: (i, c, 0)),
                      pl.BlockSpec((1, 1, cap, d), lambda i, c, e, off: (i, e, slot0 // cap, 0)),
                      pl.BlockSpec((1, nsub, N_EXPERTS, ROUTE_T), lambda i, c, e, off: (i, c, 0, 0)),
                      pl.BlockSpec((1, 8, d), lambda i, c, e, off: (i, 0, 0))],
            out_specs=pl.BlockSpec((1, chunk, d), lambda i, c, e, off: (i, c, 0))),
        out_shape=jax.ShapeDtypeStruct((b, n, d), F32),
        compiler_params=_cparams(("parallel", "parallel", "arbitrary")),
        name="moe_combine",
    )(offs, x, ye, pos, mod)


def expert_choice_ffn(sets, tri, w_gate, w_up, w_down, layer):
    routed = []
    for x, h, logits, mod in sets:
        b, n, _ = x.shape
        cap = EC_CAPACITY * n // N_EXPERTS
        aff, pos, off = route(logits, tri, cap)
        first = jnp.transpose(off[..., 0], (0, 2, 1))
        offs = jnp.concatenate([first, jnp.full((b, N_EXPERTS, 1), cap, jnp.int32)], axis=2).reshape(-1)
        xe, gs = moe_gather(offs, h, pos, aff, cap)
        routed.append((offs, pos, cap, xe, gs))
    xe_all = jnp.concatenate([r[3] for r in routed], axis=2) if len(routed) > 1 else routed[0][3]
    gs_all = jnp.concatenate([r[4] for r in routed], axis=2) if len(routed) > 1 else routed[0][4]
    ye = moe_ffn(xe_all, gs_all, w_gate, w_up, w_down, layer)
    outs, slot0 = [], 0
    for (x, _, _, mod), (offs, pos, cap, _, _) in zip(sets, routed):
        outs.append(moe_combine(offs, x, ye, pos, mod, cap, slot0))
        slot0 += cap
    return outs


def _rope_tables(n_rows_grid):
    s = n_rows_grid * GRID_W
    t = np.arange(s)
    row, col = (t // GRID_W).astype(np.float64), (t % GRID_W).astype(np.float64)

    def unit(n):
        half = n // 2
        inv = ROPE_THETA ** (-np.arange(0, n, 2, dtype=np.float64) / n)
        out = []
        for pos in (row, col):
            ang = pos[:, None] * inv[None, :]
            c, sn = np.cos(ang), np.sin(ang)
            z = np.zeros_like(sn)
            out.append((np.concatenate([c, c], 1), np.concatenate([-sn, z], 1), np.concatenate([z, sn], 1)))
        return [np.concatenate([out[0][i], out[1][i]], 1) for i in range(3)]

    ident = lambda w: (np.ones((s, w), np.float32), np.zeros((s, w), np.float32), np.zeros((s, w), np.float32))
    u16 = unit(16)
    u32 = unit(32)
    tabs = []
    idt = ident(64)
    idt32 = ident(32)
    for i in range(3):
        tabs.append(np.concatenate([idt[i], u16[i], idt32[i]], 1))
    for i in range(3):
        tabs.append(np.concatenate([u16[i]] * 4, 1))
    for i in range(3):
        tabs.append(np.concatenate([u32[i]] * 2, 1))
    return np.stack(tabs).astype(np.float32)


def _identity_tables(n):
    one, zero = np.ones((n, LANES), np.float32), np.zeros((n, LANES), np.float32)
    return np.stack([one, zero, zero] * 3)


def _seg_matrices():
    i = np.arange(LANES)
    ones = np.ones((LANES, LANES), np.float32)
    m64 = (i[:, None] // 64 == i[None, :] // 64).astype(np.float32)
    m32 = (i[:, None] // 32 == i[None, :] // 32).astype(np.float32)
    return np.stack([ones, m64, m32])


def _na_bias(rpb):
    c = np.arange(GRID_W)
    rl = np.arange(NA_QROWS)
    dc = np.clip(c[None, :] - c[:, None] + NA_WIN_W - 1, 0, 2 * NA_WIN_W - 2)
    d = np.array([-1, 0, 1])
    dr = np.clip(NA_QROWS * d[:, None, None] + rl[None, None, :] - rl[None, :, None] + NA_WIN_H - 1,
                 0, 2 * NA_WIN_H - 2)
    cols = jnp.take(rpb.astype(F32) * LOG2E, jnp.asarray(dc), axis=2)
    full = jnp.take(cols, jnp.asarray(dr), axis=1)
    return jnp.transpose(full, (0, 1, 2, 4, 3, 5)).reshape(rpb.shape[0], 3, NA_TQ, NA_TQ)


def _pad_cols(w, width):
    return jnp.pad(w, ((0, 0), (0, width - w.shape[1])))


def _layer_params(l, w_in, g_na_q, g_na_k, na_rpb, g_mla_cq, w_mla_uq, g_mla_q, g_mla_ckv, w_mla_ukv,
                  g_mla_k_nope, g_mla_k_rope, g_diff_q, g_diff_k, g_diff_sub, g_gqa_q, g_gqa_k, w_out):
    wi = w_in[l]
    (naq, nak, nav, cq, ckv, kr, dq, dk, dv, gq, gk, gv) = jnp.split(
        wi, np.cumsum([256, 256, 256, 256, 128, 32, 256, 256, 256, 256, 128])[:], axis=1)
    gq4 = gq.reshape(-1, 4, 64)[:, jnp.array([0, 2, 1, 3])].reshape(-1, 256)
    zeros = lambda w: jnp.zeros((wi.shape[0], w), wi.dtype)
    w_in_r = jnp.concatenate([naq, nak, nav, cq, ckv, dq, dk, dv, gq4, gk, gv, zeros(64), kr, zeros(32)],
                             axis=1).astype(BF16)
    uq = w_mla_uq[l].reshape(MLA_Q_RANK, 4, 96)
    wuq = jnp.pad(uq, ((0, 0), (0, 0), (0, 32))).reshape(MLA_Q_RANK, 512).astype(BF16)
    ukv = w_mla_ukv[l].reshape(MLA_KV_RANK, 4, 128)
    wuk = jnp.pad(ukv[:, :, :64], ((0, 0), (0, 0), (0, 64))).reshape(MLA_KV_RANK, 512).astype(BF16)
    wuv = ukv[:, :, 64:].reshape(MLA_KV_RANK, 256).astype(BF16)
    row = lambda v: jnp.pad(v, (0, 512 - v.shape[0]))
    z32, z64 = jnp.zeros((32,), F32), jnp.zeros((64,), F32)
    gains = jnp.stack([
        row(jnp.tile(g_na_q[l], 4)), row(jnp.tile(g_na_k[l], 4)), row(g_mla_cq[l]),
        row(jnp.tile(jnp.concatenate([g_mla_q[l], z32]), 4)), row(g_mla_ckv[l]),
        row(jnp.tile(jnp.concatenate([g_mla_k_nope[l], z64]), 4)),
        row(jnp.concatenate([z64, g_mla_k_rope[l], z32])),
        row(jnp.tile(g_diff_q[l].reshape(-1), 4)), row(jnp.tile(g_diff_k[l].reshape(-1), 4)),
        row(jnp.tile(g_gqa_q[l], 4)), row(jnp.tile(g_gqa_k[l], 2))] + [jnp.zeros((512,), F32)] * 5)
    wo = w_out[l]
    wo_g = wo[768:].reshape(4, 64, -1)[jnp.array([0, 2, 1, 3])].reshape(256, -1)
    w_out_r = jnp.concatenate([wo[:768], wo_g], axis=0).astype(BF16)
    return dict(w_in=w_in_r, wuq=wuq, wuk=wuk, wuv=wuv, gains=gains, w_out=w_out_r,
                bias=_na_bias(na_rpb[l]), g_sub=jnp.tile(g_diff_sub[l], 2).reshape(1, LANES))


def kernel(x, c, ctx, c_ctx, w_mod, b_mod, g_attn, g_ffn, w_in, g_na_q, g_na_k, na_rpb, g_mla_cq, w_mla_uq, g_mla_q, g_mla_ckv, w_mla_ukv, g_mla_k_nope, g_mla_k_rope, g_diff_q, g_diff_k, diff_lambda, g_diff_sub, g_gqa_q, g_gqa_k, w_out, w_router, w_gate, w_up, w_down):
    b, s, d = x.shape
    n_ctx = ctx.shape[1]
    rows = s // GRID_W
    tabs = jnp.asarray(_rope_tables(rows))
    tabs_ctx = jnp.asarray(_identity_tables(n_ctx))
    mats = jnp.asarray(_seg_matrices()).astype(BF16)
    tri = jnp.asarray(np.triu(np.ones((ROUTE_T, ROUTE_T), np.float32), 1)).astype(BF16)
    c_rows = jnp.concatenate([c, c_ctx[None], jnp.zeros((8 - b - 1, d), F32)], axis=0)
    xc = ctx
    for l in range(DEPTH):
        need_ctx = l < DEPTH - 1
        lam_init = 0.8 - 0.6 * math.exp(-0.3 * l)
        lp = diff_lambda[l].astype(F32)
        lam = (jnp.exp(jnp.sum(lp[0] * lp[1])) - jnp.exp(jnp.sum(lp[2] * lp[3])) + lam_init).reshape(1, 1)
        prm = _layer_params(l, w_in, g_na_q, g_na_k, na_rpb, g_mla_cq, w_mla_uq, g_mla_q, g_mla_ckv, w_mla_ukv,
                            g_mla_k_nope, g_mla_k_rope, g_diff_q, g_diff_k, g_diff_sub, g_gqa_q, g_gqa_k, w_out)
        modv = mod_vectors(c_rows, w_mod[l], b_mod[l]).reshape(8, N_MOD, d)
        mod = jnp.pad(modv[:b], ((0, 0), (0, 2), (0, 0)))
        mod_c = jnp.broadcast_to(jnp.pad(modv[b:b + 1], ((0, 0), (0, 2), (0, 0))), (b, 8, d))
        w_router_t = w_router[l].T

        def mix_inputs(xin, m, tb):
            p = ln_mod_proj(xin, g_attn[l], m[:, 0:1], m[:, 1:2], prm["w_in"])
            return prep(p, tb, prm["gains"], mats, prm["wuq"], prm["wuk"], prm["wuv"])

        (naq, nak, nav, mq, mk, mv, dq, dk, dv, gq, gk, gv, stat) = mix_inputs(x, mod, tabs)
        (naq_c, nak_c, nav_c, mq_c, mk_c, mv_c, dq_c, dk_c, dv_c, gq_c, gk_c, gv_c, stat_c) = mix_inputs(
            xc, mod_c, tabs_ctx)
        cat = lambda a, bb: jnp.concatenate([a, bb], axis=1)
        n2 = jnp.max(stat, axis=(1, 3))
        n2c = jnp.max(stat_c, axis=(1, 3))
        q2 = lambda name: n2[:, STAT_ROWS.index(name)]
        k2 = lambda name: jnp.maximum(n2[:, STAT_ROWS.index(name)], n2c[:, STAT_ROWS.index(name)])

        ext = values_transposed
        o_na = na_attention(naq, nak, nav, nak_c, nav_c, prm["bias"], rows)
        o_mla = attention(q2("mq"), k2("mk"), mq, cat(mk_c, mk), ext(cat(mv_c, mv)), HEADS_MLA)
        o_diff = attention(q2("dq"), k2("dk"), dq, cat(dk_c, dk), ext(cat(dv_c, dv)), HEADS_DIFF)
        o_gqa = attention(q2("gq"), k2("gk"), gq, cat(gk_c, gk), ext(cat(gv_c, gv)), HEADS_GQA)
        x_mid, h2, logits = outproj_mod_router(x, o_na, o_mla, o_diff, o_gqa, mod, prm["g_sub"], g_ffn[l], lam,
                                               mats, prm["w_out"], w_router_t, lam_init)
        if need_ctx:
            oc_na = flash_attention(naq_c, nak_c, ext(nav_c), HEADS_PAIRED)
            oc_mla = flash_attention(mq_c, mk_c, ext(mv_c), HEADS_MLA)
            oc_diff = flash_attention(dq_c, dk_c, ext(dv_c), HEADS_DIFF)
            oc_gqa = flash_attention(gq_c, gk_c, ext(gv_c), HEADS_GQA)
            xc_mid, hc2, logits_c = outproj_mod_router(xc, oc_na, oc_mla, oc_diff, oc_gqa, mod_c, prm["g_sub"],
                                                       g_ffn[l], lam, mats, prm["w_out"], w_router_t, lam_init)
            x, xc = expert_choice_ffn([(x_mid, h2, logits, mod), (xc_mid, hc2, logits_c, mod_c)],
                                      tri, w_gate, w_up, w_down, l)
        else:
            (x,) = expert_choice_ffn([(x_mid, h2, logits, mod)], tri, w_gate, w_up, w_down, l)
    return x
```

```python
import functools
import math

import numpy as np
import jax
import jax.numpy as jnp
from jax import lax
from jax.experimental import pallas as pl
from jax.experimental.pallas import tpu as pltpu

D_MODEL = 1024
GRID_W = 64
HEAD_DIM = 64
N_HEADS = 4
NA_WIN_H = 8
NA_WIN_W = 16
MLA_Q_RANK = 256
MLA_KV_RANK = 128
MLA_NOPE_DIM = 64
MLA_ROPE_DIM = 32
MLA_V_DIM = 64
DIFF_QK_DIM = 32
N_EXPERTS = 16
EC_CAPACITY = 2
D_FF = 2816
ROPE_THETA = 10000.0
NORM_EPS = 1e-6
N_MOD = 6
DEPTH = 2

LANES = 128
P_COLS = 2560
VMEM_LIMIT = 56 * 1024 * 1024
NEG_BIG = -1e30
LOG2E = math.log2(math.e)

F32 = jnp.float32
BF16 = jnp.bfloat16


def _cparams(sem):
    return pltpu.CompilerParams(dimension_semantics=sem, vmem_limit_bytes=VMEM_LIMIT)


def _split(a):
    hi = a.astype(BF16)
    lo = (a - hi.astype(F32)).astype(BF16)
    return hi, lo


def _dot(a, b):
    return jnp.dot(a, b, preferred_element_type=F32)


def _dot_nt(a, b):
    return lax.dot_general(a, b, (((1,), (1,)), ((), ())), preferred_element_type=F32)


def _dot3(a, b):
    ah, al = _split(a)
    bh, bl = _split(b)
    return _dot(ah, bh) + _dot(ah, bl) + _dot(al, bh)


def _dot3_nt(a, b):
    ah, al = _split(a)
    bh, bl = _split(b)
    return _dot_nt(ah, bh) + _dot_nt(ah, bl) + _dot_nt(al, bh)


def _silu(v):
    return v * jax.nn.sigmoid(v)


def _mod_kernel(c_ref, w_ref, b_ref, o_ref):
    o_ref[...] = _dot3(_silu(c_ref[...]), w_ref[...]) + b_ref[...]


def mod_vectors(c_rows, w_mod, b_mod):
    m, d = c_rows.shape
    n = w_mod.shape[1]
    tn = 1536
    return pl.pallas_call(
        _mod_kernel,
        grid=(n // tn,),
        in_specs=[pl.BlockSpec((m, d), lambda j: (0, 0)),
                  pl.BlockSpec((d, tn), lambda j: (0, j)),
                  pl.BlockSpec((1, tn), lambda j: (0, j))],
        out_specs=pl.BlockSpec((m, tn), lambda j: (0, j)),
        out_shape=jax.ShapeDtypeStruct((m, n), F32),
        compiler_params=_cparams(("arbitrary",)),
        name="mod_vectors",
    )(c_rows, w_mod, b_mod.reshape(1, n))


P_NAQ, P_NAK, P_NAV, P_CQ, P_CKV = 0, 256, 512, 768, 1024
P_DQ, P_DK, P_DV, P_GQ, P_GK, P_GV, P_KR = 1152, 1408, 1664, 1920, 2176, 2304, 2432
(G_NAQ, G_NAK, G_CQ, G_MQ, G_CKV, G_KN, G_KR, G_DQ, G_DK, G_GQ, G_GK) = range(11)
M_ONES, M_SEG64, M_SEG32 = 0, 1, 2
T_MLA, T_DIFF, T_GQA = 0, 3, 6


def _seg_norm(x, mat, inv_n, g):
    sq = x * x
    hi, lo = _split(sq)
    ms = (_dot(hi, mat) + _dot(lo, mat)) * inv_n
    return x * lax.rsqrt(ms + NORM_EPS) * g


def _rope(x, tab_ref, t0, half):
    c, s1, s2 = tab_ref[t0], tab_ref[t0 + 1], tab_ref[t0 + 2]
    return (x * c + pltpu.roll(x, LANES - half, axis=1) * s1
            + pltpu.roll(x, half, axis=1) * s2)


def _prep_kernel(x_ref, g_ref, sh_ref, sc_ref, w_ref, tab_ref, gain_ref, mat_ref, wuq_ref, wuk_ref, wuv_ref,
                 naq_ref, nak_ref, nav_ref, mq_ref, mk_ref, mv_ref,
                 dq_ref, dk_ref, dv_ref, gq_ref, gk_ref, gv_ref, st_ref):
    lane = lax.broadcasted_iota(jnp.int32, (1, LANES), 1)
    lo_half = lane < 64
    ones_m, m64, m32 = mat_ref[M_ONES], mat_ref[M_SEG64], mat_ref[M_SEG32]
    norm2 = {}

    x = x_ref[0]
    y = x * lax.rsqrt(jnp.mean(x * x, axis=-1, keepdims=True) + NORM_EPS) * g_ref[...]
    p = _dot((y * (1.0 + sc_ref[0]) + sh_ref[0]).astype(BF16), w_ref[...])

    def slab(off, i):
        return p[:, off + i * LANES: off + (i + 1) * LANES]

    def track(name, xb, mat):
        xf = xb.astype(F32)
        hi, lo = _split(xf * xf)
        cur = jnp.max(_dot(hi, mat) + _dot(lo, mat), axis=0, keepdims=True)
        norm2[name] = jnp.maximum(norm2[name], cur) if name in norm2 else cur
        return xb

    def track_elem(name, xb):
        for i in range(xb.shape[1] // LANES):
            xf = xb[:, i * LANES:(i + 1) * LANES].astype(F32)
            cur = jnp.max(xf * xf, axis=0, keepdims=True)
            norm2[name] = jnp.maximum(norm2[name], cur) if name in norm2 else cur
        return xb

    def gain(row, i):
        return gain_ref[row:row + 1, i * LANES:(i + 1) * LANES]

    s_na = HEAD_DIM ** -0.5 * LOG2E
    for i in range(2):
        q = _seg_norm(slab(P_NAQ, i), m64, 1.0 / 64, gain(G_NAQ, i)) * s_na
        naq_ref[0, :, (2 * i) * LANES:(2 * i + 1) * LANES] = jnp.where(lo_half, q, 0.0).astype(BF16)
        naq_ref[0, :, (2 * i + 1) * LANES:(2 * i + 2) * LANES] = jnp.where(lo_half, 0.0, q).astype(BF16)
        k = _seg_norm(slab(P_NAK, i), m64, 1.0 / 64, gain(G_NAK, i))
        nak_ref[0, :, i * LANES:(i + 1) * LANES] = k.astype(BF16)
        nav_ref[0, :, i * LANES:(i + 1) * LANES] = slab(P_NAV, i).astype(BF16)

    cq = p[:, P_CQ:P_CQ + MLA_Q_RANK]
    cq = cq * lax.rsqrt(jnp.mean(cq * cq, axis=-1, keepdims=True) + NORM_EPS) * gain_ref[G_CQ:G_CQ + 1, :MLA_Q_RANK]
    uq = _dot(cq.astype(BF16), wuq_ref[...])
    s_mla = (MLA_NOPE_DIM + MLA_ROPE_DIM) ** -0.5 * LOG2E
    ckv = p[:, P_CKV:P_CKV + MLA_KV_RANK]
    ckv = ckv * lax.rsqrt(jnp.mean(ckv * ckv, axis=-1, keepdims=True) + NORM_EPS) * gain_ref[G_CKV:G_CKV + 1, :MLA_KV_RANK]
    ckv_b = ckv.astype(BF16)
    uk = _dot(ckv_b, wuk_ref[...])
    mv_ref[0] = track_elem("mv", _dot(ckv_b, wuv_ref[...]).astype(BF16))
    kr = _seg_norm(slab(P_KR, 0), ones_m, 1.0 / MLA_ROPE_DIM, gain(G_KR, 0))
    kr = _rope(kr, tab_ref, T_MLA, 8)
    for h in range(N_HEADS):
        q = _seg_norm(uq[:, h * LANES:(h + 1) * LANES], ones_m, 1.0 / (MLA_NOPE_DIM + MLA_ROPE_DIM), gain(G_MQ, h))
        q = _rope(q, tab_ref, T_MLA, 8) * s_mla
        mq_ref[0, :, h * LANES:(h + 1) * LANES] = track("mq", q.astype(BF16), ones_m)
        kn = _seg_norm(uk[:, h * LANES:(h + 1) * LANES], ones_m, 1.0 / MLA_NOPE_DIM, gain(G_KN, h))
        mk_ref[0, :, h * LANES:(h + 1) * LANES] = track("mk", (kn + kr).astype(BF16), ones_m)

    s_d = DIFF_QK_DIM ** -0.5 * LOG2E
    seg = lane >> 5
    for i in range(2):
        q = _seg_norm(slab(P_DQ, i), m32, 1.0 / 32, gain(G_DQ, i))
        q = _rope(q, tab_ref, T_DIFF, 8) * s_d
        track("dq", q.astype(BF16), m32)
        for j in range(4):
            dq_ref[0, :, (4 * i + j) * LANES:(4 * i + j + 1) * LANES] = jnp.where(seg == j, q, 0.0).astype(BF16)
        k = _seg_norm(slab(P_DK, i), m32, 1.0 / 32, gain(G_DK, i))
        dk_ref[0, :, i * LANES:(i + 1) * LANES] = track("dk", _rope(k, tab_ref, T_DIFF, 8).astype(BF16), m32)
        dv_ref[0, :, i * LANES:(i + 1) * LANES] = track_elem("dv", slab(P_DV, i).astype(BF16))

    s_g = HEAD_DIM ** -0.5 * LOG2E
    for i in range(2):
        q = _seg_norm(slab(P_GQ, i), m64, 1.0 / 64, gain(G_GQ, i))
        q = _rope(q, tab_ref, T_GQA, 16) * s_g
        track("gq", q.astype(BF16), m64)
        gq_ref[0, :, (2 * i) * LANES:(2 * i + 1) * LANES] = jnp.where(lo_half, q, 0.0).astype(BF16)
        gq_ref[0, :, (2 * i + 1) * LANES:(2 * i + 2) * LANES] = jnp.where(lo_half, 0.0, q).astype(BF16)
    k = _seg_norm(slab(P_GK, 0), m64, 1.0 / 64, gain(G_GK, 0))
    gk_ref[0] = track("gk", _rope(k, tab_ref, T_GQA, 16).astype(BF16), m64)
    gv_ref[0] = track_elem("gv", slab(P_GV, 0).astype(BF16))
    st_ref[0, 0] = jnp.concatenate([norm2[n] for n in STAT_ROWS]
                                   + [jnp.zeros((STAT_PAD - len(STAT_ROWS), LANES), F32)], axis=0)


PREP_WIDTHS = (512, 256, 256, 512, 512, 256, 1024, 256, 256, 512, 128, 128)
STAT_ROWS = ("mq", "mk", "mv", "dq", "dk", "dv", "gq", "gk", "gv")
STAT_PAD = 16


def proj_prep(x, g, shift, scale, w_in, tabs, gains, mats, wuq, wuk, wuv):
    b, n, d = x.shape
    tm = min(512, n)
    const2 = lambda i, j: (0, 0)
    stat_spec = pl.BlockSpec((1, 1, STAT_PAD, LANES), lambda i, j: (i, j, 0, 0))
    stat_shape = jax.ShapeDtypeStruct((b, n // tm, STAT_PAD, LANES), F32)
    return pl.pallas_call(
        _prep_kernel,
        grid=(b, n // tm),
        in_specs=[pl.BlockSpec((1, tm, d), lambda i, j: (i, j, 0)),
                  pl.BlockSpec((1, d), const2),
                  pl.BlockSpec((1, 1, d), lambda i, j: (i, 0, 0)),
                  pl.BlockSpec((1, 1, d), lambda i, j: (i, 0, 0)),
                  pl.BlockSpec((d, P_COLS), const2),
                  pl.BlockSpec((9, tm, LANES), lambda i, j: (0, j, 0)),
                  pl.BlockSpec(gains.shape, const2),
                  pl.BlockSpec(mats.shape, lambda i, j: (0, 0, 0)),
                  pl.BlockSpec(wuq.shape, const2),
                  pl.BlockSpec(wuk.shape, const2),
                  pl.BlockSpec(wuv.shape, const2)],
        out_specs=[pl.BlockSpec((1, tm, w), lambda i, j: (i, j, 0)) for w in PREP_WIDTHS] + [stat_spec],
        out_shape=[jax.ShapeDtypeStruct((b, n, w), BF16) for w in PREP_WIDTHS] + [stat_shape],
        compiler_params=_cparams(("parallel", "arbitrary")),
        name="proj_prep",
    )(x, g.reshape(1, d), shift, scale, w_in, tabs, gains, mats, wuq, wuk, wuv)


VT_ROWS = LANES + 16


def _value_rows(heads, h):
    pair = [g for g in range(len(heads)) if heads[g][3] == heads[h][3]]
    return pair.index(h) * 64, pair.index(h) * 64 + 64
KEY_CHUNK = 256
FLASH_TQ = 512
FLASH_TK_MAX = 3584


def _flash_kernel(q_ref, k_ref, vt_ref, o_ref, m_ref, l_ref, acc_ref, *, heads, n_out):
    ki = pl.program_id(2)

    @pl.when(ki == 0)
    def _():
        m_ref[...] = jnp.full(m_ref.shape, NEG_BIG, F32)
        l_ref[...] = jnp.zeros(l_ref.shape, F32)
        acc_ref[...] = jnp.zeros(acc_ref.shape, F32)

    for o in range(n_out):
        pair = [h for h in range(len(heads)) if heads[h][3] == o]
        for pos, h in enumerate(pair):
            qs, ks, vs, _ = heads[h]
            q = q_ref[0, :, qs * LANES:(qs + 1) * LANES]
            k = k_ref[0, :, ks * LANES:(ks + 1) * LANES]
            st = _dot_nt(k, q)
            lo, hi = _value_rows(heads, h)
            m, l, acc = m_ref[h], l_ref[h], acc_ref[o, lo:hi, :]
            for c0 in range(0, st.shape[0], KEY_CHUNK):
                rows = slice(c0, min(c0 + KEY_CHUNK, st.shape[0]))
                sc = st[rows]
                m_new = jnp.maximum(m, jnp.max(sc, axis=0, keepdims=True))
                a = jnp.exp2(m - m_new)
                pt = jnp.exp2(sc - m_new).astype(BF16)
                r = _dot(vt_ref[0, vs, :, rows], pt)
                l = a * l + r[LANES:LANES + 1]
                acc = acc * a + r[lo:hi]
                m = m_new
            m_ref[h], l_ref[h] = m, l
            acc_ref[o, lo:hi, :] = acc

    @pl.when(ki == pl.num_programs(2) - 1)
    def _():
        for o in range(n_out):
            pair = [h for h in range(len(heads)) if heads[h][3] == o]
            out_t = jnp.concatenate([acc_ref[o, 0:64, :] / l_ref[pair[0]],
                                     acc_ref[o, 64:128, :] / l_ref[pair[1]]], axis=0)
            o_ref[0, :, o * LANES:(o + 1) * LANES] = out_t.T


def _pick_tk(n_keys):
    best = 256
    for t in range(256, FLASH_TK_MAX + 1, 256):
        if n_keys % t == 0:
            best = t
    return best


def values_transposed(v):
    b, n, w = v.shape
    vt = jnp.transpose(v.reshape(b, n, w // LANES, LANES), (0, 2, 3, 1))
    return jnp.concatenate([vt, jnp.ones((b, w // LANES, VT_ROWS - LANES, n), v.dtype)], axis=2)


def flash_attention(q, k, vt, heads):
    b, s, qw = q.shape
    nk = k.shape[1]
    n_out = max(h[3] for h in heads) + 1
    tq = min(FLASH_TQ, s)
    tk = _pick_tk(nk)
    kern = functools.partial(_flash_kernel, heads=heads, n_out=n_out)
    return pl.pallas_call(
        kern,
        grid=(b, s // tq, nk // tk),
        in_specs=[pl.BlockSpec((1, tq, qw), lambda i, j, t: (i, j, 0)),
                  pl.BlockSpec((1, tk, k.shape[2]), lambda i, j, t: (i, t, 0)),
                  pl.BlockSpec((1, vt.shape[1], VT_ROWS, tk), lambda i, j, t: (i, 0, 0, t))],
        out_specs=pl.BlockSpec((1, tq, n_out * LANES), lambda i, j, t: (i, j, 0)),
        out_shape=jax.ShapeDtypeStruct((b, s, n_out * LANES), F32),
        scratch_shapes=[pltpu.VMEM((len(heads), 1, tq), F32),
                        pltpu.VMEM((len(heads), 1, tq), F32),
                        pltpu.VMEM((n_out, LANES, tq), F32)],
        compiler_params=_cparams(("parallel", "parallel", "arbitrary")),
        name="flash_attention",
    )(q, k, vt)


STAB_SHIFT = 100.0
BOUND_LIMIT = 113.0
BOUND_SLACK = 1.01
SUM_LOG2_LIMIT = 126.0
BOUNDED_TQ = 1024


def _flash_bounded_kernel(stab_ref, q_ref, k_ref, vt_ref, o_ref, acc_ref, *, heads, n_out):
    ki = pl.program_id(2)

    @pl.when(ki == 0)
    def _():
        acc_ref[...] = jnp.zeros(acc_ref.shape, F32)

    stab = stab_ref[pl.program_id(0), 0]
    for h, (qs, ks, vs, _) in enumerate(heads):
        q = q_ref[0, :, qs * LANES:(qs + 1) * LANES]
        k = k_ref[0, :, ks * LANES:(ks + 1) * LANES]
        pt = jnp.exp2(_dot_nt(k, q) - stab).astype(BF16)
        acc_ref[h] += _dot(vt_ref[0, vs], pt)

    @pl.when(ki == pl.num_programs(2) - 1)
    def _():
        for o in range(n_out):
            pair = [h for h in range(len(heads)) if heads[h][3] == o]
            parts = []
            for h in pair:
                lo, hi = _value_rows(heads, h)
                parts.append(acc_ref[h, lo:hi, :] / acc_ref[h, LANES:LANES + 1, :])
            o_ref[0, :, o * LANES:(o + 1) * LANES] = jnp.concatenate(parts, axis=0).T


def flash_attention_bounded(stab, q, k, vt, heads):
    b, s, qw = q.shape
    nk = k.shape[1]
    n_out = max(h[3] for h in heads) + 1
    tq = min(BOUNDED_TQ, s)
    tk = _pick_tk(nk)
    kern = functools.partial(_flash_bounded_kernel, heads=heads, n_out=n_out)
    return pl.pallas_call(
        kern,
        grid=(b, s // tq, nk // tk),
        in_specs=[pl.BlockSpec(memory_space=pltpu.SMEM),
                  pl.BlockSpec((1, tq, qw), lambda i, j, t: (i, j, 0)),
                  pl.BlockSpec((1, tk, k.shape[2]), lambda i, j, t: (i, t, 0)),
                  pl.BlockSpec((1, vt.shape[1], VT_ROWS, tk), lambda i, j, t: (i, 0, 0, t))],
        out_specs=pl.BlockSpec((1, tq, n_out * LANES), lambda i, j, t: (i, j, 0)),
        out_shape=jax.ShapeDtypeStruct((b, s, n_out * LANES), F32),
        scratch_shapes=[pltpu.VMEM((len(heads), VT_ROWS, tq), F32)],
        compiler_params=_cparams(("parallel", "parallel", "arbitrary")),
        name="flash_attention_bounded",
    )(stab, q, k, vt)


def attention(q2, k2, v2, q, k, vt, heads):
    bound = BOUND_SLACK * jnp.sqrt(q2 * k2)
    stab = (bound - STAB_SHIFT).reshape(-1, 1).astype(F32)
    sum_log2 = STAB_SHIFT + math.log2(k.shape[1]) + 0.5 * jnp.log2(jnp.maximum(jnp.max(v2), 1.0))
    return lax.cond((jnp.max(bound) <= BOUND_LIMIT) & (sum_log2 <= SUM_LOG2_LIMIT),
                    lambda: flash_attention_bounded(stab, q, k, vt, heads),
                    lambda: flash_attention(q, k, vt, heads))


HEADS_PAIRED = tuple((h, h // 2, h // 2, h // 2) for h in range(4))
HEADS_MLA = tuple((h, h, h // 2, h // 2) for h in range(4))
HEADS_GQA = tuple((h, 0, 0, h // 2) for h in range(4))
HEADS_DIFF = tuple((j, j // 4, j // 4, (j % 2) * 2 + j // 4) for j in range(8))


NA_QROWS = 4
NA_TQ = NA_QROWS * GRID_W


def _na_kernel(q_ref, kp_ref, kc_ref, kn_ref, vp_ref, vc_ref, vn_ref, kx_ref, vx_ref, bias_ref, o_ref, *, rows):
    j = pl.program_id(1)
    kh = min(NA_WIN_H, rows)
    qi = lax.broadcasted_iota(jnp.int32, (NA_TQ, 1), 0)
    ki = lax.broadcasted_iota(jnp.int32, (1, NA_TQ), 1)
    wshift = GRID_W.bit_length() - 1
    r = j * NA_QROWS + (qi >> wshift)
    c = qi & (GRID_W - 1)
    r0 = jnp.clip(r - kh // 2, 0, rows - kh)
    c0 = jnp.clip(c - NA_WIN_W // 2, 0, GRID_W - NA_WIN_W)
    kcol = ki & (GRID_W - 1)
    col_ok = (kcol >= c0) & (kcol < c0 + NA_WIN_W)
    masks = []
    for d in (-1, 0, 1):
        kr = (j + d) * NA_QROWS + (ki >> wshift)
        masks.append(col_ok & (kr >= r0) & (kr < r0 + kh))
    lo_half = lax.broadcasted_iota(jnp.int32, (1, LANES), 1) < 64
    for o in range(2):
        k_loc = [kp_ref[0, :, o * LANES:(o + 1) * LANES], kc_ref[0, :, o * LANES:(o + 1) * LANES],
                 kn_ref[0, :, o * LANES:(o + 1) * LANES]]
        v_all = jnp.concatenate([vp_ref[0, :, o * LANES:(o + 1) * LANES], vc_ref[0, :, o * LANES:(o + 1) * LANES],
                                 vn_ref[0, :, o * LANES:(o + 1) * LANES], vx_ref[0, :, o * LANES:(o + 1) * LANES]],
                                axis=0)
        kx = kx_ref[0, :, o * LANES:(o + 1) * LANES]
        res = []
        for h in (2 * o, 2 * o + 1):
            q = q_ref[0, :, h * LANES:(h + 1) * LANES]
            parts = [jnp.where(masks[d], _dot_nt(q, k_loc[d]) + bias_ref[h, d], NEG_BIG) for d in range(3)]
            parts.append(_dot_nt(q, kx))
            s = jnp.concatenate(parts, axis=1)
            m = jnp.max(s, axis=-1, keepdims=True)
            p = jnp.exp2(s - m)
            l = jnp.sum(p, axis=-1, keepdims=True)
            res.append(_dot(p.astype(BF16), v_all) / l)
        o_ref[0, :, o * LANES:(o + 1) * LANES] = jnp.where(lo_half, res[0], res[1])


def na_attention(q, k, v, kx, vx, bias, rows):
    b, s, _ = q.shape
    nblk = rows // NA_QROWS
    nx = kx.shape[1]
    kern = functools.partial(_na_kernel, rows=rows)
    prev = lambda i, j: (i, jnp.maximum(j - 1, 0), 0)
    cur = lambda i, j: (i, j, 0)
    nxt = lambda i, j: (i, jnp.minimum(j + 1, nblk - 1), 0)
    kv_spec = lambda f: pl.BlockSpec((1, NA_TQ, 2 * LANES), f)
    return pl.pallas_call(
        kern,
        grid=(b, nblk),
        in_specs=[pl.BlockSpec((1, NA_TQ, 4 * LANES), cur),
                  kv_spec(prev), kv_spec(cur), kv_spec(nxt),
                  kv_spec(prev), kv_spec(cur), kv_spec(nxt),
                  pl.BlockSpec((1, nx, 2 * LANES), lambda i, j: (i, 0, 0)),
                  pl.BlockSpec((1, nx, 2 * LANES), lambda i, j: (i, 0, 0)),
                  pl.BlockSpec(bias.shape, lambda i, j: (0, 0, 0, 0))],
        out_specs=pl.BlockSpec((1, NA_TQ, 2 * LANES), cur),
        out_shape=jax.ShapeDtypeStruct((b, s, 2 * LANES), F32),
        compiler_params=_cparams(("parallel", "arbitrary")),
        name="na_attention",
    )(q, k, k, k, v, v, v, kx, vx, bias)


ROUTE_T = 256


def _outproj_kernel(x_ref, ona_ref, omla_ref, od_ref, ogqa_ref, mod_ref, gsub_ref, gffn_ref, lam_ref, mat_ref,
                    wout_ref, wr_ref, xo_ref, h_ref, lg_ref, *, lam_init):
    lam = lam_ref[0, 0]
    m64 = mat_ref[M_SEG64]
    pieces = [ona_ref[0].astype(BF16), omla_ref[0].astype(BF16)]
    dsl = []
    for i in range(2):
        d = od_ref[0, :, i * LANES:(i + 1) * LANES] - lam * od_ref[0, :, (2 + i) * LANES:(3 + i) * LANES]
        d = _seg_norm(d, m64, 1.0 / 64, gsub_ref[...]) * (1.0 - lam_init)
        dsl.append(d.astype(BF16))
    pieces += dsl + [ogqa_ref[0].astype(BF16)]
    o = jnp.concatenate(pieces, axis=1)
    y = _dot(o, wout_ref[...])
    x = x_ref[0] + mod_ref[0, 2:3, :] * y
    xo_ref[0] = x
    ms = jnp.mean(x * x, axis=-1, keepdims=True)
    h = x * lax.rsqrt(ms + NORM_EPS) * gffn_ref[...]
    h = h * (1.0 + mod_ref[0, 4:5, :]) + mod_ref[0, 3:4, :]
    h_ref[0] = h.astype(BF16)
    lg = _dot3_nt(wr_ref[...], h)
    for t in range(lg.shape[1] // ROUTE_T):
        lg_ref[0, t] = lg[:, t * ROUTE_T:(t + 1) * ROUTE_T]


def outproj_mod_router(x, o_na, o_mla, o_diff, o_gqa, mod, g_sub_t, g_ffn, lam, mats, w_out, w_router_t, lam_init):
    b, n, d = x.shape
    tm = min(512, n)
    nt = tm // ROUTE_T
    kern = functools.partial(_outproj_kernel, lam_init=lam_init)
    tok = lambda w: pl.BlockSpec((1, tm, w), lambda i, j: (i, j, 0))
    c2 = lambda i, j: (0, 0)
    return pl.pallas_call(
        kern,
        grid=(b, n // tm),
        in_specs=[tok(d), tok(256), tok(256), tok(512), tok(256),
                  pl.BlockSpec((1, 8, d), lambda i, j: (i, 0, 0)),
                  pl.BlockSpec((1, LANES), c2), pl.BlockSpec((1, d), c2),
                  pl.BlockSpec(memory_space=pltpu.SMEM),
                  pl.BlockSpec(mats.shape, lambda i, j: (0, 0, 0)),
                  pl.BlockSpec((d, d), c2), pl.BlockSpec((N_EXPERTS, d), c2)],
        out_specs=[tok(d), tok(d),
                   pl.BlockSpec((1, nt, N_EXPERTS, ROUTE_T), lambda i, j: (i, j, 0, 0))],
        out_shape=[jax.ShapeDtypeStruct((b, n, d), F32), jax.ShapeDtypeStruct((b, n, d), BF16),
                   jax.ShapeDtypeStruct((b, n // ROUTE_T, N_EXPERTS, ROUTE_T), F32)],
        compiler_params=_cparams(("parallel", "arbitrary")),
        name="outproj_mod_router",
    )(x, o_na, o_mla, o_diff, o_gqa, mod, g_sub_t, g_ffn.reshape(1, d), lam, mats, w_out, w_router_t)


def _route_kernel(lg_ref, tri_ref, aff_ref, pos_ref, off_ref, *, cap):
    nb = lg_ref.shape[1]
    lg = lg_ref[0]
    mx = jnp.max(lg, axis=1, keepdims=True)
    ex = jnp.exp(lg - mx)
    aff = ex / jnp.sum(ex, axis=1, keepdims=True)
    aff_ref[0] = aff
    bits = lax.bitcast_convert_type(aff, jnp.int32)

    def count_ge(t):
        hit = jnp.where(bits >= t[None], 1.0, 0.0)
        return jnp.sum(jnp.sum(hit, axis=0), axis=1, keepdims=True)

    def bis(i, t):
        cand = t | (jnp.int32(1) << (30 - i))
        return jnp.where(count_ge(cand) >= float(cap), cand, t)

    thr = lax.fori_loop(0, 31, bis, jnp.zeros((N_EXPERTS, 1), jnp.int32))
    need = float(cap) - count_ge(thr + 1)
    tri = tri_ref[...]

    def blk(jb, carry):
        c_eq, c_pos = carry
        bb = lax.bitcast_convert_type(aff_ref[0, jb], jnp.int32)
        gt = bb > thr
        eq = bb == thr
        eq_before = _dot(jnp.where(eq, 1.0, 0.0).astype(BF16), tri) + c_eq
        sel = gt | (eq & (eq_before < need))
        sel_f = jnp.where(sel, 1.0, 0.0)
        before = _dot(sel_f.astype(BF16), tri) + c_pos
        pos_ref[0, jb] = jnp.where(sel, before, -1.0).astype(jnp.int32)
        off_ref[0, jb] = jnp.broadcast_to(c_pos, (N_EXPERTS, LANES)).astype(jnp.int32)
        c_eq = c_eq + jnp.sum(jnp.where(eq, 1.0, 0.0), axis=1, keepdims=True)
        c_pos = c_pos + jnp.sum(sel_f, axis=1, keepdims=True)
        return c_eq, c_pos

    zero = jnp.zeros((N_EXPERTS, 1), F32)
    lax.fori_loop(0, nb, blk, (zero, zero))


def route(logits, tri, cap):
    b, nb, e, t = logits.shape
    kern = functools.partial(_route_kernel, cap=cap)
    spec = pl.BlockSpec((1, nb, e, t), lambda i: (i, 0, 0, 0))
    return pl.pallas_call(
        kern,
        grid=(b,),
        in_specs=[spec, pl.BlockSpec(tri.shape, lambda i: (0, 0))],
        out_specs=[spec, spec, pl.BlockSpec((1, nb, e, LANES), lambda i: (i, 0, 0, 0))],
        out_shape=[jax.ShapeDtypeStruct((b, nb, e, t), F32), jax.ShapeDtypeStruct((b, nb, e, t), jnp.int32),
                   jax.ShapeDtypeStruct((b, nb, e, LANES), jnp.int32)],
        compiler_params=_cparams(("arbitrary",)),
        name="route",
    )(logits, tri)


SLOT_ALIGN = 16


def _for_each_slot_window(off_ref, idx0, nsub, st, cap, fn):
    wins = []
    for jj in range(nsub):
        off, nxt = off_ref[idx0 + jj], off_ref[idx0 + jj + 1]
        start = pl.multiple_of(jnp.minimum(off // SLOT_ALIGN * SLOT_ALIGN, cap - st), SLOT_ALIGN)
        fn(jj, start, None)
        wins.append((start, nxt > start + st))

    @pl.when(functools.reduce(jnp.logical_or, [spill for _, spill in wins]))
    def _():
        for jj, (start, spill) in enumerate(wins):
            @pl.when(spill)
            def _():
                fn(jj, pl.multiple_of(jnp.minimum(start + st, cap - st), SLOT_ALIGN), start + st)


def _slot_one_hot(slot, prow, start, lower):
    hit = slot == (prow - start)
    return hit if lower is None else hit & (prow >= lower)


def _gather_kernel(off_ref, h_ref, pos_ref, aff_ref, xe_ref, gs_ref, acc_ref, gacc_ref, *, st, cap, nsub, nb):
    bi, e, ch = pl.program_id(0), pl.program_id(1), pl.program_id(2)

    @pl.when(ch == 0)
    def _():
        acc_ref[...] = jnp.zeros(acc_ref.shape, F32)
        gacc_ref[...] = jnp.zeros(gacc_ref.shape, F32)

    slot = lax.broadcasted_iota(jnp.int32, (st, ROUTE_T), 0)

    def add(jj, start, lower):
        hit = _slot_one_hot(slot, pos_ref[0, jj, pl.ds(e, 1), :], start, lower)
        acc_ref[pl.ds(start, st), :] += _dot(jnp.where(hit, 1.0, 0.0).astype(BF16),
                                             h_ref[0, jj * ROUTE_T:(jj + 1) * ROUTE_T, :])
        gacc_ref[pl.ds(start, st), :] += jnp.sum(jnp.where(hit, aff_ref[0, jj, pl.ds(e, 1), :], 0.0),
                                                 axis=1, keepdims=True)

    _for_each_slot_window(off_ref, (bi * N_EXPERTS + e) * (nb + 1) + ch * nsub, nsub, st, cap, add)

    @pl.when(ch == pl.num_programs(2) - 1)
    def _():
        xe_ref[0, 0] = acc_ref[...].astype(BF16)
        gs_ref[0, 0] = gacc_ref[...]


def moe_gather(offs, h, pos, aff, cap):
    b, n, d = h.shape
    nb = n // ROUTE_T
    st = min(ROUTE_T, cap)
    chunk = min(2048, n)
    nsub = chunk // ROUTE_T
    kern = functools.partial(_gather_kernel, st=st, cap=cap, nsub=nsub, nb=nb)
    rspec = pl.BlockSpec((1, nsub, N_EXPERTS, ROUTE_T), lambda i, e, c, off: (i, c, 0, 0))
    return pl.pallas_call(
        kern,
        grid_spec=pltpu.PrefetchScalarGridSpec(
            num_scalar_prefetch=1,
            grid=(b, N_EXPERTS, n // chunk),
            in_specs=[pl.BlockSpec((1, chunk, d), lambda i, e, c, off: (i, c, 0)), rspec, rspec],
            out_specs=[pl.BlockSpec((1, 1, cap, d), lambda i, e, c, off: (i, e, 0, 0)),
                       pl.BlockSpec((1, 1, cap, 1), lambda i, e, c, off: (i, e, 0, 0))],
            scratch_shapes=[pltpu.VMEM((cap, d), F32), pltpu.VMEM((cap, 1), F32)]),
        out_shape=[jax.ShapeDtypeStruct((b, N_EXPERTS, cap, d), BF16),
                   jax.ShapeDtypeStruct((b, N_EXPERTS, cap, 1), F32)],
        compiler_params=_cparams(("parallel", "parallel", "arbitrary")),
        name="moe_gather",
    )(offs, h, pos, aff)


FF_TILE = 256


def _ffn_kernel(x_ref, g_ref, wg_ref, wu_ref, wd_ref, y_ref, acc_ref):
    f = pl.program_id(2)

    @pl.when(f == 0)
    def _():
        acc_ref[...] = jnp.zeros(acc_ref.shape, F32)

    x = x_ref[0, 0]
    gate = _dot(x, wg_ref[0, 0].astype(BF16))
    up = _dot(x, wu_ref[0, 0].astype(BF16))
    hmid = (_silu(gate) * up).astype(BF16)
    acc_ref[...] += _dot(hmid, wd_ref[0, 0].astype(BF16))

    @pl.when(f == pl.num_programs(2) - 1)
    def _():
        y_ref[0, 0] = (acc_ref[...] * g_ref[0, 0]).astype(BF16)


def moe_ffn(xe, gs, w_gate, w_up, w_down, layer):
    b, e, cap, d = xe.shape
    dff = w_gate.shape[3]
    return pl.pallas_call(
        _ffn_kernel,
        grid=(e, b, dff // FF_TILE),
        in_specs=[pl.BlockSpec((1, 1, cap, d), lambda ei, bi, f: (bi, ei, 0, 0)),
                  pl.BlockSpec((1, 1, cap, 1), lambda ei, bi, f: (bi, ei, 0, 0)),
                  pl.BlockSpec((1, 1, d, FF_TILE), lambda ei, bi, f: (layer, ei, 0, f)),
                  pl.BlockSpec((1, 1, d, FF_TILE), lambda ei, bi, f: (layer, ei, 0, f)),
                  pl.BlockSpec((1, 1, FF_TILE, d), lambda ei, bi, f: (layer, ei, f, 0))],
        out_specs=pl.BlockSpec((1, 1, cap, d), lambda ei, bi, f: (bi, ei, 0, 0)),
        out_shape=jax.ShapeDtypeStruct((b, e, cap, d), BF16),
        scratch_shapes=[pltpu.VMEM((cap, d), F32)],
        compiler_params=_cparams(("parallel", "parallel", "arbitrary")),
        name="moe_ffn",
    )(xe, gs, w_gate, w_up, w_down)


def _combine_kernel(off_ref, x_ref, ye_ref, pos_ref, mod_ref, o_ref, *, st, cap, nsub, nb):
    bi, ch, e = pl.program_id(0), pl.program_id(1), pl.program_id(2)

    @pl.when(e == 0)
    def _():
        o_ref[...] = jnp.zeros(o_ref.shape, F32)

    slot = lax.broadcasted_iota(jnp.int32, (st, ROUTE_T), 0)

    def add(jj, start, lower):
        hit = _slot_one_hot(slot, pos_ref[0, jj, pl.ds(e, 1), :], start, lower)
        oh = jnp.where(hit, 1.0, 0.0).astype(BF16)
        o_ref[0, jj * ROUTE_T:(jj + 1) * ROUTE_T, :] += lax.dot_general(
            oh, ye_ref[0, 0, pl.ds(start, st), :], (((0,), (0,)), ((), ())), preferred_element_type=F32)

    _for_each_slot_window(off_ref, (bi * N_EXPERTS + e) * (nb + 1) + ch * nsub, nsub, st, cap, add)

    @pl.when(e == pl.num_programs(2) - 1)
    def _():
        o_ref[0] = x_ref[0] + mod_ref[0, 5:6, :] * o_ref[0]


def moe_combine(offs, x, ye, pos, mod, cap, slot0):
    b, n, d = x.shape
    nb = n // ROUTE_T
    st = min(ROUTE_T, cap)
    chunk = min(2048, n)
    nsub = chunk // ROUTE_T
    assert slot0 % cap == 0
    kern = functools.partial(_combine_kernel, st=st, cap=cap, nsub=nsub, nb=nb)
    return pl.pallas_call(
        kern,
        grid_spec=pltpu.PrefetchScalarGridSpec(
            num_scalar_prefetch=1,
            grid=(b, n // chunk, N_EXPERTS),
            in_specs=[pl.BlockSpec((1, chunk, d), lambda i, c, e, off: (i, c, 0)),
                      pl.BlockSpec((1, 1, cap, d), lambda i, c, e, off: (i, e, slot0 // cap, 0)),
                      pl.BlockSpec((1, nsub, N_EXPERTS, ROUTE_T), lambda i, c, e, off: (i, c, 0, 0)),
                      pl.BlockSpec((1, 8, d), lambda i, c, e, off: (i, 0, 0))],
            out_specs=pl.BlockSpec((1, chunk, d), lambda i, c, e, off: (i, c, 0))),
        out_shape=jax.ShapeDtypeStruct((b, n, d), F32),
        compiler_params=_cparams(("parallel", "parallel", "arbitrary")),
        name="moe_combine",
    )(offs, x, ye, pos, mod)


def expert_choice_ffn(sets, tri, w_gate, w_up, w_down, layer):
    routed = []
    for x, h, logits, mod in sets:
        b, n, _ = x.shape
        cap = EC_CAPACITY * n // N_EXPERTS
        aff, pos, off = route(logits, tri, cap)
        first = jnp.transpose(off[..., 0], (0, 2, 1))
        offs = jnp.concatenate([first, jnp.full((b, N_EXPERTS, 1), cap, jnp.int32)], axis=2).reshape(-1)
        xe, gs = moe_gather(offs, h, pos, aff, cap)
        routed.append((offs, pos, cap, xe, gs))
    xe_all = jnp.concatenate([r[3] for r in routed], axis=2) if len(routed) > 1 else routed[0][3]
    gs_all = jnp.concatenate([r[4] for r in routed], axis=2) if len(routed) > 1 else routed[0][4]
    ye = moe_ffn(xe_all, gs_all, w_gate, w_up, w_down, layer)
    outs, slot0 = [], 0
    for (x, _, _, mod), (offs, pos, cap, _, _) in zip(sets, routed):
        outs.append(moe_combine(offs, x, ye, pos, mod, cap, slot0))
        slot0 += cap
    return outs


def _rope_tables(n_rows_grid):
    s = n_rows_grid * GRID_W
    t = np.arange(s)
    row, col = (t // GRID_W).astype(np.float64), (t % GRID_W).astype(np.float64)

    def unit(n):
        half = n // 2
        inv = ROPE_THETA ** (-np.arange(0, n, 2, dtype=np.float64) / n)
        out = []
        for pos in (row, col):
            ang = pos[:, None] * inv[None, :]
            c, sn = np.cos(ang), np.sin(ang)
            z = np.zeros_like(sn)
            out.append((np.concatenate([c, c], 1), np.concatenate([-sn, z], 1), np.concatenate([z, sn], 1)))
        return [np.concatenate([out[0][i], out[1][i]], 1) for i in range(3)]

    ident = lambda w: (np.ones((s, w), np.float32), np.zeros((s, w), np.float32), np.zeros((s, w), np.float32))
    u16 = unit(16)
    u32 = unit(32)
    tabs = []
    idt = ident(64)
    idt32 = ident(32)
    for i in range(3):
        tabs.append(np.concatenate([idt[i], u16[i], idt32[i]], 1))
    for i in range(3):
        tabs.append(np.concatenate([u16[i]] * 4, 1))
    for i in range(3):
        tabs.append(np.concatenate([u32[i]] * 2, 1))
    return np.stack(tabs).astype(np.float32)


def _identity_tables(n):
    one, zero = np.ones((n, LANES), np.float32), np.zeros((n, LANES), np.float32)
    return np.stack([one, zero, zero] * 3)


def _seg_matrices():
    i = np.arange(LANES)
    ones = np.ones((LANES, LANES), np.float32)
    m64 = (i[:, None] // 64 == i[None, :] // 64).astype(np.float32)
    m32 = (i[:, None] // 32 == i[None, :] // 32).astype(np.float32)
    return np.stack([ones, m64, m32])


def _na_bias(rpb):
    c = np.arange(GRID_W)
    rl = np.arange(NA_QROWS)
    dc = np.clip(c[None, :] - c[:, None] + NA_WIN_W - 1, 0, 2 * NA_WIN_W - 2)
    d = np.array([-1, 0, 1])
    dr = np.clip(NA_QROWS * d[:, None, None] + rl[None, None, :] - rl[None, :, None] + NA_WIN_H - 1,
                 0, 2 * NA_WIN_H - 2)
    cols = jnp.take(rpb.astype(F32) * LOG2E, jnp.asarray(dc), axis=2)
    full = jnp.take(cols, jnp.asarray(dr), axis=1)
    return jnp.transpose(full, (0, 1, 2, 4, 3, 5)).reshape(rpb.shape[0], 3, NA_TQ, NA_TQ)


def _pad_cols(w, width):
    return jnp.pad(w, ((0, 0), (0, width - w.shape[1])))


def _layer_params(l, w_in, g_na_q, g_na_k, na_rpb, g_mla_cq, w_mla_uq, g_mla_q, g_mla_ckv, w_mla_ukv,
                  g_mla_k_nope, g_mla_k_rope, g_diff_q, g_diff_k, g_diff_sub, g_gqa_q, g_gqa_k, w_out):
    wi = w_in[l]
    (naq, nak, nav, cq, ckv, kr, dq, dk, dv, gq, gk, gv) = jnp.split(
        wi, np.cumsum([256, 256, 256, 256, 128, 32, 256, 256, 256, 256, 128])[:], axis=1)
    gq4 = gq.reshape(-1, 4, 64)[:, jnp.array([0, 2, 1, 3])].reshape(-1, 256)
    zeros = lambda w: jnp.zeros((wi.shape[0], w), wi.dtype)
    w_in_r = jnp.concatenate([naq, nak, nav, cq, ckv, dq, dk, dv, gq4, gk, gv, zeros(64), kr, zeros(32)],
                             axis=1).astype(BF16)
    uq = w_mla_uq[l].reshape(MLA_Q_RANK, 4, 96)
    wuq = jnp.pad(uq, ((0, 0), (0, 0), (0, 32))).reshape(MLA_Q_RANK, 512).astype(BF16)
    ukv = w_mla_ukv[l].reshape(MLA_KV_RANK, 4, 128)
    wuk = jnp.pad(ukv[:, :, :64], ((0, 0), (0, 0), (0, 64))).reshape(MLA_KV_RANK, 512).astype(BF16)
    wuv = ukv[:, :, 64:].reshape(MLA_KV_RANK, 256).astype(BF16)
    row = lambda v: jnp.pad(v, (0, 512 - v.shape[0]))
    z32, z64 = jnp.zeros((32,), F32), jnp.zeros((64,), F32)
    gains = jnp.stack([
        row(jnp.tile(g_na_q[l], 4)), row(jnp.tile(g_na_k[l], 4)), row(g_mla_cq[l]),
        row(jnp.tile(jnp.concatenate([g_mla_q[l], z32]), 4)), row(g_mla_ckv[l]),
        row(jnp.tile(jnp.concatenate([g_mla_k_nope[l], z64]), 4)),
        row(jnp.concatenate([z64, g_mla_k_rope[l], z32])),
        row(jnp.tile(g_diff_q[l].reshape(-1), 4)), row(jnp.tile(g_diff_k[l].reshape(-1), 4)),
        row(jnp.tile(g_gqa_q[l], 4)), row(jnp.tile(g_gqa_k[l], 2))] + [jnp.zeros((512,), F32)] * 5)
    wo = w_out[l]
    wo_g = wo[768:].reshape(4, 64, -1)[jnp.array([0, 2, 1, 3])].reshape(256, -1)
    w_out_r = jnp.concatenate([wo[:768], wo_g], axis=0).astype(BF16)
    return dict(w_in=w_in_r, wuq=wuq, wuk=wuk, wuv=wuv, gains=gains, w_out=w_out_r,
                bias=_na_bias(na_rpb[l]), g_sub=jnp.tile(g_diff_sub[l], 2).reshape(1, LANES))


def kernel(x, c, ctx, c_ctx, w_mod, b_mod, g_attn, g_ffn, w_in, g_na_q, g_na_k, na_rpb, g_mla_cq, w_mla_uq, g_mla_q, g_mla_ckv, w_mla_ukv, g_mla_k_nope, g_mla_k_rope, g_diff_q, g_diff_k, diff_lambda, g_diff_sub, g_gqa_q, g_gqa_k, w_out, w_router, w_gate, w_up, w_down):
    b, s, d = x.shape
    n_ctx = ctx.shape[1]
    rows = s // GRID_W
    tabs = jnp.asarray(_rope_tables(rows))
    tabs_ctx = jnp.asarray(_identity_tables(n_ctx))
    mats = jnp.asarray(_seg_matrices()).astype(BF16)
    tri = jnp.asarray(np.triu(np.ones((ROUTE_T, ROUTE_T), np.float32), 1)).astype(BF16)
    c_rows = jnp.concatenate([c, c_ctx[None], jnp.zeros((8 - b - 1, d), F32)], axis=0)
    xc = ctx
    for l in range(DEPTH):
        need_ctx = l < DEPTH - 1
        lam_init = 0.8 - 0.6 * math.exp(-0.3 * l)
        lp = diff_lambda[l].astype(F32)
        lam = (jnp.exp(jnp.sum(lp[0] * lp[1])) - jnp.exp(jnp.sum(lp[2] * lp[3])) + lam_init).reshape(1, 1)
        prm = _layer_params(l, w_in, g_na_q, g_na_k, na_rpb, g_mla_cq, w_mla_uq, g_mla_q, g_mla_ckv, w_mla_ukv,
                            g_mla_k_nope, g_mla_k_rope, g_diff_q, g_diff_k, g_diff_sub, g_gqa_q, g_gqa_k, w_out)
        modv = mod_vectors(c_rows, w_mod[l], b_mod[l]).reshape(8, N_MOD, d)
        mod = jnp.pad(modv[:b], ((0, 0), (0, 2), (0, 0)))
        mod_c = jnp.broadcast_to(jnp.pad(modv[b:b + 1], ((0, 0), (0, 2), (0, 0))), (b, 8, d))
        w_router_t = w_router[l].T

        def mix_inputs(xin, m, tb):
            return proj_prep(xin, g_attn[l], m[:, 0:1], m[:, 1:2], prm["w_in"], tb, prm["gains"], mats,
                             prm["wuq"], prm["wuk"], prm["wuv"])

        (naq, nak, nav, mq, mk, mv, dq, dk, dv, gq, gk, gv, stat) = mix_inputs(x, mod, tabs)
        (naq_c, nak_c, nav_c, mq_c, mk_c, mv_c, dq_c, dk_c, dv_c, gq_c, gk_c, gv_c, stat_c) = mix_inputs(
            xc, mod_c, tabs_ctx)
        cat = lambda a, bb: jnp.concatenate([a, bb], axis=1)
        n2 = jnp.max(stat, axis=(1, 3))
        n2c = jnp.max(stat_c, axis=(1, 3))
        q2 = lambda name: n2[:, STAT_ROWS.index(name)]
        k2 = lambda name: jnp.maximum(n2[:, STAT_ROWS.index(name)], n2c[:, STAT_ROWS.index(name)])

        ext = values_transposed
        o_na = na_attention(naq, nak, nav, nak_c, nav_c, prm["bias"], rows)
        o_mla = attention(q2("mq"), k2("mk"), k2("mv"), mq, cat(mk_c, mk), ext(cat(mv_c, mv)), HEADS_MLA)
        o_diff = attention(q2("dq"), k2("dk"), k2("dv"), dq, cat(dk_c, dk), ext(cat(dv_c, dv)), HEADS_DIFF)
        o_gqa = attention(q2("gq"), k2("gk"), k2("gv"), gq, cat(gk_c, gk), ext(cat(gv_c, gv)), HEADS_GQA)
        x_mid, h2, logits = outproj_mod_router(x, o_na, o_mla, o_diff, o_gqa, mod, prm["g_sub"], g_ffn[l], lam,
                                               mats, prm["w_out"], w_router_t, lam_init)
        if need_ctx:
            oc_na = flash_attention(naq_c, nak_c, ext(nav_c), HEADS_PAIRED)
            oc_mla = flash_attention(mq_c, mk_c, ext(mv_c), HEADS_MLA)
            oc_diff = flash_attention(dq_c, dk_c, ext(dv_c), HEADS_DIFF)
            oc_gqa = flash_attention(gq_c, gk_c, ext(gv_c), HEADS_GQA)
            xc_mid, hc2, logits_c = outproj_mod_router(xc, oc_na, oc_mla, oc_diff, oc_gqa, mod_c, prm["g_sub"],
                                                       g_ffn[l], lam, mats, prm["w_out"], w_router_t, lam_init)
            x, xc = expert_choice_ffn([(x_mid, h2, logits, mod), (xc_mid, hc2, logits_c, mod_c)],
                                      tri, w_gate, w_up, w_down, l)
        else:
            (x,) = expert_choice_ffn([(x_mid, h2, logits, mod)], tri, w_gate, w_up, w_down, l)
    return x
```

```python
import functools
import math

import numpy as np
import jax
import jax.numpy as jnp
from jax import lax
from jax.experimental import pallas as pl
from jax.experimental.pallas import tpu as pltpu

D_MODEL = 1024
GRID_W = 64
HEAD_DIM = 64
N_HEADS = 4
NA_WIN_H = 8
NA_WIN_W = 16
MLA_Q_RANK = 256
MLA_KV_RANK = 128
MLA_NOPE_DIM = 64
MLA_ROPE_DIM = 32
MLA_V_DIM = 64
DIFF_QK_DIM = 32
N_EXPERTS = 16
EC_CAPACITY = 2
D_FF = 2816
ROPE_THETA = 10000.0
NORM_EPS = 1e-6
N_MOD = 6
DEPTH = 2

LANES = 128
P_COLS = 2560
VMEM_LIMIT = 56 * 1024 * 1024
NEG_BIG = -1e30
LOG2E = math.log2(math.e)

F32 = jnp.float32
BF16 = jnp.bfloat16


def _cparams(sem):
    return pltpu.CompilerParams(dimension_semantics=sem, vmem_limit_bytes=VMEM_LIMIT)


def _split(a):
    hi = a.astype(BF16)
    lo = (a - hi.astype(F32)).astype(BF16)
    return hi, lo


def _dot(a, b):
    return jnp.dot(a, b, preferred_element_type=F32)


def _dot_nt(a, b):
    return lax.dot_general(a, b, (((1,), (1,)), ((), ())), preferred_element_type=F32)


def _dot3(a, b):
    ah, al = _split(a)
    bh, bl = _split(b)
    return _dot(ah, bh) + _dot(ah, bl) + _dot(al, bh)


def _dot3_nt(a, b):
    ah, al = _split(a)
    bh, bl = _split(b)
    return _dot_nt(ah, bh) + _dot_nt(ah, bl) + _dot_nt(al, bh)


def _silu(v):
    return v * jax.nn.sigmoid(v)


def _mod_kernel(c_ref, w_ref, b_ref, o_ref):
    o_ref[...] = _dot3(_silu(c_ref[...]), w_ref[...]) + b_ref[...]


def mod_vectors(c_rows, w_mod, b_mod):
    m, d = c_rows.shape
    n = w_mod.shape[1]
    tn = 1536
    return pl.pallas_call(
        _mod_kernel,
        grid=(n // tn,),
        in_specs=[pl.BlockSpec((m, d), lambda j: (0, 0)),
                  pl.BlockSpec((d, tn), lambda j: (0, j)),
                  pl.BlockSpec((1, tn), lambda j: (0, j))],
        out_specs=pl.BlockSpec((m, tn), lambda j: (0, j)),
        out_shape=jax.ShapeDtypeStruct((m, n), F32),
        compiler_params=_cparams(("arbitrary",)),
        name="mod_vectors",
    )(c_rows, w_mod, b_mod.reshape(1, n))


P_NAQ, P_NAK, P_NAV, P_CQ, P_CKV = 0, 256, 512, 768, 1024
P_DQ, P_DK, P_DV, P_GQ, P_GK, P_GV, P_KR = 1152, 1408, 1664, 1920, 2176, 2304, 2432
(G_NAQ, G_NAK, G_CQ, G_MQ, G_CKV, G_KN, G_KR, G_DQ, G_DK, G_GQ, G_GK) = range(11)
M_ONES, M_SEG64, M_SEG32 = 0, 1, 2
T_MLA, T_DIFF, T_GQA = 0, 3, 6


def _seg_norm(x, mat, inv_n, g):
    sq = x * x
    hi, lo = _split(sq)
    ms = (_dot(hi, mat) + _dot(lo, mat)) * inv_n
    return x * lax.rsqrt(ms + NORM_EPS) * g


def _rope(x, tab_ref, t0, half):
    c, s1, s2 = tab_ref[t0], tab_ref[t0 + 1], tab_ref[t0 + 2]
    return (x * c + pltpu.roll(x, LANES - half, axis=1) * s1
            + pltpu.roll(x, half, axis=1) * s2)


def _prep_kernel(x_ref, g_ref, sh_ref, sc_ref, w_ref, tab_ref, gain_ref, mat_ref, wuq_ref, wuk_ref, wuv_ref,
                 naq_ref, nak_ref, nav_ref, mq_ref, mk_ref, mv_ref,
                 dq_ref, dk_ref, dv_ref, gq_ref, gk_ref, gv_ref, st_ref):
    lane = lax.broadcasted_iota(jnp.int32, (1, LANES), 1)
    lo_half = lane < 64
    ones_m, m64, m32 = mat_ref[M_ONES], mat_ref[M_SEG64], mat_ref[M_SEG32]
    norm2 = {}

    x = x_ref[0]
    y = x * lax.rsqrt(jnp.mean(x * x, axis=-1, keepdims=True) + NORM_EPS) * g_ref[...]
    p = _dot((y * (1.0 + sc_ref[0]) + sh_ref[0]).astype(BF16), w_ref[...])

    def slab(off, i):
        return p[:, off + i * LANES: off + (i + 1) * LANES]

    def track(name, xb, mat):
        xf = xb.astype(F32)
        hi, lo = _split(xf * xf)
        cur = jnp.max(_dot(hi, mat) + _dot(lo, mat), axis=0, keepdims=True)
        norm2[name] = jnp.maximum(norm2[name], cur) if name in norm2 else cur
        return xb

    def track_elem(name, xb):
        for i in range(xb.shape[1] // LANES):
            xf = xb[:, i * LANES:(i + 1) * LANES].astype(F32)
            cur = jnp.max(xf * xf, axis=0, keepdims=True)
            norm2[name] = jnp.maximum(norm2[name], cur) if name in norm2 else cur
        return xb

    def gain(row, i):
        return gain_ref[row:row + 1, i * LANES:(i + 1) * LANES]

    s_na = HEAD_DIM ** -0.5 * LOG2E
    for i in range(2):
        q = _seg_norm(slab(P_NAQ, i), m64, 1.0 / 64, gain(G_NAQ, i)) * s_na
        naq_ref[0, :, (2 * i) * LANES:(2 * i + 1) * LANES] = jnp.where(lo_half, q, 0.0).astype(BF16)
        naq_ref[0, :, (2 * i + 1) * LANES:(2 * i + 2) * LANES] = jnp.where(lo_half, 0.0, q).astype(BF16)
        k = _seg_norm(slab(P_NAK, i), m64, 1.0 / 64, gain(G_NAK, i))
        nak_ref[0, :, i * LANES:(i + 1) * LANES] = k.astype(BF16)
        nav_ref[0, :, i * LANES:(i + 1) * LANES] = slab(P_NAV, i).astype(BF16)

    cq = p[:, P_CQ:P_CQ + MLA_Q_RANK]
    cq = cq * lax.rsqrt(jnp.mean(cq * cq, axis=-1, keepdims=True) + NORM_EPS) * gain_ref[G_CQ:G_CQ + 1, :MLA_Q_RANK]
    uq = _dot(cq.astype(BF16), wuq_ref[...])
    s_mla = (MLA_NOPE_DIM + MLA_ROPE_DIM) ** -0.5 * LOG2E
    ckv = p[:, P_CKV:P_CKV + MLA_KV_RANK]
    ckv = ckv * lax.rsqrt(jnp.mean(ckv * ckv, axis=-1, keepdims=True) + NORM_EPS) * gain_ref[G_CKV:G_CKV + 1, :MLA_KV_RANK]
    ckv_b = ckv.astype(BF16)
    uk = _dot(ckv_b, wuk_ref[...])
    mv_ref[0] = track_elem("mv", _dot(ckv_b, wuv_ref[...]).astype(BF16))
    kr = _seg_norm(slab(P_KR, 0), ones_m, 1.0 / MLA_ROPE_DIM, gain(G_KR, 0))
    kr = _rope(kr, tab_ref, T_MLA, 8)
    for h in range(N_HEADS):
        q = _seg_norm(uq[:, h * LANES:(h + 1) * LANES], ones_m, 1.0 / (MLA_NOPE_DIM + MLA_ROPE_DIM), gain(G_MQ, h))
        q = _rope(q, tab_ref, T_MLA, 8) * s_mla
        mq_ref[0, :, h * LANES:(h + 1) * LANES] = track("mq", q.astype(BF16), ones_m)
        kn = _seg_norm(uk[:, h * LANES:(h + 1) * LANES], ones_m, 1.0 / MLA_NOPE_DIM, gain(G_KN, h))
        mk_ref[0, :, h * LANES:(h + 1) * LANES] = track("mk", (kn + kr).astype(BF16), ones_m)

    s_d = DIFF_QK_DIM ** -0.5 * LOG2E
    seg = lane >> 5
    for i in range(2):
        q = _seg_norm(slab(P_DQ, i), m32, 1.0 / 32, gain(G_DQ, i))
        q = _rope(q, tab_ref, T_DIFF, 8) * s_d
        track("dq", q.astype(BF16), m32)
        for j in range(4):
            dq_ref[0, :, (4 * i + j) * LANES:(4 * i + j + 1) * LANES] = jnp.where(seg == j, q, 0.0).astype(BF16)
        k = _seg_norm(slab(P_DK, i), m32, 1.0 / 32, gain(G_DK, i))
        dk_ref[0, :, i * LANES:(i + 1) * LANES] = track("dk", _rope(k, tab_ref, T_DIFF, 8).astype(BF16), m32)
        dv_ref[0, :, i * LANES:(i + 1) * LANES] = track_elem("dv", slab(P_DV, i).astype(BF16))

    s_g = HEAD_DIM ** -0.5 * LOG2E
    for i in range(2):
        q = _seg_norm(slab(P_GQ, i), m64, 1.0 / 64, gain(G_GQ, i))
        q = _rope(q, tab_ref, T_GQA, 16) * s_g
        track("gq", q.astype(BF16), m64)
        gq_ref[0, :, (2 * i) * LANES:(2 * i + 1) * LANES] = jnp.where(lo_half, q, 0.0).astype(BF16)
        gq_ref[0, :, (2 * i + 1) * LANES:(2 * i + 2) * LANES] = jnp.where(lo_half, 0.0, q).astype(BF16)
    k = _seg_norm(slab(P_GK, 0), m64, 1.0 / 64, gain(G_GK, 0))
    gk_ref[0] = track("gk", _rope(k, tab_ref, T_GQA, 16).astype(BF16), m64)
    gv_ref[0] = track_elem("gv", slab(P_GV, 0).astype(BF16))
    st_ref[0, 0] = jnp.concatenate([norm2[n] for n in STAT_ROWS]
                                   + [jnp.zeros((STAT_PAD - len(STAT_ROWS), LANES), F32)], axis=0)


PREP_WIDTHS = (512, 256, 256, 512, 512, 256, 1024, 256, 256, 512, 128, 128)
STAT_ROWS = ("mq", "mk", "mv", "dq", "dk", "dv", "gq", "gk", "gv")
STAT_PAD = 16


def proj_prep(x, g, shift, scale, w_in, tabs, gains, mats, wuq, wuk, wuv):
    b, n, d = x.shape
    tm = min(512, n)
    const2 = lambda i, j: (0, 0)
    stat_spec = pl.BlockSpec((1, 1, STAT_PAD, LANES), lambda i, j: (i, j, 0, 0))
    stat_shape = jax.ShapeDtypeStruct((b, n // tm, STAT_PAD, LANES), F32)
    return pl.pallas_call(
        _prep_kernel,
        grid=(b, n // tm),
        in_specs=[pl.BlockSpec((1, tm, d), lambda i, j: (i, j, 0)),
                  pl.BlockSpec((1, d), const2),
                  pl.BlockSpec((1, 1, d), lambda i, j: (i, 0, 0)),
                  pl.BlockSpec((1, 1, d), lambda i, j: (i, 0, 0)),
                  pl.BlockSpec((d, P_COLS), const2),
                  pl.BlockSpec((9, tm, LANES), lambda i, j: (0, j, 0)),
                  pl.BlockSpec(gains.shape, const2),
                  pl.BlockSpec(mats.shape, lambda i, j: (0, 0, 0)),
                  pl.BlockSpec(wuq.shape, const2),
                  pl.BlockSpec(wuk.shape, const2),
                  pl.BlockSpec(wuv.shape, const2)],
        out_specs=[pl.BlockSpec((1, tm, w), lambda i, j: (i, j, 0)) for w in PREP_WIDTHS] + [stat_spec],
        out_shape=[jax.ShapeDtypeStruct((b, n, w), BF16) for w in PREP_WIDTHS] + [stat_shape],
        compiler_params=_cparams(("parallel", "arbitrary")),
        name="proj_prep",
    )(x, g.reshape(1, d), shift, scale, w_in, tabs, gains, mats, wuq, wuk, wuv)


VT_ROWS = LANES + 16


def _value_rows(heads, h):
    pair = [g for g in range(len(heads)) if heads[g][3] == heads[h][3]]
    return pair.index(h) * 64, pair.index(h) * 64 + 64
KEY_CHUNK = 256
FLASH_TQ = 512
FLASH_TK_MAX = 3584


def _flash_kernel(q_ref, k_ref, vt_ref, o_ref, m_ref, l_ref, acc_ref, *, heads, n_out):
    ki = pl.program_id(2)

    @pl.when(ki == 0)
    def _():
        m_ref[...] = jnp.full(m_ref.shape, NEG_BIG, F32)
        l_ref[...] = jnp.zeros(l_ref.shape, F32)
        acc_ref[...] = jnp.zeros(acc_ref.shape, F32)

    for o in range(n_out):
        pair = [h for h in range(len(heads)) if heads[h][3] == o]
        for pos, h in enumerate(pair):
            qs, ks, vs, _ = heads[h]
            q = q_ref[0, :, qs * LANES:(qs + 1) * LANES]
            k = k_ref[0, :, ks * LANES:(ks + 1) * LANES]
            st = _dot_nt(k, q)
            lo, hi = _value_rows(heads, h)
            m, l, acc = m_ref[h], l_ref[h], acc_ref[o, lo:hi, :]
            for c0 in range(0, st.shape[0], KEY_CHUNK):
                rows = slice(c0, min(c0 + KEY_CHUNK, st.shape[0]))
                sc = st[rows]
                m_new = jnp.maximum(m, jnp.max(sc, axis=0, keepdims=True))
                a = jnp.exp2(m - m_new)
                pt = jnp.exp2(sc - m_new).astype(BF16)
                r = _dot(vt_ref[0, vs, :, rows], pt)
                l = a * l + r[LANES:LANES + 1]
                acc = acc * a + r[lo:hi]
                m = m_new
            m_ref[h], l_ref[h] = m, l
            acc_ref[o, lo:hi, :] = acc

    @pl.when(ki == pl.num_programs(2) - 1)
    def _():
        for o in range(n_out):
            pair = [h for h in range(len(heads)) if heads[h][3] == o]
            out_t = jnp.concatenate([acc_ref[o, 0:64, :] / l_ref[pair[0]],
                                     acc_ref[o, 64:128, :] / l_ref[pair[1]]], axis=0)
            o_ref[0, :, o * LANES:(o + 1) * LANES] = out_t.T


def _pick_tk(n_keys):
    best = 256
    for t in range(256, FLASH_TK_MAX + 1, 256):
        if n_keys % t == 0:
            best = t
    return best


def values_transposed(v):
    b, n, w = v.shape
    vt = jnp.transpose(v.reshape(b, n, w // LANES, LANES), (0, 2, 3, 1))
    return jnp.concatenate([vt, jnp.ones((b, w // LANES, VT_ROWS - LANES, n), v.dtype)], axis=2)


def flash_attention(q, k, vt, heads):
    b, s, qw = q.shape
    nk = k.shape[1]
    n_out = max(h[3] for h in heads) + 1
    tq = min(FLASH_TQ, s)
    tk = _pick_tk(nk)
    kern = functools.partial(_flash_kernel, heads=heads, n_out=n_out)
    return pl.pallas_call(
        kern,
        grid=(b, s // tq, nk // tk),
        in_specs=[pl.BlockSpec((1, tq, qw), lambda i, j, t: (i, j, 0)),
                  pl.BlockSpec((1, tk, k.shape[2]), lambda i, j, t: (i, t, 0)),
                  pl.BlockSpec((1, vt.shape[1], VT_ROWS, tk), lambda i, j, t: (i, 0, 0, t))],
        out_specs=pl.BlockSpec((1, tq, n_out * LANES), lambda i, j, t: (i, j, 0)),
        out_shape=jax.ShapeDtypeStruct((b, s, n_out * LANES), F32),
        scratch_shapes=[pltpu.VMEM((len(heads), 1, tq), F32),
                        pltpu.VMEM((len(heads), 1, tq), F32),
                        pltpu.VMEM((n_out, LANES, tq), F32)],
        compiler_params=_cparams(("parallel", "parallel", "arbitrary")),
        name="flash_attention",
    )(q, k, vt)


STAB_SHIFT = 100.0
BOUND_LIMIT = 113.0
BOUND_SLACK = 1.01
SUM_LOG2_LIMIT = 126.0
BOUNDED_TQ = 1024


def _flash_bounded_kernel(stab_ref, q_ref, k_ref, vt_ref, o_ref, acc_ref, *, heads, n_out):
    ki = pl.program_id(2)

    @pl.when(ki == 0)
    def _():
        acc_ref[...] = jnp.zeros(acc_ref.shape, F32)

    stab = stab_ref[pl.program_id(0), 0]
    for h, (qs, ks, vs, _) in enumerate(heads):
        q = q_ref[0, :, qs * LANES:(qs + 1) * LANES]
        k = k_ref[0, :, ks * LANES:(ks + 1) * LANES]
        pt = jnp.exp2(_dot_nt(k, q) - stab).astype(BF16)
        acc_ref[h] += _dot(vt_ref[0, vs], pt)

    @pl.when(ki == pl.num_programs(2) - 1)
    def _():
        for o in range(n_out):
            pair = [h for h in range(len(heads)) if heads[h][3] == o]
            parts = []
            for h in pair:
                lo, hi = _value_rows(heads, h)
                parts.append(acc_ref[h, lo:hi, :] / acc_ref[h, LANES:LANES + 1, :])
            o_ref[0, :, o * LANES:(o + 1) * LANES] = jnp.concatenate(parts, axis=0).T


def flash_attention_bounded(stab, q, k, vt, heads):
    b, s, qw = q.shape
    nk = k.shape[1]
    n_out = max(h[3] for h in heads) + 1
    tq = min(BOUNDED_TQ, s)
    tk = _pick_tk(nk)
    kern = functools.partial(_flash_bounded_kernel, heads=heads, n_out=n_out)
    return pl.pallas_call(
        kern,
        grid=(b, s // tq, nk // tk),
        in_specs=[pl.BlockSpec(memory_space=pltpu.SMEM),
                  pl.BlockSpec((1, tq, qw), lambda i, j, t: (i, j, 0)),
                  pl.BlockSpec((1, tk, k.shape[2]), lambda i, j, t: (i, t, 0)),
                  pl.BlockSpec((1, vt.shape[1], VT_ROWS, tk), lambda i, j, t: (i, 0, 0, t))],
        out_specs=pl.BlockSpec((1, tq, n_out * LANES), lambda i, j, t: (i, j, 0)),
        out_shape=jax.ShapeDtypeStruct((b, s, n_out * LANES), F32),
        scratch_shapes=[pltpu.VMEM((len(heads), VT_ROWS, tq), F32)],
        compiler_params=_cparams(("parallel", "parallel", "arbitrary")),
        name="flash_attention_bounded",
    )(stab, q, k, vt)


def attention(q2, k2, v2, q, k, vt, heads):
    bound = BOUND_SLACK * jnp.sqrt(q2 * k2)
    stab = (bound - STAB_SHIFT).reshape(-1, 1).astype(F32)
    sum_log2 = STAB_SHIFT + math.log2(k.shape[1]) + 0.5 * jnp.log2(jnp.maximum(jnp.max(v2), 1.0))
    return lax.cond((jnp.max(bound) <= BOUND_LIMIT) & (sum_log2 <= SUM_LOG2_LIMIT),
                    lambda: flash_attention_bounded(stab, q, k, vt, heads),
                    lambda: flash_attention(q, k, vt, heads))


HEADS_PAIRED = tuple((h, h // 2, h // 2, h // 2) for h in range(4))
HEADS_MLA = tuple((h, h, h // 2, h // 2) for h in range(4))
HEADS_GQA = tuple((h, 0, 0, h // 2) for h in range(4))
HEADS_DIFF = tuple((j, j // 4, j // 4, (j % 2) * 2 + j // 4) for j in range(8))


NA_QROWS = 4
NA_TQ = NA_QROWS * GRID_W


def _na_kernel(q_ref, kp_ref, kc_ref, kn_ref, vp_ref, vc_ref, vn_ref, kx_ref, vx_ref, bias_ref, o_ref, *, rows):
    j = pl.program_id(1)
    kh = min(NA_WIN_H, rows)
    qi = lax.broadcasted_iota(jnp.int32, (NA_TQ, 1), 0)
    ki = lax.broadcasted_iota(jnp.int32, (1, NA_TQ), 1)
    wshift = GRID_W.bit_length() - 1
    r = j * NA_QROWS + (qi >> wshift)
    c = qi & (GRID_W - 1)
    r0 = jnp.clip(r - kh // 2, 0, rows - kh)
    c0 = jnp.clip(c - NA_WIN_W // 2, 0, GRID_W - NA_WIN_W)
    kcol = ki & (GRID_W - 1)
    col_ok = (kcol >= c0) & (kcol < c0 + NA_WIN_W)
    masks = []
    for d in (-1, 0, 1):
        kr = (j + d) * NA_QROWS + (ki >> wshift)
        masks.append(col_ok & (kr >= r0) & (kr < r0 + kh))
    lo_half = lax.broadcasted_iota(jnp.int32, (1, LANES), 1) < 64
    for o in range(2):
        k_loc = [kp_ref[0, :, o * LANES:(o + 1) * LANES], kc_ref[0, :, o * LANES:(o + 1) * LANES],
                 kn_ref[0, :, o * LANES:(o + 1) * LANES]]
        v_all = jnp.concatenate([vp_ref[0, :, o * LANES:(o + 1) * LANES], vc_ref[0, :, o * LANES:(o + 1) * LANES],
                                 vn_ref[0, :, o * LANES:(o + 1) * LANES], vx_ref[0, :, o * LANES:(o + 1) * LANES]],
                                axis=0)
        kx = kx_ref[0, :, o * LANES:(o + 1) * LANES]
        res = []
        for h in (2 * o, 2 * o + 1):
            q = q_ref[0, :, h * LANES:(h + 1) * LANES]
            parts = [jnp.where(masks[d], _dot_nt(q, k_loc[d]) + bias_ref[h, d], NEG_BIG) for d in range(3)]
            parts.append(_dot_nt(q, kx))
            s = jnp.concatenate(parts, axis=1)
            m = jnp.max(s, axis=-1, keepdims=True)
            p = jnp.exp2(s - m)
            l = jnp.sum(p, axis=-1, keepdims=True)
            res.append(_dot(p.astype(BF16), v_all) / l)
        o_ref[0, :, o * LANES:(o + 1) * LANES] = jnp.where(lo_half, res[0], res[1])


def na_attention(q, k, v, kx, vx, bias, rows):
    b, s, _ = q.shape
    nblk = rows // NA_QROWS
    nx = kx.shape[1]
    kern = functools.partial(_na_kernel, rows=rows)
    prev = lambda i, j: (i, jnp.maximum(j - 1, 0), 0)
    cur = lambda i, j: (i, j, 0)
    nxt = lambda i, j: (i, jnp.minimum(j + 1, nblk - 1), 0)
    kv_spec = lambda f: pl.BlockSpec((1, NA_TQ, 2 * LANES), f)
    return pl.pallas_call(
        kern,
        grid=(b, nblk),
        in_specs=[pl.BlockSpec((1, NA_TQ, 4 * LANES), cur),
                  kv_spec(prev), kv_spec(cur), kv_spec(nxt),
                  kv_spec(prev), kv_spec(cur), kv_spec(nxt),
                  pl.BlockSpec((1, nx, 2 * LANES), lambda i, j: (i, 0, 0)),
                  pl.BlockSpec((1, nx, 2 * LANES), lambda i, j: (i, 0, 0)),
                  pl.BlockSpec(bias.shape, lambda i, j: (0, 0, 0, 0))],
        out_specs=pl.BlockSpec((1, NA_TQ, 2 * LANES), cur),
        out_shape=jax.ShapeDtypeStruct((b, s, 2 * LANES), F32),
        compiler_params=_cparams(("parallel", "arbitrary")),
        name="na_attention",
    )(q, k, k, k, v, v, v, kx, vx, bias)


ROUTE_T = 256


def _outproj_kernel(x_ref, ona_ref, omla_ref, od_ref, ogqa_ref, mod_ref, gsub_ref, gffn_ref, lam_ref, mat_ref,
                    wout_ref, wr_ref, xo_ref, h_ref, lg_ref, *, lam_init):
    lam = lam_ref[0, 0]
    m64 = mat_ref[M_SEG64]
    pieces = [ona_ref[0].astype(BF16), omla_ref[0].astype(BF16)]
    dsl = []
    for i in range(2):
        d = od_ref[0, :, i * LANES:(i + 1) * LANES] - lam * od_ref[0, :, (2 + i) * LANES:(3 + i) * LANES]
        d = _seg_norm(d, m64, 1.0 / 64, gsub_ref[...]) * (1.0 - lam_init)
        dsl.append(d.astype(BF16))
    pieces += dsl + [ogqa_ref[0].astype(BF16)]
    o = jnp.concatenate(pieces, axis=1)
    y = _dot(o, wout_ref[...])
    x = x_ref[0] + mod_ref[0, 2:3, :] * y
    xo_ref[0] = x
    ms = jnp.mean(x * x, axis=-1, keepdims=True)
    h = x * lax.rsqrt(ms + NORM_EPS) * gffn_ref[...]
    h = h * (1.0 + mod_ref[0, 4:5, :]) + mod_ref[0, 3:4, :]
    h_ref[0] = h.astype(BF16)
    lg = _dot3_nt(wr_ref[...], h)
    for t in range(lg.shape[1] // ROUTE_T):
        lg_ref[0, t] = lg[:, t * ROUTE_T:(t + 1) * ROUTE_T]


def outproj_mod_router(x, o_na, o_mla, o_diff, o_gqa, mod, g_sub_t, g_ffn, lam, mats, w_out, w_router_t, lam_init):
    b, n, d = x.shape
    tm = min(512, n)
    nt = tm // ROUTE_T
    kern = functools.partial(_outproj_kernel, lam_init=lam_init)
    tok = lambda w: pl.BlockSpec((1, tm, w), lambda i, j: (i, j, 0))
    c2 = lambda i, j: (0, 0)
    return pl.pallas_call(
        kern,
        grid=(b, n // tm),
        in_specs=[tok(d), tok(256), tok(256), tok(512), tok(256),
                  pl.BlockSpec((1, 8, d), lambda i, j: (i, 0, 0)),
                  pl.BlockSpec((1, LANES), c2), pl.BlockSpec((1, d), c2),
                  pl.BlockSpec(memory_space=pltpu.SMEM),
                  pl.BlockSpec(mats.shape, lambda i, j: (0, 0, 0)),
                  pl.BlockSpec((d, d), c2), pl.BlockSpec((N_EXPERTS, d), c2)],
        out_specs=[tok(d), tok(d),
                   pl.BlockSpec((1, nt, N_EXPERTS, ROUTE_T), lambda i, j: (i, j, 0, 0))],
        out_shape=[jax.ShapeDtypeStruct((b, n, d), F32), jax.ShapeDtypeStruct((b, n, d), BF16),
                   jax.ShapeDtypeStruct((b, n // ROUTE_T, N_EXPERTS, ROUTE_T), F32)],
        compiler_params=_cparams(("parallel", "arbitrary")),
        name="outproj_mod_router",
    )(x, o_na, o_mla, o_diff, o_gqa, mod, g_sub_t, g_ffn.reshape(1, d), lam, mats, w_out, w_router_t)


def _route_kernel(lg_ref, tri_ref, aff_ref, pos_ref, off_ref, *, cap):
    nb = lg_ref.shape[1]
    lg = lg_ref[0]
    mx = jnp.max(lg, axis=1, keepdims=True)
    ex = jnp.exp(lg - mx)
    aff = ex / jnp.sum(ex, axis=1, keepdims=True)
    aff_ref[0] = aff
    bits = lax.bitcast_convert_type(aff, jnp.int32)

    def count_ge(t):
        hit = jnp.where(bits >= t[None], 1.0, 0.0)
        return jnp.sum(jnp.sum(hit, axis=0), axis=1, keepdims=True)

    def bis(i, t):
        cand = t | (jnp.int32(1) << (30 - i))
        return jnp.where(count_ge(cand) >= float(cap), cand, t)

    thr = lax.fori_loop(0, 31, bis, jnp.zeros((N_EXPERTS, 1), jnp.int32))
    need = float(cap) - count_ge(thr + 1)
    tri = tri_ref[...]

    def blk(jb, carry):
        c_eq, c_pos = carry
        bb = lax.bitcast_convert_type(aff_ref[0, jb], jnp.int32)
        gt = bb > thr
        eq = bb == thr
        eq_before = _dot(jnp.where(eq, 1.0, 0.0).astype(BF16), tri) + c_eq
        sel = gt | (eq & (eq_before < need))
        sel_f = jnp.where(sel, 1.0, 0.0)
        before = _dot(sel_f.astype(BF16), tri) + c_pos
        pos_ref[0, jb] = jnp.where(sel, before, -1.0).astype(jnp.int32)
        off_ref[0, jb] = jnp.broadcast_to(c_pos, (N_EXPERTS, LANES)).astype(jnp.int32)
        c_eq = c_eq + jnp.sum(jnp.where(eq, 1.0, 0.0), axis=1, keepdims=True)
        c_pos = c_pos + jnp.sum(sel_f, axis=1, keepdims=True)
        return c_eq, c_pos

    zero = jnp.zeros((N_EXPERTS, 1), F32)
    lax.fori_loop(0, nb, blk, (zero, zero))


def route(logits, tri, cap):
    b, nb, e, t = logits.shape
    kern = functools.partial(_route_kernel, cap=cap)
    spec = pl.BlockSpec((1, nb, e, t), lambda i: (i, 0, 0, 0))
    return pl.pallas_call(
        kern,
        grid=(b,),
        in_specs=[spec, pl.BlockSpec(tri.shape, lambda i: (0, 0))],
        out_specs=[spec, spec, pl.BlockSpec((1, nb, e, LANES), lambda i: (i, 0, 0, 0))],
        out_shape=[jax.ShapeDtypeStruct((b, nb, e, t), F32), jax.ShapeDtypeStruct((b, nb, e, t), jnp.int32),
                   jax.ShapeDtypeStruct((b, nb, e, LANES), jnp.int32)],
        compiler_params=_cparams(("arbitrary",)),
        name="route",
    )(logits, tri)


SLOT_ALIGN = 16
SLOT_WIN = 128


def _for_each_slot_window(off_ref, idx0, nsub, st, cap, fn):
    n_extra = min(-(-(ROUTE_T + SLOT_ALIGN) // st), cap // st) - 1
    wins = []
    for jj in range(nsub):
        off, nxt = off_ref[idx0 + jj], off_ref[idx0 + jj + 1]
        start = pl.multiple_of(jnp.minimum(off // SLOT_ALIGN * SLOT_ALIGN, cap - st), SLOT_ALIGN)
        fn(jj, start, None)
        wins.append((start, nxt))
    if n_extra == 0:
        return

    @pl.when(functools.reduce(jnp.logical_or, [nxt > start + st for start, nxt in wins]))
    def _():
        for jj, (start, nxt) in enumerate(wins):
            for w in range(1, n_extra + 1):
                @pl.when(nxt > start + w * st)
                def _():
                    fn(jj, pl.multiple_of(jnp.minimum(start + w * st, cap - st), SLOT_ALIGN), start + w * st)


def _slot_one_hot(slot, prow, start, lower, st):
    hit = slot == (prow - start)
    return hit if lower is None else hit & (prow >= lower) & (prow < lower + st)


def _gather_kernel(off_ref, h_ref, pos_ref, aff_ref, xe_ref, gs_ref, acc_ref, gacc_ref, *, st, cap, nsub, nb):
    bi, e, ch = pl.program_id(0), pl.program_id(1), pl.program_id(2)

    @pl.when(ch == 0)
    def _():
        acc_ref[...] = jnp.zeros(acc_ref.shape, F32)
        gacc_ref[...] = jnp.zeros(gacc_ref.shape, F32)

    slot = lax.broadcasted_iota(jnp.int32, (st, ROUTE_T), 0)

    def add(jj, start, lower):
        hit = _slot_one_hot(slot, pos_ref[0, jj, pl.ds(e, 1), :], start, lower, st)
        acc_ref[pl.ds(start, st), :] += _dot(jnp.where(hit, 1.0, 0.0).astype(BF16),
                                             h_ref[0, jj * ROUTE_T:(jj + 1) * ROUTE_T, :])
        gacc_ref[pl.ds(start, st), :] += jnp.sum(jnp.where(hit, aff_ref[0, jj, pl.ds(e, 1), :], 0.0),
                                                 axis=1, keepdims=True)

    _for_each_slot_window(off_ref, (bi * N_EXPERTS + e) * (nb + 1) + ch * nsub, nsub, st, cap, add)

    @pl.when(ch == pl.num_programs(2) - 1)
    def _():
        xe_ref[0, 0] = acc_ref[...].astype(BF16)
        gs_ref[0, 0] = gacc_ref[...]


def moe_gather(offs, h, pos, aff, cap):
    b, n, d = h.shape
    nb = n // ROUTE_T
    st = min(SLOT_WIN, cap)
    chunk = min(2048, n)
    nsub = chunk // ROUTE_T
    kern = functools.partial(_gather_kernel, st=st, cap=cap, nsub=nsub, nb=nb)
    rspec = pl.BlockSpec((1, nsub, N_EXPERTS, ROUTE_T), lambda i, e, c, off: (i, c, 0, 0))
    return pl.pallas_call(
        kern,
        grid_spec=pltpu.PrefetchScalarGridSpec(
            num_scalar_prefetch=1,
            grid=(b, N_EXPERTS, n // chunk),
            in_specs=[pl.BlockSpec((1, chunk, d), lambda i, e, c, off: (i, c, 0)), rspec, rspec],
            out_specs=[pl.BlockSpec((1, 1, cap, d), lambda i, e, c, off: (i, e, 0, 0)),
                       pl.BlockSpec((1, 1, cap, 1), lambda i, e, c, off: (i, e, 0, 0))],
            scratch_shapes=[pltpu.VMEM((cap, d), F32), pltpu.VMEM((cap, 1), F32)]),
        out_shape=[jax.ShapeDtypeStruct((b, N_EXPERTS, cap, d), BF16),
                   jax.ShapeDtypeStruct((b, N_EXPERTS, cap, 1), F32)],
        compiler_params=_cparams(("parallel", "parallel", "arbitrary")),
        name="moe_gather",
    )(offs, h, pos, aff)


FF_TILE = 256


def _ffn_kernel(x_ref, g_ref, wg_ref, wu_ref, wd_ref, y_ref, acc_ref):
    f = pl.program_id(2)

    @pl.when(f == 0)
    def _():
        acc_ref[...] = jnp.zeros(acc_ref.shape, F32)

    x = x_ref[0, 0]
    gate = _dot(x, wg_ref[0, 0].astype(BF16))
    up = _dot(x, wu_ref[0, 0].astype(BF16))
    hmid = (_silu(gate) * up).astype(BF16)
    acc_ref[...] += _dot(hmid, wd_ref[0, 0].astype(BF16))

    @pl.when(f == pl.num_programs(2) - 1)
    def _():
        y_ref[0, 0] = (acc_ref[...] * g_ref[0, 0]).astype(BF16)


def moe_ffn(xe, gs, w_gate, w_up, w_down, layer):
    b, e, cap, d = xe.shape
    dff = w_gate.shape[3]
    return pl.pallas_call(
        _ffn_kernel,
        grid=(e, b, dff // FF_TILE),
        in_specs=[pl.BlockSpec((1, 1, cap, d), lambda ei, bi, f: (bi, ei, 0, 0)),
                  pl.BlockSpec((1, 1, cap, 1), lambda ei, bi, f: (bi, ei, 0, 0)),
                  pl.BlockSpec((1, 1, d, FF_TILE), lambda ei, bi, f: (layer, ei, 0, f)),
                  pl.BlockSpec((1, 1, d, FF_TILE), lambda ei, bi, f: (layer, ei, 0, f)),
                  pl.BlockSpec((1, 1, FF_TILE, d), lambda ei, bi, f: (layer, ei, f, 0))],
        out_specs=pl.BlockSpec((1, 1, cap, d), lambda ei, bi, f: (bi, ei, 0, 0)),
        out_shape=jax.ShapeDtypeStruct((b, e, cap, d), BF16),
        scratch_shapes=[pltpu.VMEM((cap, d), F32)],
        compiler_params=_cparams(("parallel", "parallel", "arbitrary")),
        name="moe_ffn",
    )(xe, gs, w_gate, w_up, w_down)


def _combine_kernel(off_ref, x_ref, ye_ref, pos_ref, mod_ref, o_ref, *, st, cap, nsub, nb):
    bi, ch, e = pl.program_id(0), pl.program_id(1), pl.program_id(2)

    @pl.when(e == 0)
    def _():
        o_ref[...] = jnp.zeros(o_ref.shape, F32)

    slot = lax.broadcasted_iota(jnp.int32, (st, ROUTE_T), 0)

    def add(jj, start, lower):
        hit = _slot_one_hot(slot, pos_ref[0, jj, pl.ds(e, 1), :], start, lower, st)
        oh = jnp.where(hit, 1.0, 0.0).astype(BF16)
        o_ref[0, jj * ROUTE_T:(jj + 1) * ROUTE_T, :] += lax.dot_general(
            oh, ye_ref[0, 0, pl.ds(start, st), :], (((0,), (0,)), ((), ())), preferred_element_type=F32)

    _for_each_slot_window(off_ref, (bi * N_EXPERTS + e) * (nb + 1) + ch * nsub, nsub, st, cap, add)

    @pl.when(e == pl.num_programs(2) - 1)
    def _():
        o_ref[0] = x_ref[0] + mod_ref[0, 5:6, :] * o_ref[0]


def moe_combine(offs, x, ye, pos, mod, cap, slot0):
    b, n, d = x.shape
    nb = n // ROUTE_T
    st = min(SLOT_WIN, cap)
    chunk = min(2048, n)
    nsub = chunk // ROUTE_T
    assert slot0 % cap == 0
    kern = functools.partial(_combine_kernel, st=st, cap=cap, nsub=nsub, nb=nb)
    return pl.pallas_call(
        kern,
        grid_spec=pltpu.PrefetchScalarGridSpec(
            num_scalar_prefetch=1,
            grid=(b, n // chunk, N_EXPERTS),
            in_specs=[pl.BlockSpec((1, chunk, d), lambda i, c, e, off: (i, c, 0)),
                      pl.BlockSpec((1, 1, cap, d), lambda i, c, e, off: (i, e, slot0 // cap, 0)),
                      pl.BlockSpec((1, nsub, N_EXPERTS, ROUTE_T), lambda i, c, e, off: (i, c, 0, 0)),
                      pl.BlockSpec((1, 8, d), lambda i, c, e, off: (i, 0, 0))],
            out_specs=pl.BlockSpec((1, chunk, d), lambda i, c, e, off: (i, c, 0))),
        out_shape=jax.ShapeDtypeStruct((b, n, d), F32),
        compiler_params=_cparams(("parallel", "parallel", "arbitrary")),
        name="moe_combine",
    )(offs, x, ye, pos, mod)


def expert_choice_ffn(sets, tri, w_gate, w_up, w_down, layer):
    routed = []
    for x, h, logits, mod in sets:
        b, n, _ = x.shape
        cap = EC_CAPACITY * n // N_EXPERTS
        aff, pos, off = route(logits, tri, cap)
        first = jnp.transpose(off[..., 0], (0, 2, 1))
        offs = jnp.concatenate([first, jnp.full((b, N_EXPERTS, 1), cap, jnp.int32)], axis=2).reshape(-1)
        xe, gs = moe_gather(offs, h, pos, aff, cap)
        routed.append((offs, pos, cap, xe, gs))
    xe_all = jnp.concatenate([r[3] for r in routed], axis=2) if len(routed) > 1 else routed[0][3]
    gs_all = jnp.concatenate([r[4] for r in routed], axis=2) if len(routed) > 1 else routed[0][4]
    ye = moe_ffn(xe_all, gs_all, w_gate, w_up, w_down, layer)
    outs, slot0 = [], 0
    for (x, _, _, mod), (offs, pos, cap, _, _) in zip(sets, routed):
        outs.append(moe_combine(offs, x, ye, pos, mod, cap, slot0))
        slot0 += cap
    return outs


def _rope_tables(n_rows_grid):
    s = n_rows_grid * GRID_W
    t = np.arange(s)
    row, col = (t // GRID_W).astype(np.float64), (t % GRID_W).astype(np.float64)

    def unit(n):
        half = n // 2
        inv = ROPE_THETA ** (-np.arange(0, n, 2, dtype=np.float64) / n)
        out = []
        for pos in (row, col):
            ang = pos[:, None] * inv[None, :]
            c, sn = np.cos(ang), np.sin(ang)
            z = np.zeros_like(sn)
            out.append((np.concatenate([c, c], 1), np.concatenate([-sn, z], 1), np.concatenate([z, sn], 1)))
        return [np.concatenate([out[0][i], out[1][i]], 1) for i in range(3)]

    ident = lambda w: (np.ones((s, w), np.float32), np.zeros((s, w), np.float32), np.zeros((s, w), np.float32))
    u16 = unit(16)
    u32 = unit(32)
    tabs = []
    idt = ident(64)
    idt32 = ident(32)
    for i in range(3):
        tabs.append(np.concatenate([idt[i], u16[i], idt32[i]], 1))
    for i in range(3):
        tabs.append(np.concatenate([u16[i]] * 4, 1))
    for i in range(3):
        tabs.append(np.concatenate([u32[i]] * 2, 1))
    return np.stack(tabs).astype(np.float32)


def _identity_tables(n):
    one, zero = np.ones((n, LANES), np.float32), np.zeros((n, LANES), np.float32)
    return np.stack([one, zero, zero] * 3)


def _seg_matrices():
    i = np.arange(LANES)
    ones = np.ones((LANES, LANES), np.float32)
    m64 = (i[:, None] // 64 == i[None, :] // 64).astype(np.float32)
    m32 = (i[:, None] // 32 == i[None, :] // 32).astype(np.float32)
    return np.stack([ones, m64, m32])


def _na_bias(rpb):
    c = np.arange(GRID_W)
    rl = np.arange(NA_QROWS)
    dc = np.clip(c[None, :] - c[:, None] + NA_WIN_W - 1, 0, 2 * NA_WIN_W - 2)
    d = np.array([-1, 0, 1])
    dr = np.clip(NA_QROWS * d[:, None, None] + rl[None, None, :] - rl[None, :, None] + NA_WIN_H - 1,
                 0, 2 * NA_WIN_H - 2)
    cols = jnp.take(rpb.astype(F32) * LOG2E, jnp.asarray(dc), axis=2)
    full = jnp.take(cols, jnp.asarray(dr), axis=1)
    return jnp.transpose(full, (0, 1, 2, 4, 3, 5)).reshape(rpb.shape[0], 3, NA_TQ, NA_TQ)


def _pad_cols(w, width):
    return jnp.pad(w, ((0, 0), (0, width - w.shape[1])))


def _layer_params(l, w_in, g_na_q, g_na_k, na_rpb, g_mla_cq, w_mla_uq, g_mla_q, g_mla_ckv, w_mla_ukv,
                  g_mla_k_nope, g_mla_k_rope, g_diff_q, g_diff_k, g_diff_sub, g_gqa_q, g_gqa_k, w_out):
    wi = w_in[l]
    (naq, nak, nav, cq, ckv, kr, dq, dk, dv, gq, gk, gv) = jnp.split(
        wi, np.cumsum([256, 256, 256, 256, 128, 32, 256, 256, 256, 256, 128])[:], axis=1)
    gq4 = gq.reshape(-1, 4, 64)[:, jnp.array([0, 2, 1, 3])].reshape(-1, 256)
    zeros = lambda w: jnp.zeros((wi.shape[0], w), wi.dtype)
    w_in_r = jnp.concatenate([naq, nak, nav, cq, ckv, dq, dk, dv, gq4, gk, gv, zeros(64), kr, zeros(32)],
                             axis=1).astype(BF16)
    uq = w_mla_uq[l].reshape(MLA_Q_RANK, 4, 96)
    wuq = jnp.pad(uq, ((0, 0), (0, 0), (0, 32))).reshape(MLA_Q_RANK, 512).astype(BF16)
    ukv = w_mla_ukv[l].reshape(MLA_KV_RANK, 4, 128)
    wuk = jnp.pad(ukv[:, :, :64], ((0, 0), (0, 0), (0, 64))).reshape(MLA_KV_RANK, 512).astype(BF16)
    wuv = ukv[:, :, 64:].reshape(MLA_KV_RANK, 256).astype(BF16)
    row = lambda v: jnp.pad(v, (0, 512 - v.shape[0]))
    z32, z64 = jnp.zeros((32,), F32), jnp.zeros((64,), F32)
    gains = jnp.stack([
        row(jnp.tile(g_na_q[l], 4)), row(jnp.tile(g_na_k[l], 4)), row(g_mla_cq[l]),
        row(jnp.tile(jnp.concatenate([g_mla_q[l], z32]), 4)), row(g_mla_ckv[l]),
        row(jnp.tile(jnp.concatenate([g_mla_k_nope[l], z64]), 4)),
        row(jnp.concatenate([z64, g_mla_k_rope[l], z32])),
        row(jnp.tile(g_diff_q[l].reshape(-1), 4)), row(jnp.tile(g_diff_k[l].reshape(-1), 4)),
        row(jnp.tile(g_gqa_q[l], 4)), row(jnp.tile(g_gqa_k[l], 2))] + [jnp.zeros((512,), F32)] * 5)
    wo = w_out[l]
    wo_g = wo[768:].reshape(4, 64, -1)[jnp.array([0, 2, 1, 3])].reshape(256, -1)
    w_out_r = jnp.concatenate([wo[:768], wo_g], axis=0).astype(BF16)
    return dict(w_in=w_in_r, wuq=wuq, wuk=wuk, wuv=wuv, gains=gains, w_out=w_out_r,
                bias=_na_bias(na_rpb[l]), g_sub=jnp.tile(g_diff_sub[l], 2).reshape(1, LANES))


def kernel(x, c, ctx, c_ctx, w_mod, b_mod, g_attn, g_ffn, w_in, g_na_q, g_na_k, na_rpb, g_mla_cq, w_mla_uq, g_mla_q, g_mla_ckv, w_mla_ukv, g_mla_k_nope, g_mla_k_rope, g_diff_q, g_diff_k, diff_lambda, g_diff_sub, g_gqa_q, g_gqa_k, w_out, w_router, w_gate, w_up, w_down):
    b, s, d = x.shape
    n_ctx = ctx.shape[1]
    rows = s // GRID_W
    tabs = jnp.asarray(_rope_tables(rows))
    tabs_ctx = jnp.asarray(_identity_tables(n_ctx))
    mats = jnp.asarray(_seg_matrices()).astype(BF16)
    tri = jnp.asarray(np.triu(np.ones((ROUTE_T, ROUTE_T), np.float32), 1)).astype(BF16)
    c_rows = jnp.concatenate([c, c_ctx[None], jnp.zeros((8 - b - 1, d), F32)], axis=0)
    xc = ctx
    for l in range(DEPTH):
        need_ctx = l < DEPTH - 1
        lam_init = 0.8 - 0.6 * math.exp(-0.3 * l)
        lp = diff_lambda[l].astype(F32)
        lam = (jnp.exp(jnp.sum(lp[0] * lp[1])) - jnp.exp(jnp.sum(lp[2] * lp[3])) + lam_init).reshape(1, 1)
        prm = _layer_params(l, w_in, g_na_q, g_na_k, na_rpb, g_mla_cq, w_mla_uq, g_mla_q, g_mla_ckv, w_mla_ukv,
                            g_mla_k_nope, g_mla_k_rope, g_diff_q, g_diff_k, g_diff_sub, g_gqa_q, g_gqa_k, w_out)
        modv = mod_vectors(c_rows, w_mod[l], b_mod[l]).reshape(8, N_MOD, d)
        mod = jnp.pad(modv[:b], ((0, 0), (0, 2), (0, 0)))
        mod_c = jnp.broadcast_to(jnp.pad(modv[b:b + 1], ((0, 0), (0, 2), (0, 0))), (b, 8, d))
        w_router_t = w_router[l].T

        def mix_inputs(xin, m, tb):
            return proj_prep(xin, g_attn[l], m[:, 0:1], m[:, 1:2], prm["w_in"], tb, prm["gains"], mats,
                             prm["wuq"], prm["wuk"], prm["wuv"])

        (naq, nak, nav, mq, mk, mv, dq, dk, dv, gq, gk, gv, stat) = mix_inputs(x, mod, tabs)
        (naq_c, nak_c, nav_c, mq_c, mk_c, mv_c, dq_c, dk_c, dv_c, gq_c, gk_c, gv_c, stat_c) = mix_inputs(
            xc, mod_c, tabs_ctx)
        cat = lambda a, bb: jnp.concatenate([a, bb], axis=1)
        n2 = jnp.max(stat, axis=(1, 3))
        n2c = jnp.max(stat_c, axis=(1, 3))
        q2 = lambda name: n2[:, STAT_ROWS.index(name)]
        k2 = lambda name: jnp.maximum(n2[:, STAT_ROWS.index(name)], n2c[:, STAT_ROWS.index(name)])

        ext = values_transposed
        o_na = na_attention(naq, nak, nav, nak_c, nav_c, prm["bias"], rows)
        o_mla = attention(q2("mq"), k2("mk"), k2("mv"), mq, cat(mk_c, mk), ext(cat(mv_c, mv)), HEADS_MLA)
        o_diff = attention(q2("dq"), k2("dk"), k2("dv"), dq, cat(dk_c, dk), ext(cat(dv_c, dv)), HEADS_DIFF)
        o_gqa = attention(q2("gq"), k2("gk"), k2("gv"), gq, cat(gk_c, gk), ext(cat(gv_c, gv)), HEADS_GQA)
        x_mid, h2, logits = outproj_mod_router(x, o_na, o_mla, o_diff, o_gqa, mod, prm["g_sub"], g_ffn[l], lam,
                                               mats, prm["w_out"], w_router_t, lam_init)
        if need_ctx:
            oc_na = flash_attention(naq_c, nak_c, ext(nav_c), HEADS_PAIRED)
            oc_mla = flash_attention(mq_c, mk_c, ext(mv_c), HEADS_MLA)
            oc_diff = flash_attention(dq_c, dk_c, ext(dv_c), HEADS_DIFF)
            oc_gqa = flash_attention(gq_c, gk_c, ext(gv_c), HEADS_GQA)
            xc_mid, hc2, logits_c = outproj_mod_router(xc, oc_na, oc_mla, oc_diff, oc_gqa, mod_c, prm["g_sub"],
                                                       g_ffn[l], lam, mats, prm["w_out"], w_router_t, lam_init)
            x, xc = expert_choice_ffn([(x_mid, h2, logits, mod), (xc_mid, hc2, logits_c, mod_c)],
                                      tri, w_gate, w_up, w_down, l)
        else:
            (x,) = expert_choice_ffn([(x_mid, h2, logits, mod)], tri, w_gate, w_up, w_down, l)
    return x
```

```python
import functools
import math

import numpy as np
import jax
import jax.numpy as jnp
from jax import lax
from jax.experimental import pallas as pl
from jax.experimental.pallas import tpu as pltpu

D_MODEL = 1024
GRID_W = 64
HEAD_DIM = 64
N_HEADS = 4
NA_WIN_H = 8
NA_WIN_W = 16
MLA_Q_RANK = 256
MLA_KV_RANK = 128
MLA_NOPE_DIM = 64
MLA_ROPE_DIM = 32
MLA_V_DIM = 64
DIFF_QK_DIM = 32
N_EXPERTS = 16
EC_CAPACITY = 2
D_FF = 2816
ROPE_THETA = 10000.0
NORM_EPS = 1e-6
N_MOD = 6
DEPTH = 2

LANES = 128
P_COLS = 2560
VMEM_LIMIT = 56 * 1024 * 1024
NEG_BIG = -1e30
LOG2E = math.log2(math.e)

F32 = jnp.float32
BF16 = jnp.bfloat16


def _cparams(sem):
    return pltpu.CompilerParams(dimension_semantics=sem, vmem_limit_bytes=VMEM_LIMIT)


def _split(a):
    hi = a.astype(BF16)
    lo = (a - hi.astype(F32)).astype(BF16)
    return hi, lo


def _dot(a, b):
    return jnp.dot(a, b, preferred_element_type=F32)


def _dot_nt(a, b):
    return lax.dot_general(a, b, (((1,), (1,)), ((), ())), preferred_element_type=F32)


def _dot3(a, b):
    ah, al = _split(a)
    bh, bl = _split(b)
    return _dot(ah, bh) + _dot(ah, bl) + _dot(al, bh)


def _dot3_nt(a, b):
    ah, al = _split(a)
    bh, bl = _split(b)
    return _dot_nt(ah, bh) + _dot_nt(ah, bl) + _dot_nt(al, bh)


def _silu(v):
    return v * jax.nn.sigmoid(v)


def _mod_kernel(c_ref, w_ref, b_ref, o_ref):
    o_ref[...] = _dot3(_silu(c_ref[...]), w_ref[...]) + b_ref[...]


def mod_vectors(c_rows, w_mod, b_mod):
    m, d = c_rows.shape
    n = w_mod.shape[1]
    tn = 1536
    return pl.pallas_call(
        _mod_kernel,
        grid=(n // tn,),
        in_specs=[pl.BlockSpec((m, d), lambda j: (0, 0)),
                  pl.BlockSpec((d, tn), lambda j: (0, j)),
                  pl.BlockSpec((1, tn), lambda j: (0, j))],
        out_specs=pl.BlockSpec((m, tn), lambda j: (0, j)),
        out_shape=jax.ShapeDtypeStruct((m, n), F32),
        compiler_params=_cparams(("arbitrary",)),
        name="mod_vectors",
    )(c_rows, w_mod, b_mod.reshape(1, n))


P_NAQ, P_NAK, P_NAV, P_CQ, P_CKV = 0, 256, 512, 768, 1024
P_DQ, P_DK, P_DV, P_GQ, P_GK, P_GV, P_KR = 1152, 1408, 1664, 1920, 2176, 2304, 2432
(G_NAQ, G_NAK, G_CQ, G_MQ, G_CKV, G_KN, G_KR, G_DQ, G_DK, G_GQ, G_GK) = range(11)
M_ONES, M_SEG64, M_SEG32 = 0, 1, 2
T_MLA, T_DIFF, T_GQA = 0, 3, 6


def _seg_norm(x, mat, inv_n, g):
    sq = x * x
    hi, lo = _split(sq)
    ms = (_dot(hi, mat) + _dot(lo, mat)) * inv_n
    return x * lax.rsqrt(ms + NORM_EPS) * g


def _rope(x, tab_ref, t0, half):
    c, s1, s2 = tab_ref[t0], tab_ref[t0 + 1], tab_ref[t0 + 2]
    return (x * c + pltpu.roll(x, LANES - half, axis=1) * s1
            + pltpu.roll(x, half, axis=1) * s2)


def _prep_kernel(x_ref, g_ref, sh_ref, sc_ref, w_ref, tab_ref, gain_ref, mat_ref, wuq_ref, wuk_ref, wuv_ref,
                 naq_ref, nak_ref, nav_ref, mq_ref, mk_ref, mv_ref,
                 dq_ref, dk_ref, dv_ref, gq_ref, gk_ref, gv_ref, st_ref):
    lane = lax.broadcasted_iota(jnp.int32, (1, LANES), 1)
    lo_half = lane < 64
    ones_m, m64, m32 = mat_ref[M_ONES], mat_ref[M_SEG64], mat_ref[M_SEG32]
    norm2 = {}

    x = x_ref[0]
    y = x * lax.rsqrt(jnp.mean(x * x, axis=-1, keepdims=True) + NORM_EPS) * g_ref[...]
    p = _dot((y * (1.0 + sc_ref[0]) + sh_ref[0]).astype(BF16), w_ref[...])

    def slab(off, i):
        return p[:, off + i * LANES: off + (i + 1) * LANES]

    def track(name, xb, mat):
        xf = xb.astype(F32)
        hi, lo = _split(xf * xf)
        cur = jnp.max(_dot(hi, mat) + _dot(lo, mat), axis=0, keepdims=True)
        norm2[name] = jnp.maximum(norm2[name], cur) if name in norm2 else cur
        return xb

    def track_elem(name, xb):
        for i in range(xb.shape[1] // LANES):
            xf = xb[:, i * LANES:(i + 1) * LANES].astype(F32)
            cur = jnp.max(xf * xf, axis=0, keepdims=True)
            norm2[name] = jnp.maximum(norm2[name], cur) if name in norm2 else cur
        return xb

    def gain(row, i):
        return gain_ref[row:row + 1, i * LANES:(i + 1) * LANES]

    s_na = HEAD_DIM ** -0.5 * LOG2E
    for i in range(2):
        q = _seg_norm(slab(P_NAQ, i), m64, 1.0 / 64, gain(G_NAQ, i)) * s_na
        naq_ref[0, :, (2 * i) * LANES:(2 * i + 1) * LANES] = jnp.where(lo_half, q, 0.0).astype(BF16)
        naq_ref[0, :, (2 * i + 1) * LANES:(2 * i + 2) * LANES] = jnp.where(lo_half, 0.0, q).astype(BF16)
        k = _seg_norm(slab(P_NAK, i), m64, 1.0 / 64, gain(G_NAK, i))
        nak_ref[0, :, i * LANES:(i + 1) * LANES] = k.astype(BF16)
        nav_ref[0, :, i * LANES:(i + 1) * LANES] = slab(P_NAV, i).astype(BF16)

    cq = p[:, P_CQ:P_CQ + MLA_Q_RANK]
    cq = cq * lax.rsqrt(jnp.mean(cq * cq, axis=-1, keepdims=True) + NORM_EPS) * gain_ref[G_CQ:G_CQ + 1, :MLA_Q_RANK]
    uq = _dot(cq.astype(BF16), wuq_ref[...])
    s_mla = (MLA_NOPE_DIM + MLA_ROPE_DIM) ** -0.5 * LOG2E
    ckv = p[:, P_CKV:P_CKV + MLA_KV_RANK]
    ckv = ckv * lax.rsqrt(jnp.mean(ckv * ckv, axis=-1, keepdims=True) + NORM_EPS) * gain_ref[G_CKV:G_CKV + 1, :MLA_KV_RANK]
    ckv_b = ckv.astype(BF16)
    uk = _dot(ckv_b, wuk_ref[...])
    mv_ref[0] = track_elem("mv", _dot(ckv_b, wuv_ref[...]).astype(BF16))
    kr = _seg_norm(slab(P_KR, 0), ones_m, 1.0 / MLA_ROPE_DIM, gain(G_KR, 0))
    kr = _rope(kr, tab_ref, T_MLA, 8)
    for h in range(N_HEADS):
        q = _seg_norm(uq[:, h * LANES:(h + 1) * LANES], ones_m, 1.0 / (MLA_NOPE_DIM + MLA_ROPE_DIM), gain(G_MQ, h))
        q = _rope(q, tab_ref, T_MLA, 8) * s_mla
        mq_ref[0, :, h * LANES:(h + 1) * LANES] = track("mq", q.astype(BF16), ones_m)
        kn = _seg_norm(uk[:, h * LANES:(h + 1) * LANES], ones_m, 1.0 / MLA_NOPE_DIM, gain(G_KN, h))
        mk_ref[0, :, h * LANES:(h + 1) * LANES] = track("mk", (kn + kr).astype(BF16), ones_m)

    s_d = DIFF_QK_DIM ** -0.5 * LOG2E
    seg = lane >> 5
    for i in range(2):
        q = _seg_norm(slab(P_DQ, i), m32, 1.0 / 32, gain(G_DQ, i))
        q = _rope(q, tab_ref, T_DIFF, 8) * s_d
        track("dq", q.astype(BF16), m32)
        for j in range(4):
            dq_ref[0, :, (4 * i + j) * LANES:(4 * i + j + 1) * LANES] = jnp.where(seg == j, q, 0.0).astype(BF16)
        k = _seg_norm(slab(P_DK, i), m32, 1.0 / 32, gain(G_DK, i))
        dk_ref[0, :, i * LANES:(i + 1) * LANES] = track("dk", _rope(k, tab_ref, T_DIFF, 8).astype(BF16), m32)
        dv_ref[0, :, i * LANES:(i + 1) * LANES] = track_elem("dv", slab(P_DV, i).astype(BF16))

    s_g = HEAD_DIM ** -0.5 * LOG2E
    for i in range(2):
        q = _seg_norm(slab(P_GQ, i), m64, 1.0 / 64, gain(G_GQ, i))
        q = _rope(q, tab_ref, T_GQA, 16) * s_g
        track("gq", q.astype(BF16), m64)
        gq_ref[0, :, (2 * i) * LANES:(2 * i + 1) * LANES] = jnp.where(lo_half, q, 0.0).astype(BF16)
        gq_ref[0, :, (2 * i + 1) * LANES:(2 * i + 2) * LANES] = jnp.where(lo_half, 0.0, q).astype(BF16)
    k = _seg_norm(slab(P_GK, 0), m64, 1.0 / 64, gain(G_GK, 0))
    gk_ref[0] = track("gk", _rope(k, tab_ref, T_GQA, 16).astype(BF16), m64)
    gv_ref[0] = track_elem("gv", slab(P_GV, 0).astype(BF16))
    st_ref[0, 0] = jnp.concatenate([norm2[n] for n in STAT_ROWS]
                                   + [jnp.zeros((STAT_PAD - len(STAT_ROWS), LANES), F32)], axis=0)


PREP_WIDTHS = (512, 256, 256, 512, 512, 256, 1024, 256, 256, 512, 128, 128)
STAT_ROWS = ("mq", "mk", "mv", "dq", "dk", "dv", "gq", "gk", "gv")
STAT_PAD = 16


def proj_prep(x, g, shift, scale, w_in, tabs, gains, mats, wuq, wuk, wuv):
    b, n, d = x.shape
    tm = min(512, n)
    const2 = lambda i, j: (0, 0)
    stat_spec = pl.BlockSpec((1, 1, STAT_PAD, LANES), lambda i, j: (i, j, 0, 0))
    stat_shape = jax.ShapeDtypeStruct((b, n // tm, STAT_PAD, LANES), F32)
    return pl.pallas_call(
        _prep_kernel,
        grid=(b, n // tm),
        in_specs=[pl.BlockSpec((1, tm, d), lambda i, j: (i, j, 0)),
                  pl.BlockSpec((1, d), const2),
                  pl.BlockSpec((1, 1, d), lambda i, j: (i, 0, 0)),
                  pl.BlockSpec((1, 1, d), lambda i, j: (i, 0, 0)),
                  pl.BlockSpec((d, P_COLS), const2),
                  pl.BlockSpec((9, tm, LANES), lambda i, j: (0, j, 0)),
                  pl.BlockSpec(gains.shape, const2),
                  pl.BlockSpec(mats.shape, lambda i, j: (0, 0, 0)),
                  pl.BlockSpec(wuq.shape, const2),
                  pl.BlockSpec(wuk.shape, const2),
                  pl.BlockSpec(wuv.shape, const2)],
        out_specs=[pl.BlockSpec((1, tm, w), lambda i, j: (i, j, 0)) for w in PREP_WIDTHS] + [stat_spec],
        out_shape=[jax.ShapeDtypeStruct((b, n, w), BF16) for w in PREP_WIDTHS] + [stat_shape],
        compiler_params=_cparams(("parallel", "arbitrary")),
        name="proj_prep",
    )(x, g.reshape(1, d), shift, scale, w_in, tabs, gains, mats, wuq, wuk, wuv)


VT_ROWS = LANES + 16


def _value_rows(heads, h):
    pair = [g for g in range(len(heads)) if heads[g][3] == heads[h][3]]
    return pair.index(h) * 64, pair.index(h) * 64 + 64
KEY_CHUNK = 256
FLASH_TQ = 512
FLASH_TK_MAX = 3584


def _flash_kernel(q_ref, k_ref, vt_ref, o_ref, m_ref, l_ref, acc_ref, *, heads, n_out):
    ki = pl.program_id(2)

    @pl.when(ki == 0)
    def _():
        m_ref[...] = jnp.full(m_ref.shape, NEG_BIG, F32)
        l_ref[...] = jnp.zeros(l_ref.shape, F32)
        acc_ref[...] = jnp.zeros(acc_ref.shape, F32)

    for o in range(n_out):
        pair = [h for h in range(len(heads)) if heads[h][3] == o]
        for pos, h in enumerate(pair):
            qs, ks, vs, _ = heads[h]
            q = q_ref[0, :, qs * LANES:(qs + 1) * LANES]
            k = k_ref[0, :, ks * LANES:(ks + 1) * LANES]
            st = _dot_nt(k, q)
            lo, hi = _value_rows(heads, h)
            m, l, acc = m_ref[h], l_ref[h], acc_ref[o, lo:hi, :]
            for c0 in range(0, st.shape[0], KEY_CHUNK):
                rows = slice(c0, min(c0 + KEY_CHUNK, st.shape[0]))
                sc = st[rows]
                m_new = jnp.maximum(m, jnp.max(sc, axis=0, keepdims=True))
                a = jnp.exp2(m - m_new)
                pt = jnp.exp2(sc - m_new).astype(BF16)
                r = _dot(vt_ref[0, vs, :, rows], pt)
                l = a * l + r[LANES:LANES + 1]
                acc = acc * a + r[lo:hi]
                m = m_new
            m_ref[h], l_ref[h] = m, l
            acc_ref[o, lo:hi, :] = acc

    @pl.when(ki == pl.num_programs(2) - 1)
    def _():
        for o in range(n_out):
            pair = [h for h in range(len(heads)) if heads[h][3] == o]
            out_t = jnp.concatenate([acc_ref[o, 0:64, :] / l_ref[pair[0]],
                                     acc_ref[o, 64:128, :] / l_ref[pair[1]]], axis=0)
            o_ref[0, :, o * LANES:(o + 1) * LANES] = out_t.T


def _pick_tk(n_keys):
    best = 256
    for t in range(256, FLASH_TK_MAX + 1, 256):
        if n_keys % t == 0:
            best = t
    return best


def values_transposed(v):
    b, n, w = v.shape
    vt = jnp.transpose(v.reshape(b, n, w // LANES, LANES), (0, 2, 3, 1))
    return jnp.concatenate([vt, jnp.ones((b, w // LANES, VT_ROWS - LANES, n), v.dtype)], axis=2)


def flash_attention(q, k, vt, heads):
    b, s, qw = q.shape
    nk = k.shape[1]
    n_out = max(h[3] for h in heads) + 1
    tq = min(FLASH_TQ, s)
    tk = _pick_tk(nk)
    kern = functools.partial(_flash_kernel, heads=heads, n_out=n_out)
    return pl.pallas_call(
        kern,
        grid=(b, s // tq, nk // tk),
        in_specs=[pl.BlockSpec((1, tq, qw), lambda i, j, t: (i, j, 0)),
                  pl.BlockSpec((1, tk, k.shape[2]), lambda i, j, t: (i, t, 0)),
                  pl.BlockSpec((1, vt.shape[1], VT_ROWS, tk), lambda i, j, t: (i, 0, 0, t))],
        out_specs=pl.BlockSpec((1, tq, n_out * LANES), lambda i, j, t: (i, j, 0)),
        out_shape=jax.ShapeDtypeStruct((b, s, n_out * LANES), F32),
        scratch_shapes=[pltpu.VMEM((len(heads), 1, tq), F32),
                        pltpu.VMEM((len(heads), 1, tq), F32),
                        pltpu.VMEM((n_out, LANES, tq), F32)],
        compiler_params=_cparams(("parallel", "parallel", "arbitrary")),
        name="flash_attention",
    )(q, k, vt)


STAB_SHIFT = 100.0
BOUND_LIMIT = 113.0
BOUND_SLACK = 1.01
SUM_LOG2_LIMIT = 126.0
BOUNDED_TQ = 1024


def _flash_bounded_kernel(stab_ref, q_ref, k_ref, vt_ref, o_ref, acc_ref, *, heads, n_out):
    ki = pl.program_id(2)

    @pl.when(ki == 0)
    def _():
        acc_ref[...] = jnp.zeros(acc_ref.shape, F32)

    stab = stab_ref[pl.program_id(0), 0]
    for h, (qs, ks, vs, _) in enumerate(heads):
        q = q_ref[0, :, qs * LANES:(qs + 1) * LANES]
        k = k_ref[0, :, ks * LANES:(ks + 1) * LANES]
        pt = jnp.exp2(_dot_nt(k, q) - stab).astype(BF16)
        acc_ref[h] += _dot(vt_ref[0, vs], pt)

    @pl.when(ki == pl.num_programs(2) - 1)
    def _():
        for o in range(n_out):
            pair = [h for h in range(len(heads)) if heads[h][3] == o]
            parts = []
            for h in pair:
                lo, hi = _value_rows(heads, h)
                parts.append(acc_ref[h, lo:hi, :] / acc_ref[h, LANES:LANES + 1, :])
            o_ref[0, :, o * LANES:(o + 1) * LANES] = jnp.concatenate(parts, axis=0).T


def flash_attention_bounded(stab, q, k, vt, heads):
    b, s, qw = q.shape
    nk = k.shape[1]
    n_out = max(h[3] for h in heads) + 1
    tq = min(BOUNDED_TQ, s)
    tk = _pick_tk(nk)
    kern = functools.partial(_flash_bounded_kernel, heads=heads, n_out=n_out)
    return pl.pallas_call(
        kern,
        grid=(b, s // tq, nk // tk),
        in_specs=[pl.BlockSpec(memory_space=pltpu.SMEM),
                  pl.BlockSpec((1, tq, qw), lambda i, j, t: (i, j, 0)),
                  pl.BlockSpec((1, tk, k.shape[2]), lambda i, j, t: (i, t, 0)),
                  pl.BlockSpec((1, vt.shape[1], VT_ROWS, tk), lambda i, j, t: (i, 0, 0, t))],
        out_specs=pl.BlockSpec((1, tq, n_out * LANES), lambda i, j, t: (i, j, 0)),
        out_shape=jax.ShapeDtypeStruct((b, s, n_out * LANES), F32),
        scratch_shapes=[pltpu.VMEM((len(heads), VT_ROWS, tq), F32)],
        compiler_params=_cparams(("parallel", "parallel", "arbitrary")),
        name="flash_attention_bounded",
    )(stab, q, k, vt)


def attention(q2, k2, v2, q, k, vt, heads):
    bound = BOUND_SLACK * jnp.sqrt(q2 * k2)
    stab = (bound - STAB_SHIFT).reshape(-1, 1).astype(F32)
    sum_log2 = STAB_SHIFT + math.log2(k.shape[1]) + 0.5 * jnp.log2(jnp.maximum(jnp.max(v2), 1.0))
    return lax.cond((jnp.max(bound) <= BOUND_LIMIT) & (sum_log2 <= SUM_LOG2_LIMIT),
                    lambda: flash_attention_bounded(stab, q, k, vt, heads),
                    lambda: flash_attention(q, k, vt, heads))


HEADS_PAIRED = tuple((h, h // 2, h // 2, h // 2) for h in range(4))
HEADS_MLA = tuple((h, h, h // 2, h // 2) for h in range(4))
HEADS_GQA = tuple((h, 0, 0, h // 2) for h in range(4))
HEADS_DIFF = tuple((j, j // 4, j // 4, (j % 2) * 2 + j // 4) for j in range(8))


NA_QROWS = 4
NA_TQ = NA_QROWS * GRID_W


def _na_kernel(q_ref, kp_ref, kc_ref, kn_ref, vp_ref, vc_ref, vn_ref, kx_ref, vx_ref, bias_ref, o_ref, *, rows):
    j = pl.program_id(1)
    kh = min(NA_WIN_H, rows)
    qi = lax.broadcasted_iota(jnp.int32, (NA_TQ, 1), 0)
    ki = lax.broadcasted_iota(jnp.int32, (1, NA_TQ), 1)
    wshift = GRID_W.bit_length() - 1
    r = j * NA_QROWS + (qi >> wshift)
    c = qi & (GRID_W - 1)
    r0 = jnp.clip(r - kh // 2, 0, rows - kh)
    c0 = jnp.clip(c - NA_WIN_W // 2, 0, GRID_W - NA_WIN_W)
    kcol = ki & (GRID_W - 1)
    col_ok = (kcol >= c0) & (kcol < c0 + NA_WIN_W)
    masks = []
    for d in (-1, 0, 1):
        kr = (j + d) * NA_QROWS + (ki >> wshift)
        masks.append(col_ok & (kr >= r0) & (kr < r0 + kh))
    lo_half = lax.broadcasted_iota(jnp.int32, (1, LANES), 1) < 64
    for o in range(2):
        k_loc = [kp_ref[0, :, o * LANES:(o + 1) * LANES], kc_ref[0, :, o * LANES:(o + 1) * LANES],
                 kn_ref[0, :, o * LANES:(o + 1) * LANES]]
        v_all = jnp.concatenate([vp_ref[0, :, o * LANES:(o + 1) * LANES], vc_ref[0, :, o * LANES:(o + 1) * LANES],
                                 vn_ref[0, :, o * LANES:(o + 1) * LANES], vx_ref[0, :, o * LANES:(o + 1) * LANES]],
                                axis=0)
        kx = kx_ref[0, :, o * LANES:(o + 1) * LANES]
        res = []
        for h in (2 * o, 2 * o + 1):
            q = q_ref[0, :, h * LANES:(h + 1) * LANES]
            parts = [jnp.where(masks[d], _dot_nt(q, k_loc[d]) + bias_ref[h, d], NEG_BIG) for d in range(3)]
            parts.append(_dot_nt(q, kx))
            s = jnp.concatenate(parts, axis=1)
            m = jnp.max(s, axis=-1, keepdims=True)
            p = jnp.exp2(s - m)
            l = jnp.sum(p, axis=-1, keepdims=True)
            res.append(_dot(p.astype(BF16), v_all) / l)
        o_ref[0, :, o * LANES:(o + 1) * LANES] = jnp.where(lo_half, res[0], res[1])


def na_attention(q, k, v, kx, vx, bias, rows):
    b, s, _ = q.shape
    nblk = rows // NA_QROWS
    nx = kx.shape[1]
    kern = functools.partial(_na_kernel, rows=rows)
    prev = lambda i, j: (i, jnp.maximum(j - 1, 0), 0)
    cur = lambda i, j: (i, j, 0)
    nxt = lambda i, j: (i, jnp.minimum(j + 1, nblk - 1), 0)
    kv_spec = lambda f: pl.BlockSpec((1, NA_TQ, 2 * LANES), f)
    return pl.pallas_call(
        kern,
        grid=(b, nblk),
        in_specs=[pl.BlockSpec((1, NA_TQ, 4 * LANES), cur),
                  kv_spec(prev), kv_spec(cur), kv_spec(nxt),
                  kv_spec(prev), kv_spec(cur), kv_spec(nxt),
                  pl.BlockSpec((1, nx, 2 * LANES), lambda i, j: (i, 0, 0)),
                  pl.BlockSpec((1, nx, 2 * LANES), lambda i, j: (i, 0, 0)),
                  pl.BlockSpec(bias.shape, lambda i, j: (0, 0, 0, 0))],
        out_specs=pl.BlockSpec((1, NA_TQ, 2 * LANES), cur),
        out_shape=jax.ShapeDtypeStruct((b, s, 2 * LANES), F32),
        compiler_params=_cparams(("parallel", "arbitrary")),
        name="na_attention",
    )(q, k, k, k, v, v, v, kx, vx, bias)


ROUTE_T = 256


def _outproj_kernel(x_ref, ona_ref, omla_ref, od_ref, ogqa_ref, mod_ref, gsub_ref, gffn_ref, lam_ref, mat_ref,
                    wout_ref, wr_ref, xo_ref, h_ref, lg_ref, *, lam_init):
    lam = lam_ref[0, 0]
    m64 = mat_ref[M_SEG64]
    pieces = [ona_ref[0].astype(BF16), omla_ref[0].astype(BF16)]
    dsl = []
    for i in range(2):
        d = od_ref[0, :, i * LANES:(i + 1) * LANES] - lam * od_ref[0, :, (2 + i) * LANES:(3 + i) * LANES]
        d = _seg_norm(d, m64, 1.0 / 64, gsub_ref[...]) * (1.0 - lam_init)
        dsl.append(d.astype(BF16))
    pieces += dsl + [ogqa_ref[0].astype(BF16)]
    o = jnp.concatenate(pieces, axis=1)
    y = _dot(o, wout_ref[...])
    x = x_ref[0] + mod_ref[0, 2:3, :] * y
    xo_ref[0] = x
    ms = jnp.mean(x * x, axis=-1, keepdims=True)
    h = x * lax.rsqrt(ms + NORM_EPS) * gffn_ref[...]
    h = h * (1.0 + mod_ref[0, 4:5, :]) + mod_ref[0, 3:4, :]
    h_ref[0] = h.astype(BF16)
    lg = _dot3_nt(wr_ref[...], h)
    for t in range(lg.shape[1] // ROUTE_T):
        lg_ref[0, t] = lg[:, t * ROUTE_T:(t + 1) * ROUTE_T]


def outproj_mod_router(x, o_na, o_mla, o_diff, o_gqa, mod, g_sub_t, g_ffn, lam, mats, w_out, w_router_t, lam_init):
    b, n, d = x.shape
    tm = min(512, n)
    nt = tm // ROUTE_T
    kern = functools.partial(_outproj_kernel, lam_init=lam_init)
    tok = lambda w: pl.BlockSpec((1, tm, w), lambda i, j: (i, j, 0))
    c2 = lambda i, j: (0, 0)
    return pl.pallas_call(
        kern,
        grid=(b, n // tm),
        in_specs=[tok(d), tok(256), tok(256), tok(512), tok(256),
                  pl.BlockSpec((1, 8, d), lambda i, j: (i, 0, 0)),
                  pl.BlockSpec((1, LANES), c2), pl.BlockSpec((1, d), c2),
                  pl.BlockSpec(memory_space=pltpu.SMEM),
                  pl.BlockSpec(mats.shape, lambda i, j: (0, 0, 0)),
                  pl.BlockSpec((d, d), c2), pl.BlockSpec((N_EXPERTS, d), c2)],
        out_specs=[tok(d), tok(d),
                   pl.BlockSpec((1, nt, N_EXPERTS, ROUTE_T), lambda i, j: (i, j, 0, 0))],
        out_shape=[jax.ShapeDtypeStruct((b, n, d), F32), jax.ShapeDtypeStruct((b, n, d), BF16),
                   jax.ShapeDtypeStruct((b, n // ROUTE_T, N_EXPERTS, ROUTE_T), F32)],
        compiler_params=_cparams(("parallel", "arbitrary")),
        name="outproj_mod_router",
    )(x, o_na, o_mla, o_diff, o_gqa, mod, g_sub_t, g_ffn.reshape(1, d), lam, mats, w_out, w_router_t)


def _route_kernel(lg_ref, tri_ref, aff_ref, pos_ref, off_ref, *, cap):
    nb = lg_ref.shape[1]
    lg = lg_ref[0]
    mx = jnp.max(lg, axis=1, keepdims=True)
    ex = jnp.exp(lg - mx)
    aff = ex / jnp.sum(ex, axis=1, keepdims=True)
    aff_ref[0] = aff
    bits = lax.bitcast_convert_type(aff, jnp.int32)

    def count_ge(t):
        hit = jnp.where(bits >= t[None], 1.0, 0.0)
        return jnp.sum(jnp.sum(hit, axis=0), axis=1, keepdims=True)

    def bis(i, t):
        cand = t | (jnp.int32(1) << (30 - i))
        return jnp.where(count_ge(cand) >= float(cap), cand, t)

    thr = lax.fori_loop(0, 31, bis, jnp.zeros((N_EXPERTS, 1), jnp.int32))
    need = float(cap) - count_ge(thr + 1)
    tri = tri_ref[...]

    def blk(jb, carry):
        c_eq, c_pos = carry
        bb = lax.bitcast_convert_type(aff_ref[0, jb], jnp.int32)
        gt = bb > thr
        eq = bb == thr
        eq_before = _dot(jnp.where(eq, 1.0, 0.0).astype(BF16), tri) + c_eq
        sel = gt | (eq & (eq_before < need))
        sel_f = jnp.where(sel, 1.0, 0.0)
        before = _dot(sel_f.astype(BF16), tri) + c_pos
        pos_ref[0, jb] = jnp.where(sel, before, -1.0).astype(jnp.int32)
        off_ref[0, jb] = jnp.broadcast_to(c_pos, (N_EXPERTS, LANES)).astype(jnp.int32)
        c_eq = c_eq + jnp.sum(jnp.where(eq, 1.0, 0.0), axis=1, keepdims=True)
        c_pos = c_pos + jnp.sum(sel_f, axis=1, keepdims=True)
        return c_eq, c_pos

    zero = jnp.zeros((N_EXPERTS, 1), F32)
    lax.fori_loop(0, nb, blk, (zero, zero))


def route(logits, tri, cap):
    b, nb, e, t = logits.shape
    kern = functools.partial(_route_kernel, cap=cap)
    spec = pl.BlockSpec((1, nb, e, t), lambda i: (i, 0, 0, 0))
    return pl.pallas_call(
        kern,
        grid=(b,),
        in_specs=[spec, pl.BlockSpec(tri.shape, lambda i: (0, 0))],
        out_specs=[spec, spec, pl.BlockSpec((1, nb, e, LANES), lambda i: (i, 0, 0, 0))],
        out_shape=[jax.ShapeDtypeStruct((b, nb, e, t), F32), jax.ShapeDtypeStruct((b, nb, e, t), jnp.int32),
                   jax.ShapeDtypeStruct((b, nb, e, LANES), jnp.int32)],
        compiler_params=_cparams(("arbitrary",)),
        name="route",
    )(logits, tri)


SLOT_ALIGN = 16
SLOT_WIN = 128


def _for_each_slot_window(off_ref, idx0, nsub, st, cap, fn):
    n_extra = min(-(-(ROUTE_T + SLOT_ALIGN) // st), cap // st) - 1
    wins = []
    for jj in range(nsub):
        off, nxt = off_ref[idx0 + jj], off_ref[idx0 + jj + 1]
        start = pl.multiple_of(jnp.minimum(off // SLOT_ALIGN * SLOT_ALIGN, cap - st), SLOT_ALIGN)
        fn(jj, start, None)
        wins.append((start, nxt))
    if n_extra == 0:
        return

    @pl.when(functools.reduce(jnp.logical_or, [nxt > start + st for start, nxt in wins]))
    def _():
        for jj, (start, nxt) in enumerate(wins):
            for w in range(1, n_extra + 1):
                @pl.when(nxt > start + w * st)
                def _():
                    fn(jj, pl.multiple_of(jnp.minimum(start + w * st, cap - st), SLOT_ALIGN), start + w * st)


def _slot_one_hot(slot, prow, start, lower, st):
    hit = slot == (prow - start)
    return hit if lower is None else hit & (prow >= lower) & (prow < lower + st)


GATHER_EXPERTS = 2


def _gather_kernel(off_ref, h_ref, pos_ref, aff_ref, xe_ref, gs_ref, acc_ref, gacc_ref, *, st, cap, nsub, nb):
    bi, eg, ch = pl.program_id(0), pl.program_id(1), pl.program_id(2)

    @pl.when(ch == 0)
    def _():
        acc_ref[...] = jnp.zeros(acc_ref.shape, F32)
        gacc_ref[...] = jnp.zeros(gacc_ref.shape, F32)

    slot = lax.broadcasted_iota(jnp.int32, (st, ROUTE_T), 0)
    for g in range(GATHER_EXPERTS):
        e = eg * GATHER_EXPERTS + g

        def add(jj, start, lower, g=g, e=e):
            hit = _slot_one_hot(slot, pos_ref[0, jj, pl.ds(e, 1), :], start, lower, st)
            acc_ref[g, pl.ds(start, st), :] += _dot(jnp.where(hit, 1.0, 0.0).astype(BF16),
                                                    h_ref[0, jj * ROUTE_T:(jj + 1) * ROUTE_T, :])
            gacc_ref[g, pl.ds(start, st), :] += jnp.sum(jnp.where(hit, aff_ref[0, jj, pl.ds(e, 1), :], 0.0),
                                                        axis=1, keepdims=True)

        _for_each_slot_window(off_ref, (bi * N_EXPERTS + e) * (nb + 1) + ch * nsub, nsub, st, cap, add)

    @pl.when(ch == pl.num_programs(2) - 1)
    def _():
        xe_ref[0] = acc_ref[...].astype(BF16)
        gs_ref[0] = gacc_ref[...]


def moe_gather(offs, h, pos, aff, cap):
    b, n, d = h.shape
    nb = n // ROUTE_T
    st = min(SLOT_WIN, cap)
    chunk = min(2048, n)
    nsub = chunk // ROUTE_T
    kern = functools.partial(_gather_kernel, st=st, cap=cap, nsub=nsub, nb=nb)
    rspec = pl.BlockSpec((1, nsub, N_EXPERTS, ROUTE_T), lambda i, e, c, off: (i, c, 0, 0))
    return pl.pallas_call(
        kern,
        grid_spec=pltpu.PrefetchScalarGridSpec(
            num_scalar_prefetch=1,
            grid=(b, N_EXPERTS // GATHER_EXPERTS, n // chunk),
            in_specs=[pl.BlockSpec((1, chunk, d), lambda i, e, c, off: (i, c, 0)), rspec, rspec],
            out_specs=[pl.BlockSpec((1, GATHER_EXPERTS, cap, d), lambda i, e, c, off: (i, e, 0, 0)),
                       pl.BlockSpec((1, GATHER_EXPERTS, cap, 1), lambda i, e, c, off: (i, e, 0, 0))],
            scratch_shapes=[pltpu.VMEM((GATHER_EXPERTS, cap, d), F32), pltpu.VMEM((GATHER_EXPERTS, cap, 1), F32)]),
        out_shape=[jax.ShapeDtypeStruct((b, N_EXPERTS, cap, d), BF16),
                   jax.ShapeDtypeStruct((b, N_EXPERTS, cap, 1), F32)],
        compiler_params=_cparams(("parallel", "parallel", "arbitrary")),
        name="moe_gather",
    )(offs, h, pos, aff)


FF_TILE = 256


def _ffn_kernel(x_ref, g_ref, wg_ref, wu_ref, wd_ref, y_ref, acc_ref):
    f = pl.program_id(2)

    @pl.when(f == 0)
    def _():
        acc_ref[...] = jnp.zeros(acc_ref.shape, F32)

    x = x_ref[0, 0]
    gate = _dot(x, wg_ref[0, 0].astype(BF16))
    up = _dot(x, wu_ref[0, 0].astype(BF16))
    hmid = (_silu(gate) * up).astype(BF16)
    acc_ref[...] += _dot(hmid, wd_ref[0, 0].astype(BF16))

    @pl.when(f == pl.num_programs(2) - 1)
    def _():
        y_ref[0, 0] = (acc_ref[...] * g_ref[0, 0]).astype(BF16)


def moe_ffn(xe, gs, w_gate, w_up, w_down, layer):
    b, e, cap, d = xe.shape
    dff = w_gate.shape[3]
    return pl.pallas_call(
        _ffn_kernel,
        grid=(e, b, dff // FF_TILE),
        in_specs=[pl.BlockSpec((1, 1, cap, d), lambda ei, bi, f: (bi, ei, 0, 0)),
                  pl.BlockSpec((1, 1, cap, 1), lambda ei, bi, f: (bi, ei, 0, 0)),
                  pl.BlockSpec((1, 1, d, FF_TILE), lambda ei, bi, f: (layer, ei, 0, f)),
                  pl.BlockSpec((1, 1, d, FF_TILE), lambda ei, bi, f: (layer, ei, 0, f)),
                  pl.BlockSpec((1, 1, FF_TILE, d), lambda ei, bi, f: (layer, ei, f, 0))],
        out_specs=pl.BlockSpec((1, 1, cap, d), lambda ei, bi, f: (bi, ei, 0, 0)),
        out_shape=jax.ShapeDtypeStruct((b, e, cap, d), BF16),
        scratch_shapes=[pltpu.VMEM((cap, d), F32)],
        compiler_params=_cparams(("parallel", "parallel", "arbitrary")),
        name="moe_ffn",
    )(xe, gs, w_gate, w_up, w_down)


def _combine_kernel(off_ref, x_ref, ye_ref, pos_ref, mod_ref, o_ref, *, st, cap, nsub, nb):
    bi, ch, e = pl.program_id(0), pl.program_id(1), pl.program_id(2)

    @pl.when(e == 0)
    def _():
        o_ref[...] = jnp.zeros(o_ref.shape, F32)

    slot = lax.broadcasted_iota(jnp.int32, (st, ROUTE_T), 0)

    def add(jj, start, lower):
        hit = _slot_one_hot(slot, pos_ref[0, jj, pl.ds(e, 1), :], start, lower, st)
        oh = jnp.where(hit, 1.0, 0.0).astype(BF16)
        o_ref[0, jj * ROUTE_T:(jj + 1) * ROUTE_T, :] += lax.dot_general(
            oh, ye_ref[0, 0, pl.ds(start, st), :], (((0,), (0,)), ((), ())), preferred_element_type=F32)

    _for_each_slot_window(off_ref, (bi * N_EXPERTS + e) * (nb + 1) + ch * nsub, nsub, st, cap, add)

    @pl.when(e == pl.num_programs(2) - 1)
    def _():
        o_ref[0] = x_ref[0] + mod_ref[0, 5:6, :] * o_ref[0]


def moe_combine(offs, x, ye, pos, mod, cap, slot0):
    b, n, d = x.shape
    nb = n // ROUTE_T
    st = min(SLOT_WIN, cap)
    chunk = min(2048, n)
    nsub = chunk // ROUTE_T
    assert slot0 % cap == 0
    kern = functools.partial(_combine_kernel, st=st, cap=cap, nsub=nsub, nb=nb)
    return pl.pallas_call(
        kern,
        grid_spec=pltpu.PrefetchScalarGridSpec(
            num_scalar_prefetch=1,
            grid=(b, n // chunk, N_EXPERTS),
            in_specs=[pl.BlockSpec((1, chunk, d), lambda i, c, e, off: (i, c, 0)),
                      pl.BlockSpec((1, 1, cap, d), lambda i, c, e, off: (i, e, slot0 // cap, 0)),
                      pl.BlockSpec((1, nsub, N_EXPERTS, ROUTE_T), lambda i, c, e, off: (i, c, 0, 0)),
                      pl.BlockSpec((1, 8, d), lambda i, c, e, off: (i, 0, 0))],
            out_specs=pl.BlockSpec((1, chunk, d), lambda i, c, e, off: (i, c, 0))),
        out_shape=jax.ShapeDtypeStruct((b, n, d), F32),
        compiler_params=_cparams(("parallel", "parallel", "arbitrary")),
        name="moe_combine",
    )(offs, x, ye, pos, mod)


def expert_choice_ffn(sets, tri, w_gate, w_up, w_down, layer):
    routed = []
    for x, h, logits, mod in sets:
        b, n, _ = x.shape
        cap = EC_CAPACITY * n // N_EXPERTS
        aff, pos, off = route(logits, tri, cap)
        first = jnp.transpose(off[..., 0], (0, 2, 1))
        offs = jnp.concatenate([first, jnp.full((b, N_EXPERTS, 1), cap, jnp.int32)], axis=2).reshape(-1)
        xe, gs = moe_gather(offs, h, pos, aff, cap)
        routed.append((offs, pos, cap, xe, gs))
    xe_all = jnp.concatenate([r[3] for r in routed], axis=2) if len(routed) > 1 else routed[0][3]
    gs_all = jnp.concatenate([r[4] for r in routed], axis=2) if len(routed) > 1 else routed[0][4]
    ye = moe_ffn(xe_all, gs_all, w_gate, w_up, w_down, layer)
    outs, slot0 = [], 0
    for (x, _, _, mod), (offs, pos, cap, _, _) in zip(sets, routed):
        outs.append(moe_combine(offs, x, ye, pos, mod, cap, slot0))
        slot0 += cap
    return outs


def _rope_tables(n_rows_grid):
    s = n_rows_grid * GRID_W
    t = np.arange(s)
    row, col = (t // GRID_W).astype(np.float64), (t % GRID_W).astype(np.float64)

    def unit(n):
        half = n // 2
        inv = ROPE_THETA ** (-np.arange(0, n, 2, dtype=np.float64) / n)
        out = []
        for pos in (row, col):
            ang = pos[:, None] * inv[None, :]
            c, sn = np.cos(ang), np.sin(ang)
            z = np.zeros_like(sn)
            out.append((np.concatenate([c, c], 1), np.concatenate([-sn, z], 1), np.concatenate([z, sn], 1)))
        return [np.concatenate([out[0][i], out[1][i]], 1) for i in range(3)]

    ident = lambda w: (np.ones((s, w), np.float32), np.zeros((s, w), np.float32), np.zeros((s, w), np.float32))
    u16 = unit(16)
    u32 = unit(32)
    tabs = []
    idt = ident(64)
    idt32 = ident(32)
    for i in range(3):
        tabs.append(np.concatenate([idt[i], u16[i], idt32[i]], 1))
    for i in range(3):
        tabs.append(np.concatenate([u16[i]] * 4, 1))
    for i in range(3):
        tabs.append(np.concatenate([u32[i]] * 2, 1))
    return np.stack(tabs).astype(np.float32)


def _identity_tables(n):
    one, zero = np.ones((n, LANES), np.float32), np.zeros((n, LANES), np.float32)
    return np.stack([one, zero, zero] * 3)


def _seg_matrices():
    i = np.arange(LANES)
    ones = np.ones((LANES, LANES), np.float32)
    m64 = (i[:, None] // 64 == i[None, :] // 64).astype(np.float32)
    m32 = (i[:, None] // 32 == i[None, :] // 32).astype(np.float32)
    return np.stack([ones, m64, m32])


def _na_bias(rpb):
    c = np.arange(GRID_W)
    rl = np.arange(NA_QROWS)
    dc = np.clip(c[None, :] - c[:, None] + NA_WIN_W - 1, 0, 2 * NA_WIN_W - 2)
    d = np.array([-1, 0, 1])
    dr = np.clip(NA_QROWS * d[:, None, None] + rl[None, None, :] - rl[None, :, None] + NA_WIN_H - 1,
                 0, 2 * NA_WIN_H - 2)
    cols = jnp.take(rpb.astype(F32) * LOG2E, jnp.asarray(dc), axis=2)
    full = jnp.take(cols, jnp.asarray(dr), axis=1)
    return jnp.transpose(full, (0, 1, 2, 4, 3, 5)).reshape(rpb.shape[0], 3, NA_TQ, NA_TQ)


def _pad_cols(w, width):
    return jnp.pad(w, ((0, 0), (0, width - w.shape[1])))


def _layer_params(l, w_in, g_na_q, g_na_k, na_rpb, g_mla_cq, w_mla_uq, g_mla_q, g_mla_ckv, w_mla_ukv,
                  g_mla_k_nope, g_mla_k_rope, g_diff_q, g_diff_k, g_diff_sub, g_gqa_q, g_gqa_k, w_out):
    wi = w_in[l]
    (naq, nak, nav, cq, ckv, kr, dq, dk, dv, gq, gk, gv) = jnp.split(
        wi, np.cumsum([256, 256, 256, 256, 128, 32, 256, 256, 256, 256, 128])[:], axis=1)
    gq4 = gq.reshape(-1, 4, 64)[:, jnp.array([0, 2, 1, 3])].reshape(-1, 256)
    zeros = lambda w: jnp.zeros((wi.shape[0], w), wi.dtype)
    w_in_r = jnp.concatenate([naq, nak, nav, cq, ckv, dq, dk, dv, gq4, gk, gv, zeros(64), kr, zeros(32)],
                             axis=1).astype(BF16)
    uq = w_mla_uq[l].reshape(MLA_Q_RANK, 4, 96)
    wuq = jnp.pad(uq, ((0, 0), (0, 0), (0, 32))).reshape(MLA_Q_RANK, 512).astype(BF16)
    ukv = w_mla_ukv[l].reshape(MLA_KV_RANK, 4, 128)
    wuk = jnp.pad(ukv[:, :, :64], ((0, 0), (0, 0), (0, 64))).reshape(MLA_KV_RANK, 512).astype(BF16)
    wuv = ukv[:, :, 64:].reshape(MLA_KV_RANK, 256).astype(BF16)
    row = lambda v: jnp.pad(v, (0, 512 - v.shape[0]))
    z32, z64 = jnp.zeros((32,), F32), jnp.zeros((64,), F32)
    gains = jnp.stack([
        row(jnp.tile(g_na_q[l], 4)), row(jnp.tile(g_na_k[l], 4)), row(g_mla_cq[l]),
        row(jnp.tile(jnp.concatenate([g_mla_q[l], z32]), 4)), row(g_mla_ckv[l]),
        row(jnp.tile(jnp.concatenate([g_mla_k_nope[l], z64]), 4)),
        row(jnp.concatenate([z64, g_mla_k_rope[l], z32])),
        row(jnp.tile(g_diff_q[l].reshape(-1), 4)), row(jnp.tile(g_diff_k[l].reshape(-1), 4)),
        row(jnp.tile(g_gqa_q[l], 4)), row(jnp.tile(g_gqa_k[l], 2))] + [jnp.zeros((512,), F32)] * 5)
    wo = w_out[l]
    wo_g = wo[768:].reshape(4, 64, -1)[jnp.array([0, 2, 1, 3])].reshape(256, -1)
    w_out_r = jnp.concatenate([wo[:768], wo_g], axis=0).astype(BF16)
    return dict(w_in=w_in_r, wuq=wuq, wuk=wuk, wuv=wuv, gains=gains, w_out=w_out_r,
                bias=_na_bias(na_rpb[l]), g_sub=jnp.tile(g_diff_sub[l], 2).reshape(1, LANES))


def kernel(x, c, ctx, c_ctx, w_mod, b_mod, g_attn, g_ffn, w_in, g_na_q, g_na_k, na_rpb, g_mla_cq, w_mla_uq, g_mla_q, g_mla_ckv, w_mla_ukv, g_mla_k_nope, g_mla_k_rope, g_diff_q, g_diff_k, diff_lambda, g_diff_sub, g_gqa_q, g_gqa_k, w_out, w_router, w_gate, w_up, w_down):
    b, s, d = x.shape
    n_ctx = ctx.shape[1]
    rows = s // GRID_W
    tabs = jnp.asarray(_rope_tables(rows))
    tabs_ctx = jnp.asarray(_identity_tables(n_ctx))
    mats = jnp.asarray(_seg_matrices()).astype(BF16)
    tri = jnp.asarray(np.triu(np.ones((ROUTE_T, ROUTE_T), np.float32), 1)).astype(BF16)
    c_rows = jnp.concatenate([c, c_ctx[None], jnp.zeros((8 - b - 1, d), F32)], axis=0)
    xc = ctx
    for l in range(DEPTH):
        need_ctx = l < DEPTH - 1
        lam_init = 0.8 - 0.6 * math.exp(-0.3 * l)
        lp = diff_lambda[l].astype(F32)
        lam = (jnp.exp(jnp.sum(lp[0] * lp[1])) - jnp.exp(jnp.sum(lp[2] * lp[3])) + lam_init).reshape(1, 1)
        prm = _layer_params(l, w_in, g_na_q, g_na_k, na_rpb, g_mla_cq, w_mla_uq, g_mla_q, g_mla_ckv, w_mla_ukv,
                            g_mla_k_nope, g_mla_k_rope, g_diff_q, g_diff_k, g_diff_sub, g_gqa_q, g_gqa_k, w_out)
        modv = mod_vectors(c_rows, w_mod[l], b_mod[l]).reshape(8, N_MOD, d)
        mod = jnp.pad(modv[:b], ((0, 0), (0, 2), (0, 0)))
        mod_c = jnp.broadcast_to(jnp.pad(modv[b:b + 1], ((0, 0), (0, 2), (0, 0))), (b, 8, d))
        w_router_t = w_router[l].T

        def mix_inputs(xin, m, tb):
            return proj_prep(xin, g_attn[l], m[:, 0:1], m[:, 1:2], prm["w_in"], tb, prm["gains"], mats,
                             prm["wuq"], prm["wuk"], prm["wuv"])

        (naq, nak, nav, mq, mk, mv, dq, dk, dv, gq, gk, gv, stat) = mix_inputs(x, mod, tabs)
        (naq_c, nak_c, nav_c, mq_c, mk_c, mv_c, dq_c, dk_c, dv_c, gq_c, gk_c, gv_c, stat_c) = mix_inputs(
            xc, mod_c, tabs_ctx)
        cat = lambda a, bb: jnp.concatenate([a, bb], axis=1)
        n2 = jnp.max(stat, axis=(1, 3))
        n2c = jnp.max(stat_c, axis=(1, 3))
        q2 = lambda name: n2[:, STAT_ROWS.index(name)]
        k2 = lambda name: jnp.maximum(n2[:, STAT_ROWS.index(name)], n2c[:, STAT_ROWS.index(name)])

        ext = values_transposed
        o_na = na_attention(naq, nak, nav, nak_c, nav_c, prm["bias"], rows)
        o_mla = attention(q2("mq"), k2("mk"), k2("mv"), mq, cat(mk_c, mk), ext(cat(mv_c, mv)), HEADS_MLA)
        o_diff = attention(q2("dq"), k2("dk"), k2("dv"), dq, cat(dk_c, dk), ext(cat(dv_c, dv)), HEADS_DIFF)
        o_gqa = attention(q2("gq"), k2("gk"), k2("gv"), gq, cat(gk_c, gk), ext(cat(gv_c, gv)), HEADS_GQA)
        x_mid, h2, logits = outproj_mod_router(x, o_na, o_mla, o_diff, o_gqa, mod, prm["g_sub"], g_ffn[l], lam,
                                               mats, prm["w_out"], w_router_t, lam_init)
        if need_ctx:
            oc_na = flash_attention(naq_c, nak_c, ext(nav_c), HEADS_PAIRED)
            oc_mla = flash_attention(mq_c, mk_c, ext(mv_c), HEADS_MLA)
            oc_diff = flash_attention(dq_c, dk_c, ext(dv_c), HEADS_DIFF)
            oc_gqa = flash_attention(gq_c, gk_c, ext(gv_c), HEADS_GQA)
            xc_mid, hc2, logits_c = outproj_mod_router(xc, oc_na, oc_mla, oc_diff, oc_gqa, mod_c, prm["g_sub"],
                                                       g_ffn[l], lam, mats, prm["w_out"], w_router_t, lam_init)
            x, xc = expert_choice_ffn([(x_mid, h2, logits, mod), (xc_mid, hc2, logits_c, mod_c)],
                                      tri, w_gate, w_up, w_down, l)
        else:
            (x,) = expert_choice_ffn([(x_mid, h2, logits, mod)], tri, w_gate, w_up, w_down, l)
    return x
```

```python
import functools
import math

import numpy as np
import jax
import jax.numpy as jnp
from jax import lax
from jax.experimental import pallas as pl
from jax.experimental.pallas import tpu as pltpu

D_MODEL = 1024
GRID_W = 64
HEAD_DIM = 64
N_HEADS = 4
NA_WIN_H = 8
NA_WIN_W = 16
MLA_Q_RANK = 256
MLA_KV_RANK = 128
MLA_NOPE_DIM = 64
MLA_ROPE_DIM = 32
MLA_V_DIM = 64
DIFF_QK_DIM = 32
N_EXPERTS = 16
EC_CAPACITY = 2
D_FF = 2816
ROPE_THETA = 10000.0
NORM_EPS = 1e-6
N_MOD = 6
DEPTH = 2

LANES = 128
P_COLS = 2560
VMEM_LIMIT = 56 * 1024 * 1024
NEG_BIG = -1e30
LOG2E = math.log2(math.e)

F32 = jnp.float32
BF16 = jnp.bfloat16


def _cparams(sem):
    return pltpu.CompilerParams(dimension_semantics=sem, vmem_limit_bytes=VMEM_LIMIT)


def _split(a):
    hi = a.astype(BF16)
    lo = (a - hi.astype(F32)).astype(BF16)
    return hi, lo


def _dot(a, b):
    return jnp.dot(a, b, preferred_element_type=F32)


def _dot_nt(a, b):
    return lax.dot_general(a, b, (((1,), (1,)), ((), ())), preferred_element_type=F32)


def _dot3(a, b):
    ah, al = _split(a)
    bh, bl = _split(b)
    return _dot(ah, bh) + _dot(ah, bl) + _dot(al, bh)


def _dot3_nt(a, b):
    ah, al = _split(a)
    bh, bl = _split(b)
    return _dot_nt(ah, bh) + _dot_nt(ah, bl) + _dot_nt(al, bh)


def _silu(v):
    return v * jax.nn.sigmoid(v)


def _mod_kernel(c_ref, w_ref, b_ref, o_ref):
    o_ref[...] = _dot3(_silu(c_ref[...]), w_ref[...]) + b_ref[...]


def mod_vectors(c_rows, w_mod, b_mod):
    m, d = c_rows.shape
    n = w_mod.shape[1]
    tn = 1536
    return pl.pallas_call(
        _mod_kernel,
        grid=(n // tn,),
        in_specs=[pl.BlockSpec((m, d), lambda j: (0, 0)),
                  pl.BlockSpec((d, tn), lambda j: (0, j)),
                  pl.BlockSpec((1, tn), lambda j: (0, j))],
        out_specs=pl.BlockSpec((m, tn), lambda j: (0, j)),
        out_shape=jax.ShapeDtypeStruct((m, n), F32),
        compiler_params=_cparams(("arbitrary",)),
        name="mod_vectors",
    )(c_rows, w_mod, b_mod.reshape(1, n))


P_NAQ, P_NAK, P_NAV, P_CQ, P_CKV = 0, 256, 512, 768, 1024
P_DQ, P_DK, P_DV, P_GQ, P_GK, P_GV, P_KR = 1152, 1408, 1664, 1920, 2176, 2304, 2432
(G_NAQ, G_NAK, G_CQ, G_MQ, G_CKV, G_KN, G_KR, G_DQ, G_DK, G_GQ, G_GK) = range(11)
M_ONES, M_SEG64, M_SEG32 = 0, 1, 2
T_MLA, T_DIFF, T_GQA = 0, 3, 6


def _seg_norm(x, mat, inv_n, g):
    sq = x * x
    hi, lo = _split(sq)
    ms = (_dot(hi, mat) + _dot(lo, mat)) * inv_n
    return x * lax.rsqrt(ms + NORM_EPS) * g


def _rope(x, tab_ref, t0, half):
    c, s1, s2 = tab_ref[t0], tab_ref[t0 + 1], tab_ref[t0 + 2]
    return (x * c + pltpu.roll(x, LANES - half, axis=1) * s1
            + pltpu.roll(x, half, axis=1) * s2)


def _prep_kernel(x_ref, g_ref, sh_ref, sc_ref, w_ref, tab_ref, gain_ref, mat_ref, wuq_ref, wuk_ref, wuv_ref,
                 naq_ref, nak_ref, nav_ref, mq_ref, mk_ref, mv_ref,
                 dq_ref, dk_ref, dv_ref, gq_ref, gk_ref, gv_ref, st_ref):
    lane = lax.broadcasted_iota(jnp.int32, (1, LANES), 1)
    lo_half = lane < 64
    ones_m, m64, m32 = mat_ref[M_ONES], mat_ref[M_SEG64], mat_ref[M_SEG32]
    norm2 = {}

    x = x_ref[0]
    y = x * lax.rsqrt(jnp.mean(x * x, axis=-1, keepdims=True) + NORM_EPS) * g_ref[...]
    p = _dot((y * (1.0 + sc_ref[0]) + sh_ref[0]).astype(BF16), w_ref[...])

    def slab(off, i):
        return p[:, off + i * LANES: off + (i + 1) * LANES]

    def track(name, xb, mat):
        xf = xb.astype(F32)
        hi, lo = _split(xf * xf)
        cur = jnp.max(_dot(hi, mat) + _dot(lo, mat), axis=0, keepdims=True)
        norm2[name] = jnp.maximum(norm2[name], cur) if name in norm2 else cur
        return xb

    def track_elem(name, xb):
        for i in range(xb.shape[1] // LANES):
            xf = xb[:, i * LANES:(i + 1) * LANES].astype(F32)
            cur = jnp.max(xf * xf, axis=0, keepdims=True)
            norm2[name] = jnp.maximum(norm2[name], cur) if name in norm2 else cur
        return xb

    def gain(row, i):
        return gain_ref[row:row + 1, i * LANES:(i + 1) * LANES]

    s_na = HEAD_DIM ** -0.5 * LOG2E
    for i in range(2):
        q = _seg_norm(slab(P_NAQ, i), m64, 1.0 / 64, gain(G_NAQ, i)) * s_na
        naq_ref[0, :, (2 * i) * LANES:(2 * i + 1) * LANES] = jnp.where(lo_half, q, 0.0).astype(BF16)
        naq_ref[0, :, (2 * i + 1) * LANES:(2 * i + 2) * LANES] = jnp.where(lo_half, 0.0, q).astype(BF16)
        k = _seg_norm(slab(P_NAK, i), m64, 1.0 / 64, gain(G_NAK, i))
        nak_ref[0, :, i * LANES:(i + 1) * LANES] = k.astype(BF16)
        nav_ref[0, :, i * LANES:(i + 1) * LANES] = slab(P_NAV, i).astype(BF16)

    cq = p[:, P_CQ:P_CQ + MLA_Q_RANK]
    cq = cq * lax.rsqrt(jnp.mean(cq * cq, axis=-1, keepdims=True) + NORM_EPS) * gain_ref[G_CQ:G_CQ + 1, :MLA_Q_RANK]
    uq = _dot(cq.astype(BF16), wuq_ref[...])
    s_mla = (MLA_NOPE_DIM + MLA_ROPE_DIM) ** -0.5 * LOG2E
    ckv = p[:, P_CKV:P_CKV + MLA_KV_RANK]
    ckv = ckv * lax.rsqrt(jnp.mean(ckv * ckv, axis=-1, keepdims=True) + NORM_EPS) * gain_ref[G_CKV:G_CKV + 1, :MLA_KV_RANK]
    ckv_b = ckv.astype(BF16)
    uk = _dot(ckv_b, wuk_ref[...])
    mv_ref[0] = track_elem("mv", _dot(ckv_b, wuv_ref[...]).astype(BF16))
    kr = _seg_norm(slab(P_KR, 0), ones_m, 1.0 / MLA_ROPE_DIM, gain(G_KR, 0))
    kr = _rope(kr, tab_ref, T_MLA, 8)
    for h in range(N_HEADS):
        q = _seg_norm(uq[:, h * LANES:(h + 1) * LANES], ones_m, 1.0 / (MLA_NOPE_DIM + MLA_ROPE_DIM), gain(G_MQ, h))
        q = _rope(q, tab_ref, T_MLA, 8) * s_mla
        mq_ref[0, :, h * LANES:(h + 1) * LANES] = track("mq", q.astype(BF16), ones_m)
        kn = _seg_norm(uk[:, h * LANES:(h + 1) * LANES], ones_m, 1.0 / MLA_NOPE_DIM, gain(G_KN, h))
        mk_ref[0, :, h * LANES:(h + 1) * LANES] = track("mk", (kn + kr).astype(BF16), ones_m)

    s_d = DIFF_QK_DIM ** -0.5 * LOG2E
    seg = lane >> 5
    for i in range(2):
        q = _seg_norm(slab(P_DQ, i), m32, 1.0 / 32, gain(G_DQ, i))
        q = _rope(q, tab_ref, T_DIFF, 8) * s_d
        track("dq", q.astype(BF16), m32)
        for j in range(4):
            dq_ref[0, :, (4 * i + j) * LANES:(4 * i + j + 1) * LANES] = jnp.where(seg == j, q, 0.0).astype(BF16)
        k = _seg_norm(slab(P_DK, i), m32, 1.0 / 32, gain(G_DK, i))
        dk_ref[0, :, i * LANES:(i + 1) * LANES] = track("dk", _rope(k, tab_ref, T_DIFF, 8).astype(BF16), m32)
        dv_ref[0, :, i * LANES:(i + 1) * LANES] = track_elem("dv", slab(P_DV, i).astype(BF16))

    s_g = HEAD_DIM ** -0.5 * LOG2E
    for i in range(2):
        q = _seg_norm(slab(P_GQ, i), m64, 1.0 / 64, gain(G_GQ, i))
        q = _rope(q, tab_ref, T_GQA, 16) * s_g
        track("gq", q.astype(BF16), m64)
        gq_ref[0, :, (2 * i) * LANES:(2 * i + 1) * LANES] = jnp.where(lo_half, q, 0.0).astype(BF16)
        gq_ref[0, :, (2 * i + 1) * LANES:(2 * i + 2) * LANES] = jnp.where(lo_half, 0.0, q).astype(BF16)
    k = _seg_norm(slab(P_GK, 0), m64, 1.0 / 64, gain(G_GK, 0))
    gk_ref[0] = track("gk", _rope(k, tab_ref, T_GQA, 16).astype(BF16), m64)
    gv_ref[0] = track_elem("gv", slab(P_GV, 0).astype(BF16))
    st_ref[0, 0] = jnp.concatenate([norm2[n] for n in STAT_ROWS]
                                   + [jnp.zeros((STAT_PAD - len(STAT_ROWS), LANES), F32)], axis=0)


PREP_WIDTHS = (512, 256, 256, 512, 512, 256, 1024, 256, 256, 512, 128, 128)
STAT_ROWS = ("mq", "mk", "mv", "dq", "dk", "dv", "gq", "gk", "gv")
STAT_PAD = 16


def proj_prep(x, g, shift, scale, w_in, tabs, gains, mats, wuq, wuk, wuv):
    b, n, d = x.shape
    tm = min(512, n)
    const2 = lambda i, j: (0, 0)
    stat_spec = pl.BlockSpec((1, 1, STAT_PAD, LANES), lambda i, j: (i, j, 0, 0))
    stat_shape = jax.ShapeDtypeStruct((b, n // tm, STAT_PAD, LANES), F32)
    return pl.pallas_call(
        _prep_kernel,
        grid=(b, n // tm),
        in_specs=[pl.BlockSpec((1, tm, d), lambda i, j: (i, j, 0)),
                  pl.BlockSpec((1, d), const2),
                  pl.BlockSpec((1, 1, d), lambda i, j: (i, 0, 0)),
                  pl.BlockSpec((1, 1, d), lambda i, j: (i, 0, 0)),
                  pl.BlockSpec((d, P_COLS), const2),
                  pl.BlockSpec((9, tm, LANES), lambda i, j: (0, j, 0)),
                  pl.BlockSpec(gains.shape, const2),
                  pl.BlockSpec(mats.shape, lambda i, j: (0, 0, 0)),
                  pl.BlockSpec(wuq.shape, const2),
                  pl.BlockSpec(wuk.shape, const2),
                  pl.BlockSpec(wuv.shape, const2)],
        out_specs=[pl.BlockSpec((1, tm, w), lambda i, j: (i, j, 0)) for w in PREP_WIDTHS] + [stat_spec],
        out_shape=[jax.ShapeDtypeStruct((b, n, w), BF16) for w in PREP_WIDTHS] + [stat_shape],
        compiler_params=_cparams(("parallel", "arbitrary")),
        name="proj_prep",
    )(x, g.reshape(1, d), shift, scale, w_in, tabs, gains, mats, wuq, wuk, wuv)


VT_ROWS = LANES + 16


def _value_rows(heads, h):
    pair = [g for g in range(len(heads)) if heads[g][3] == heads[h][3]]
    return pair.index(h) * 64, pair.index(h) * 64 + 64
KEY_CHUNK = 256
FLASH_TQ = 512
FLASH_TK_MAX = 3584


def _flash_kernel(q_ref, k_ref, vt_ref, o_ref, m_ref, l_ref, acc_ref, *, heads, n_out):
    ki = pl.program_id(2)

    @pl.when(ki == 0)
    def _():
        m_ref[...] = jnp.full(m_ref.shape, NEG_BIG, F32)
        l_ref[...] = jnp.zeros(l_ref.shape, F32)
        acc_ref[...] = jnp.zeros(acc_ref.shape, F32)

    for o in range(n_out):
        pair = [h for h in range(len(heads)) if heads[h][3] == o]
        for pos, h in enumerate(pair):
            qs, ks, vs, _ = heads[h]
            q = q_ref[0, :, qs * LANES:(qs + 1) * LANES]
            k = k_ref[0, :, ks * LANES:(ks + 1) * LANES]
            st = _dot_nt(k, q)
            lo, hi = _value_rows(heads, h)
            m, l, acc = m_ref[h], l_ref[h], acc_ref[o, lo:hi, :]
            for c0 in range(0, st.shape[0], KEY_CHUNK):
                rows = slice(c0, min(c0 + KEY_CHUNK, st.shape[0]))
                sc = st[rows]
                m_new = jnp.maximum(m, jnp.max(sc, axis=0, keepdims=True))
                a = jnp.exp2(m - m_new)
                pt = jnp.exp2(sc - m_new).astype(BF16)
                r = _dot(vt_ref[0, vs, :, rows], pt)
                l = a * l + r[LANES:LANES + 1]
                acc = acc * a + r[lo:hi]
                m = m_new
            m_ref[h], l_ref[h] = m, l
            acc_ref[o, lo:hi, :] = acc

    @pl.when(ki == pl.num_programs(2) - 1)
    def _():
        for o in range(n_out):
            pair = [h for h in range(len(heads)) if heads[h][3] == o]
            out_t = jnp.concatenate([acc_ref[o, 0:64, :] / l_ref[pair[0]],
                                     acc_ref[o, 64:128, :] / l_ref[pair[1]]], axis=0)
            o_ref[0, :, o * LANES:(o + 1) * LANES] = out_t.T


def _pick_tk(n_keys):
    best = 256
    for t in range(256, FLASH_TK_MAX + 1, 256):
        if n_keys % t == 0:
            best = t
    return best


def values_transposed(v):
    b, n, w = v.shape
    vt = jnp.transpose(v.reshape(b, n, w // LANES, LANES), (0, 2, 3, 1))
    return jnp.concatenate([vt, jnp.ones((b, w // LANES, VT_ROWS - LANES, n), v.dtype)], axis=2)


def flash_attention(q, k, vt, heads):
    b, s, qw = q.shape
    nk = k.shape[1]
    n_out = max(h[3] for h in heads) + 1
    tq = min(FLASH_TQ, s)
    tk = _pick_tk(nk)
    kern = functools.partial(_flash_kernel, heads=heads, n_out=n_out)
    return pl.pallas_call(
        kern,
        grid=(b, s // tq, nk // tk),
        in_specs=[pl.BlockSpec((1, tq, qw), lambda i, j, t: (i, j, 0)),
                  pl.BlockSpec((1, tk, k.shape[2]), lambda i, j, t: (i, t, 0)),
                  pl.BlockSpec((1, vt.shape[1], VT_ROWS, tk), lambda i, j, t: (i, 0, 0, t))],
        out_specs=pl.BlockSpec((1, tq, n_out * LANES), lambda i, j, t: (i, j, 0)),
        out_shape=jax.ShapeDtypeStruct((b, s, n_out * LANES), F32),
        scratch_shapes=[pltpu.VMEM((len(heads), 1, tq), F32),
                        pltpu.VMEM((len(heads), 1, tq), F32),
                        pltpu.VMEM((n_out, LANES, tq), F32)],
        compiler_params=_cparams(("parallel", "parallel", "arbitrary")),
        name="flash_attention",
    )(q, k, vt)


STAB_SHIFT = 100.0
BOUND_LIMIT = 113.0
BOUND_SLACK = 1.01
SUM_LOG2_LIMIT = 126.0
BOUNDED_TQ = 1024


def _flash_bounded_kernel(stab_ref, q_ref, k_ref, vt_ref, o_ref, acc_ref, *, heads, n_out):
    ki = pl.program_id(2)

    @pl.when(ki == 0)
    def _():
        acc_ref[...] = jnp.zeros(acc_ref.shape, F32)

    stab = stab_ref[pl.program_id(0), 0]
    for h, (qs, ks, vs, _) in enumerate(heads):
        q = q_ref[0, :, qs * LANES:(qs + 1) * LANES]
        k = k_ref[0, :, ks * LANES:(ks + 1) * LANES]
        pt = jnp.exp2(_dot_nt(k, q) - stab).astype(BF16)
        acc_ref[h] += _dot(vt_ref[0, vs], pt)

    @pl.when(ki == pl.num_programs(2) - 1)
    def _():
        for o in range(n_out):
            pair = [h for h in range(len(heads)) if heads[h][3] == o]
            parts = []
            for h in pair:
                lo, hi = _value_rows(heads, h)
                parts.append(acc_ref[h, lo:hi, :] / acc_ref[h, LANES:LANES + 1, :])
            o_ref[0, :, o * LANES:(o + 1) * LANES] = jnp.concatenate(parts, axis=0).T


def flash_attention_bounded(stab, q, k, vt, heads):
    b, s, qw = q.shape
    nk = k.shape[1]
    n_out = max(h[3] for h in heads) + 1
    tq = min(BOUNDED_TQ, s)
    tk = _pick_tk(nk)
    kern = functools.partial(_flash_bounded_kernel, heads=heads, n_out=n_out)
    return pl.pallas_call(
        kern,
        grid=(b, s // tq, nk // tk),
        in_specs=[pl.BlockSpec(memory_space=pltpu.SMEM),
                  pl.BlockSpec((1, tq, qw), lambda i, j, t: (i, j, 0)),
                  pl.BlockSpec((1, tk, k.shape[2]), lambda i, j, t: (i, t, 0)),
                  pl.BlockSpec((1, vt.shape[1], VT_ROWS, tk), lambda i, j, t: (i, 0, 0, t))],
        out_specs=pl.BlockSpec((1, tq, n_out * LANES), lambda i, j, t: (i, j, 0)),
        out_shape=jax.ShapeDtypeStruct((b, s, n_out * LANES), F32),
        scratch_shapes=[pltpu.VMEM((len(heads), VT_ROWS, tq), F32)],
        compiler_params=_cparams(("parallel", "parallel", "arbitrary")),
        name="flash_attention_bounded",
    )(stab, q, k, vt)


def attention(q2, k2, v2, q, k, vt, heads):
    bound = BOUND_SLACK * jnp.sqrt(q2 * k2)
    stab = (bound - STAB_SHIFT).reshape(-1, 1).astype(F32)
    sum_log2 = STAB_SHIFT + math.log2(k.shape[1]) + 0.5 * jnp.log2(jnp.maximum(jnp.max(v2), 1.0))
    return lax.cond((jnp.max(bound) <= BOUND_LIMIT) & (sum_log2 <= SUM_LOG2_LIMIT),
                    lambda: flash_attention_bounded(stab, q, k, vt, heads),
                    lambda: flash_attention(q, k, vt, heads))


HEADS_PAIRED = tuple((h, h // 2, h // 2, h // 2) for h in range(4))
HEADS_MLA = tuple((h, h, h // 2, h // 2) for h in range(4))
HEADS_GQA = tuple((h, 0, 0, h // 2) for h in range(4))
HEADS_DIFF = tuple((j, j // 4, j // 4, (j % 2) * 2 + j // 4) for j in range(8))


NA_QROWS = 4
NA_TQ = NA_QROWS * GRID_W


def _na_kernel(q_ref, kp_ref, kc_ref, kn_ref, vp_ref, vc_ref, vn_ref, kx_ref, vx_ref, bias_ref, o_ref, *, rows):
    j = pl.program_id(1)
    kh = min(NA_WIN_H, rows)
    qi = lax.broadcasted_iota(jnp.int32, (NA_TQ, 1), 0)
    ki = lax.broadcasted_iota(jnp.int32, (1, NA_TQ), 1)
    wshift = GRID_W.bit_length() - 1
    r = j * NA_QROWS + (qi >> wshift)
    c = qi & (GRID_W - 1)
    r0 = jnp.clip(r - kh // 2, 0, rows - kh)
    c0 = jnp.clip(c - NA_WIN_W // 2, 0, GRID_W - NA_WIN_W)
    kcol = ki & (GRID_W - 1)
    col_ok = (kcol >= c0) & (kcol < c0 + NA_WIN_W)
    masks = []
    for d in (-1, 0, 1):
        kr = (j + d) * NA_QROWS + (ki >> wshift)
        masks.append(col_ok & (kr >= r0) & (kr < r0 + kh))
    lo_half = lax.broadcasted_iota(jnp.int32, (1, LANES), 1) < 64
    for o in range(2):
        k_loc = [kp_ref[0, :, o * LANES:(o + 1) * LANES], kc_ref[0, :, o * LANES:(o + 1) * LANES],
                 kn_ref[0, :, o * LANES:(o + 1) * LANES]]
        v_all = jnp.concatenate([vp_ref[0, :, o * LANES:(o + 1) * LANES], vc_ref[0, :, o * LANES:(o + 1) * LANES],
                                 vn_ref[0, :, o * LANES:(o + 1) * LANES], vx_ref[0, :, o * LANES:(o + 1) * LANES]],
                                axis=0)
        kx = kx_ref[0, :, o * LANES:(o + 1) * LANES]
        res = []
        for h in (2 * o, 2 * o + 1):
            q = q_ref[0, :, h * LANES:(h + 1) * LANES]
            parts = [jnp.where(masks[d], _dot_nt(q, k_loc[d]) + bias_ref[h, d], NEG_BIG) for d in range(3)]
            parts.append(_dot_nt(q, kx))
            s = jnp.concatenate(parts, axis=1)
            m = jnp.max(s, axis=-1, keepdims=True)
            p = jnp.exp2(s - m)
            l = jnp.sum(p, axis=-1, keepdims=True)
            res.append(_dot(p.astype(BF16), v_all) / l)
        o_ref[0, :, o * LANES:(o + 1) * LANES] = jnp.where(lo_half, res[0], res[1])


def na_attention(q, k, v, kx, vx, bias, rows):
    b, s, _ = q.shape
    nblk = rows // NA_QROWS
    nx = kx.shape[1]
    kern = functools.partial(_na_kernel, rows=rows)
    prev = lambda i, j: (i, jnp.maximum(j - 1, 0), 0)
    cur = lambda i, j: (i, j, 0)
    nxt = lambda i, j: (i, jnp.minimum(j + 1, nblk - 1), 0)
    kv_spec = lambda f: pl.BlockSpec((1, NA_TQ, 2 * LANES), f)
    return pl.pallas_call(
        kern,
        grid=(b, nblk),
        in_specs=[pl.BlockSpec((1, NA_TQ, 4 * LANES), cur),
                  kv_spec(prev), kv_spec(cur), kv_spec(nxt),
                  kv_spec(prev), kv_spec(cur), kv_spec(nxt),
                  pl.BlockSpec((1, nx, 2 * LANES), lambda i, j: (i, 0, 0)),
                  pl.BlockSpec((1, nx, 2 * LANES), lambda i, j: (i, 0, 0)),
                  pl.BlockSpec(bias.shape, lambda i, j: (0, 0, 0, 0))],
        out_specs=pl.BlockSpec((1, NA_TQ, 2 * LANES), cur),
        out_shape=jax.ShapeDtypeStruct((b, s, 2 * LANES), F32),
        compiler_params=_cparams(("parallel", "arbitrary")),
        name="na_attention",
    )(q, k, k, k, v, v, v, kx, vx, bias)


ROUTE_T = 256


def _outproj_kernel(x_ref, ona_ref, omla_ref, od_ref, ogqa_ref, mod_ref, gsub_ref, gffn_ref, lam_ref, mat_ref,
                    wout_ref, wr_ref, xo_ref, h_ref, lg_ref, *, lam_init):
    lam = lam_ref[0, 0]
    m64 = mat_ref[M_SEG64]
    pieces = [ona_ref[0].astype(BF16), omla_ref[0].astype(BF16)]
    dsl = []
    for i in range(2):
        d = od_ref[0, :, i * LANES:(i + 1) * LANES] - lam * od_ref[0, :, (2 + i) * LANES:(3 + i) * LANES]
        d = _seg_norm(d, m64, 1.0 / 64, gsub_ref[...]) * (1.0 - lam_init)
        dsl.append(d.astype(BF16))
    pieces += dsl + [ogqa_ref[0].astype(BF16)]
    o = jnp.concatenate(pieces, axis=1)
    y = _dot(o, wout_ref[...])
    x = x_ref[0] + mod_ref[0, 2:3, :] * y
    xo_ref[0] = x
    ms = jnp.mean(x * x, axis=-1, keepdims=True)
    h = x * lax.rsqrt(ms + NORM_EPS) * gffn_ref[...]
    h = h * (1.0 + mod_ref[0, 4:5, :]) + mod_ref[0, 3:4, :]
    h_ref[0] = h.astype(BF16)
    lg = _dot3_nt(wr_ref[...], h)
    for t in range(lg.shape[1] // ROUTE_T):
        lg_ref[0, t] = lg[:, t * ROUTE_T:(t + 1) * ROUTE_T]


def outproj_mod_router(x, o_na, o_mla, o_diff, o_gqa, mod, g_sub_t, g_ffn, lam, mats, w_out, w_router_t, lam_init):
    b, n, d = x.shape
    tm = min(512, n)
    nt = tm // ROUTE_T
    kern = functools.partial(_outproj_kernel, lam_init=lam_init)
    tok = lambda w: pl.BlockSpec((1, tm, w), lambda i, j: (i, j, 0))
    c2 = lambda i, j: (0, 0)
    return pl.pallas_call(
        kern,
        grid=(b, n // tm),
        in_specs=[tok(d), tok(256), tok(256), tok(512), tok(256),
                  pl.BlockSpec((1, 8, d), lambda i, j: (i, 0, 0)),
                  pl.BlockSpec((1, LANES), c2), pl.BlockSpec((1, d), c2),
                  pl.BlockSpec(memory_space=pltpu.SMEM),
                  pl.BlockSpec(mats.shape, lambda i, j: (0, 0, 0)),
                  pl.BlockSpec((d, d), c2), pl.BlockSpec((N_EXPERTS, d), c2)],
        out_specs=[tok(d), tok(d),
                   pl.BlockSpec((1, nt, N_EXPERTS, ROUTE_T), lambda i, j: (i, j, 0, 0))],
        out_shape=[jax.ShapeDtypeStruct((b, n, d), F32), jax.ShapeDtypeStruct((b, n, d), BF16),
                   jax.ShapeDtypeStruct((b, n // ROUTE_T, N_EXPERTS, ROUTE_T), F32)],
        compiler_params=_cparams(("parallel", "arbitrary")),
        name="outproj_mod_router",
    )(x, o_na, o_mla, o_diff, o_gqa, mod, g_sub_t, g_ffn.reshape(1, d), lam, mats, w_out, w_router_t)


def _route_kernel(lg_ref, tri_ref, aff_ref, pos_ref, off_ref, *, cap):
    nb = lg_ref.shape[1]
    lg = lg_ref[0]
    mx = jnp.max(lg, axis=1, keepdims=True)
    ex = jnp.exp(lg - mx)
    aff = ex / jnp.sum(ex, axis=1, keepdims=True)
    aff_ref[0] = aff
    bits = lax.bitcast_convert_type(aff, jnp.int32)

    def count_ge(t):
        hit = jnp.where(bits >= t[None], 1.0, 0.0)
        return jnp.sum(jnp.sum(hit, axis=0), axis=1, keepdims=True)

    def bis(i, t):
        cand = t | (jnp.int32(1) << (30 - i))
        return jnp.where(count_ge(cand) >= float(cap), cand, t)

    thr = lax.fori_loop(0, 31, bis, jnp.zeros((N_EXPERTS, 1), jnp.int32))
    need = float(cap) - count_ge(thr + 1)
    tri = tri_ref[...]

    def blk(jb, carry):
        c_eq, c_pos = carry
        bb = lax.bitcast_convert_type(aff_ref[0, jb], jnp.int32)
        gt = bb > thr
        eq = bb == thr
        eq_before = _dot(jnp.where(eq, 1.0, 0.0).astype(BF16), tri) + c_eq
        sel = gt | (eq & (eq_before < need))
        sel_f = jnp.where(sel, 1.0, 0.0)
        before = _dot(sel_f.astype(BF16), tri) + c_pos
        pos_ref[0, jb] = jnp.where(sel, before, -1.0).astype(jnp.int32)
        off_ref[0, jb] = jnp.broadcast_to(c_pos, (N_EXPERTS, LANES)).astype(jnp.int32)
        c_eq = c_eq + jnp.sum(jnp.where(eq, 1.0, 0.0), axis=1, keepdims=True)
        c_pos = c_pos + jnp.sum(sel_f, axis=1, keepdims=True)
        return c_eq, c_pos

    zero = jnp.zeros((N_EXPERTS, 1), F32)
    lax.fori_loop(0, nb, blk, (zero, zero))


def route(logits, tri, cap):
    b, nb, e, t = logits.shape
    kern = functools.partial(_route_kernel, cap=cap)
    spec = pl.BlockSpec((1, nb, e, t), lambda i: (i, 0, 0, 0))
    return pl.pallas_call(
        kern,
        grid=(b,),
        in_specs=[spec, pl.BlockSpec(tri.shape, lambda i: (0, 0))],
        out_specs=[spec, spec, pl.BlockSpec((1, nb, e, LANES), lambda i: (i, 0, 0, 0))],
        out_shape=[jax.ShapeDtypeStruct((b, nb, e, t), F32), jax.ShapeDtypeStruct((b, nb, e, t), jnp.int32),
                   jax.ShapeDtypeStruct((b, nb, e, LANES), jnp.int32)],
        compiler_params=_cparams(("arbitrary",)),
        name="route",
    )(logits, tri)


SLOT_ALIGN = 16
SLOT_WIN = 128


def _for_each_slot_window(off_ref, idx0, nsub, st, cap, fn):
    n_extra = min(-(-(ROUTE_T + SLOT_ALIGN) // st), cap // st) - 1
    wins = []
    for jj in range(nsub):
        off, nxt = off_ref[idx0 + jj], off_ref[idx0 + jj + 1]
        start = pl.multiple_of(jnp.minimum(off // SLOT_ALIGN * SLOT_ALIGN, cap - st), SLOT_ALIGN)
        fn(jj, start, None)
        wins.append((start, nxt))
    if n_extra == 0:
        return

    @pl.when(functools.reduce(jnp.logical_or, [nxt > start + st for start, nxt in wins]))
    def _():
        for jj, (start, nxt) in enumerate(wins):
            for w in range(1, n_extra + 1):
                @pl.when(nxt > start + w * st)
                def _():
                    fn(jj, pl.multiple_of(jnp.minimum(start + w * st, cap - st), SLOT_ALIGN), start + w * st)


def _slot_one_hot(slot, prow, start, lower, st):
    hit = slot == (prow - start)
    return hit if lower is None else hit & (prow >= lower) & (prow < lower + st)


GATHER_EXPERTS = 2


def _gather_kernel(off_ref, h_ref, pos_ref, aff_ref, xe_ref, gs_ref, acc_ref, gacc_ref, *, st, cap, nsub, nb):
    bi, eg, ch = pl.program_id(0), pl.program_id(1), pl.program_id(2)

    @pl.when(ch == 0)
    def _():
        acc_ref[...] = jnp.zeros(acc_ref.shape, F32)
        gacc_ref[...] = jnp.zeros(gacc_ref.shape, F32)

    slot = lax.broadcasted_iota(jnp.int32, (st, ROUTE_T), 0)
    for g in range(GATHER_EXPERTS):
        e = eg * GATHER_EXPERTS + g

        def add(jj, start, lower, g=g, e=e):
            hit = _slot_one_hot(slot, pos_ref[0, jj, pl.ds(e, 1), :], start, lower, st)
            acc_ref[g, pl.ds(start, st), :] += _dot(jnp.where(hit, 1.0, 0.0).astype(BF16),
                                                    h_ref[0, jj * ROUTE_T:(jj + 1) * ROUTE_T, :])
            gacc_ref[g, pl.ds(start, st), :] += jnp.sum(jnp.where(hit, aff_ref[0, jj, pl.ds(e, 1), :], 0.0),
                                                        axis=1, keepdims=True)

        _for_each_slot_window(off_ref, (bi * N_EXPERTS + e) * (nb + 1) + ch * nsub, nsub, st, cap, add)

    @pl.when(ch == pl.num_programs(2) - 1)
    def _():
        xe_ref[0] = acc_ref[...].astype(BF16)
        gs_ref[0] = gacc_ref[...]


def moe_gather(offs, h, pos, aff, cap):
    b, n, d = h.shape
    nb = n // ROUTE_T
    st = min(SLOT_WIN, cap)
    chunk = min(2048, n)
    nsub = chunk // ROUTE_T
    kern = functools.partial(_gather_kernel, st=st, cap=cap, nsub=nsub, nb=nb)
    rspec = pl.BlockSpec((1, nsub, N_EXPERTS, ROUTE_T), lambda i, e, c, off: (i, c, 0, 0))
    return pl.pallas_call(
        kern,
        grid_spec=pltpu.PrefetchScalarGridSpec(
            num_scalar_prefetch=1,
            grid=(b, N_EXPERTS // GATHER_EXPERTS, n // chunk),
            in_specs=[pl.BlockSpec((1, chunk, d), lambda i, e, c, off: (i, c, 0)), rspec, rspec],
            out_specs=[pl.BlockSpec((1, GATHER_EXPERTS, cap, d), lambda i, e, c, off: (i, e, 0, 0)),
                       pl.BlockSpec((1, GATHER_EXPERTS, cap, 1), lambda i, e, c, off: (i, e, 0, 0))],
            scratch_shapes=[pltpu.VMEM((GATHER_EXPERTS, cap, d), F32), pltpu.VMEM((GATHER_EXPERTS, cap, 1), F32)]),
        out_shape=[jax.ShapeDtypeStruct((b, N_EXPERTS, cap, d), BF16),
                   jax.ShapeDtypeStruct((b, N_EXPERTS, cap, 1), F32)],
        compiler_params=_cparams(("parallel", "parallel", "arbitrary")),
        name="moe_gather",
    )(offs, h, pos, aff)


FF_TILE = 256


def _ffn_kernel(x_ref, g_ref, wg_ref, wu_ref, wd_ref, y_ref, acc_ref):
    f = pl.program_id(1)
    n_batch = x_ref.shape[0]

    @pl.when(f == 0)
    def _():
        acc_ref[...] = jnp.zeros(acc_ref.shape, F32)

    wg, wu, wd = wg_ref[0, 0].astype(BF16), wu_ref[0, 0].astype(BF16), wd_ref[0, 0].astype(BF16)
    for bi in range(n_batch):
        x = x_ref[bi, 0]
        hmid = (_silu(_dot(x, wg)) * _dot(x, wu)).astype(BF16)
        acc_ref[bi] += _dot(hmid, wd)

    @pl.when(f == pl.num_programs(1) - 1)
    def _():
        for bi in range(n_batch):
            y_ref[bi, 0] = (acc_ref[bi] * g_ref[bi, 0]).astype(BF16)


def moe_ffn(xe, gs, w_gate, w_up, w_down, layer):
    b, e, cap, d = xe.shape
    dff = w_gate.shape[3]
    once = pl.Buffered(1)
    return pl.pallas_call(
        _ffn_kernel,
        grid=(e, dff // FF_TILE),
        in_specs=[pl.BlockSpec((b, 1, cap, d), lambda ei, f: (0, ei, 0, 0), pipeline_mode=once),
                  pl.BlockSpec((b, 1, cap, 1), lambda ei, f: (0, ei, 0, 0), pipeline_mode=once),
                  pl.BlockSpec((1, 1, d, FF_TILE), lambda ei, f: (layer, ei, 0, f)),
                  pl.BlockSpec((1, 1, d, FF_TILE), lambda ei, f: (layer, ei, 0, f)),
                  pl.BlockSpec((1, 1, FF_TILE, d), lambda ei, f: (layer, ei, f, 0))],
        out_specs=pl.BlockSpec((b, 1, cap, d), lambda ei, f: (0, ei, 0, 0), pipeline_mode=once),
        out_shape=jax.ShapeDtypeStruct((b, e, cap, d), BF16),
        scratch_shapes=[pltpu.VMEM((b, cap, d), F32)],
        compiler_params=_cparams(("parallel", "arbitrary")),
        name="moe_ffn",
    )(xe, gs, w_gate, w_up, w_down)


def _combine_kernel(off_ref, x_ref, ye_ref, pos_ref, mod_ref, o_ref, *, st, cap, nsub, nb):
    bi, ch, e = pl.program_id(0), pl.program_id(1), pl.program_id(2)

    @pl.when(e == 0)
    def _():
        o_ref[...] = jnp.zeros(o_ref.shape, F32)

    slot = lax.broadcasted_iota(jnp.int32, (st, ROUTE_T), 0)

    def add(jj, start, lower):
        hit = _slot_one_hot(slot, pos_ref[0, jj, pl.ds(e, 1), :], start, lower, st)
        oh = jnp.where(hit, 1.0, 0.0).astype(BF16)
        o_ref[0, jj * ROUTE_T:(jj + 1) * ROUTE_T, :] += lax.dot_general(
            oh, ye_ref[0, 0, pl.ds(start, st), :], (((0,), (0,)), ((), ())), preferred_element_type=F32)

    _for_each_slot_window(off_ref, (bi * N_EXPERTS + e) * (nb + 1) + ch * nsub, nsub, st, cap, add)

    @pl.when(e == pl.num_programs(2) - 1)
    def _():
        o_ref[0] = x_ref[0] + mod_ref[0, 5:6, :] * o_ref[0]


def moe_combine(offs, x, ye, pos, mod, cap, slot0):
    b, n, d = x.shape
    nb = n // ROUTE_T
    st = min(SLOT_WIN, cap)
    chunk = min(2048, n)
    nsub = chunk // ROUTE_T
    assert slot0 % cap == 0
    kern = functools.partial(_combine_kernel, st=st, cap=cap, nsub=nsub, nb=nb)
    return pl.pallas_call(
        kern,
        grid_spec=pltpu.PrefetchScalarGridSpec(
            num_scalar_prefetch=1,
            grid=(b, n // chunk, N_EXPERTS),
            in_specs=[pl.BlockSpec((1, chunk, d), lambda i, c, e, off: (i, c, 0)),
                      pl.BlockSpec((1, 1, cap, d), lambda i, c, e, off: (i, e, slot0 // cap, 0)),
                      pl.BlockSpec((1, nsub, N_EXPERTS, ROUTE_T), lambda i, c, e, off: (i, c, 0, 0)),
                      pl.BlockSpec((1, 8, d), lambda i, c, e, off: (i, 0, 0))],
            out_specs=pl.BlockSpec((1, chunk, d), lambda i, c, e, off: (i, c, 0))),
        out_shape=jax.ShapeDtypeStruct((b, n, d), F32),
        compiler_params=_cparams(("parallel", "parallel", "arbitrary")),
        name="moe_combine",
    )(offs, x, ye, pos, mod)


def expert_choice_ffn(sets, tri, w_gate, w_up, w_down, layer):
    routed = []
    for x, h, logits, mod in sets:
        b, n, _ = x.shape
        cap = EC_CAPACITY * n // N_EXPERTS
        aff, pos, off = route(logits, tri, cap)
        first = jnp.transpose(off[..., 0], (0, 2, 1))
        offs = jnp.concatenate([first, jnp.full((b, N_EXPERTS, 1), cap, jnp.int32)], axis=2).reshape(-1)
        xe, gs = moe_gather(offs, h, pos, aff, cap)
        routed.append((offs, pos, cap, xe, gs))
    xe_all = jnp.concatenate([r[3] for r in routed], axis=2) if len(routed) > 1 else routed[0][3]
    gs_all = jnp.concatenate([r[4] for r in routed], axis=2) if len(routed) > 1 else routed[0][4]
    ye = moe_ffn(xe_all, gs_all, w_gate, w_up, w_down, layer)
    outs, slot0 = [], 0
    for (x, _, _, mod), (offs, pos, cap, _, _) in zip(sets, routed):
        outs.append(moe_combine(offs, x, ye, pos, mod, cap, slot0))
        slot0 += cap
    return outs


def _rope_tables(n_rows_grid):
    s = n_rows_grid * GRID_W
    t = np.arange(s)
    row, col = (t // GRID_W).astype(np.float64), (t % GRID_W).astype(np.float64)

    def unit(n):
        half = n // 2
        inv = ROPE_THETA ** (-np.arange(0, n, 2, dtype=np.float64) / n)
        out = []
        for pos in (row, col):
            ang = pos[:, None] * inv[None, :]
            c, sn = np.cos(ang), np.sin(ang)
            z = np.zeros_like(sn)
            out.append((np.concatenate([c, c], 1), np.concatenate([-sn, z], 1), np.concatenate([z, sn], 1)))
        return [np.concatenate([out[0][i], out[1][i]], 1) for i in range(3)]

    ident = lambda w: (np.ones((s, w), np.float32), np.zeros((s, w), np.float32), np.zeros((s, w), np.float32))
    u16 = unit(16)
    u32 = unit(32)
    tabs = []
    idt = ident(64)
    idt32 = ident(32)
    for i in range(3):
        tabs.append(np.concatenate([idt[i], u16[i], idt32[i]], 1))
    for i in range(3):
        tabs.append(np.concatenate([u16[i]] * 4, 1))
    for i in range(3):
        tabs.append(np.concatenate([u32[i]] * 2, 1))
    return np.stack(tabs).astype(np.float32)


def _identity_tables(n):
    one, zero = np.ones((n, LANES), np.float32), np.zeros((n, LANES), np.float32)
    return np.stack([one, zero, zero] * 3)


def _seg_matrices():
    i = np.arange(LANES)
    ones = np.ones((LANES, LANES), np.float32)
    m64 = (i[:, None] // 64 == i[None, :] // 64).astype(np.float32)
    m32 = (i[:, None] // 32 == i[None, :] // 32).astype(np.float32)
    return np.stack([ones, m64, m32])


def _na_bias(rpb):
    c = np.arange(GRID_W)
    rl = np.arange(NA_QROWS)
    dc = np.clip(c[None, :] - c[:, None] + NA_WIN_W - 1, 0, 2 * NA_WIN_W - 2)
    d = np.array([-1, 0, 1])
    dr = np.clip(NA_QROWS * d[:, None, None] + rl[None, None, :] - rl[None, :, None] + NA_WIN_H - 1,
                 0, 2 * NA_WIN_H - 2)
    cols = jnp.take(rpb.astype(F32) * LOG2E, jnp.asarray(dc), axis=2)
    full = jnp.take(cols, jnp.asarray(dr), axis=1)
    return jnp.transpose(full, (0, 1, 2, 4, 3, 5)).reshape(rpb.shape[0], 3, NA_TQ, NA_TQ)


def _pad_cols(w, width):
    return jnp.pad(w, ((0, 0), (0, width - w.shape[1])))


def _layer_params(l, w_in, g_na_q, g_na_k, na_rpb, g_mla_cq, w_mla_uq, g_mla_q, g_mla_ckv, w_mla_ukv,
                  g_mla_k_nope, g_mla_k_rope, g_diff_q, g_diff_k, g_diff_sub, g_gqa_q, g_gqa_k, w_out):
    wi = w_in[l]
    (naq, nak, nav, cq, ckv, kr, dq, dk, dv, gq, gk, gv) = jnp.split(
        wi, np.cumsum([256, 256, 256, 256, 128, 32, 256, 256, 256, 256, 128])[:], axis=1)
    gq4 = gq.reshape(-1, 4, 64)[:, jnp.array([0, 2, 1, 3])].reshape(-1, 256)
    zeros = lambda w: jnp.zeros((wi.shape[0], w), wi.dtype)
    w_in_r = jnp.concatenate([naq, nak, nav, cq, ckv, dq, dk, dv, gq4, gk, gv, zeros(64), kr, zeros(32)],
                             axis=1).astype(BF16)
    uq = w_mla_uq[l].reshape(MLA_Q_RANK, 4, 96)
    wuq = jnp.pad(uq, ((0, 0), (0, 0), (0, 32))).reshape(MLA_Q_RANK, 512).astype(BF16)
    ukv = w_mla_ukv[l].reshape(MLA_KV_RANK, 4, 128)
    wuk = jnp.pad(ukv[:, :, :64], ((0, 0), (0, 0), (0, 64))).reshape(MLA_KV_RANK, 512).astype(BF16)
    wuv = ukv[:, :, 64:].reshape(MLA_KV_RANK, 256).astype(BF16)
    row = lambda v: jnp.pad(v, (0, 512 - v.shape[0]))
    z32, z64 = jnp.zeros((32,), F32), jnp.zeros((64,), F32)
    gains = jnp.stack([
        row(jnp.tile(g_na_q[l], 4)), row(jnp.tile(g_na_k[l], 4)), row(g_mla_cq[l]),
        row(jnp.tile(jnp.concatenate([g_mla_q[l], z32]), 4)), row(g_mla_ckv[l]),
        row(jnp.tile(jnp.concatenate([g_mla_k_nope[l], z64]), 4)),
        row(jnp.concatenate([z64, g_mla_k_rope[l], z32])),
        row(jnp.tile(g_diff_q[l].reshape(-1), 4)), row(jnp.tile(g_diff_k[l].reshape(-1), 4)),
        row(jnp.tile(g_gqa_q[l], 4)), row(jnp.tile(g_gqa_k[l], 2))] + [jnp.zeros((512,), F32)] * 5)
    wo = w_out[l]
    wo_g = wo[768:].reshape(4, 64, -1)[jnp.array([0, 2, 1, 3])].reshape(256, -1)
    w_out_r = jnp.concatenate([wo[:768], wo_g], axis=0).astype(BF16)
    return dict(w_in=w_in_r, wuq=wuq, wuk=wuk, wuv=wuv, gains=gains, w_out=w_out_r,
                bias=_na_bias(na_rpb[l]), g_sub=jnp.tile(g_diff_sub[l], 2).reshape(1, LANES))


def kernel(x, c, ctx, c_ctx, w_mod, b_mod, g_attn, g_ffn, w_in, g_na_q, g_na_k, na_rpb, g_mla_cq, w_mla_uq, g_mla_q, g_mla_ckv, w_mla_ukv, g_mla_k_nope, g_mla_k_rope, g_diff_q, g_diff_k, diff_lambda, g_diff_sub, g_gqa_q, g_gqa_k, w_out, w_router, w_gate, w_up, w_down):
    b, s, d = x.shape
    n_ctx = ctx.shape[1]
    rows = s // GRID_W
    tabs = jnp.asarray(_rope_tables(rows))
    tabs_ctx = jnp.asarray(_identity_tables(n_ctx))
    mats = jnp.asarray(_seg_matrices()).astype(BF16)
    tri = jnp.asarray(np.triu(np.ones((ROUTE_T, ROUTE_T), np.float32), 1)).astype(BF16)
    c_rows = jnp.concatenate([c, c_ctx[None], jnp.zeros((8 - b - 1, d), F32)], axis=0)
    xc = ctx
    for l in range(DEPTH):
        need_ctx = l < DEPTH - 1
        lam_init = 0.8 - 0.6 * math.exp(-0.3 * l)
        lp = diff_lambda[l].astype(F32)
        lam = (jnp.exp(jnp.sum(lp[0] * lp[1])) - jnp.exp(jnp.sum(lp[2] * lp[3])) + lam_init).reshape(1, 1)
        prm = _layer_params(l, w_in, g_na_q, g_na_k, na_rpb, g_mla_cq, w_mla_uq, g_mla_q, g_mla_ckv, w_mla_ukv,
                            g_mla_k_nope, g_mla_k_rope, g_diff_q, g_diff_k, g_diff_sub, g_gqa_q, g_gqa_k, w_out)
        modv = mod_vectors(c_rows, w_mod[l], b_mod[l]).reshape(8, N_MOD, d)
        mod = jnp.pad(modv[:b], ((0, 0), (0, 2), (0, 0)))
        mod_c = jnp.broadcast_to(jnp.pad(modv[b:b + 1], ((0, 0), (0, 2), (0, 0))), (b, 8, d))
        w_router_t = w_router[l].T

        def mix_inputs(xin, m, tb):
            return proj_prep(xin, g_attn[l], m[:, 0:1], m[:, 1:2], prm["w_in"], tb, prm["gains"], mats,
                             prm["wuq"], prm["wuk"], prm["wuv"])

        (naq, nak, nav, mq, mk, mv, dq, dk, dv, gq, gk, gv, stat) = mix_inputs(x, mod, tabs)
        (naq_c, nak_c, nav_c, mq_c, mk_c, mv_c, dq_c, dk_c, dv_c, gq_c, gk_c, gv_c, stat_c) = mix_inputs(
            xc, mod_c, tabs_ctx)
        cat = lambda a, bb: jnp.concatenate([a, bb], axis=1)
        n2 = jnp.max(stat, axis=(1, 3))
        n2c = jnp.max(stat_c, axis=(1, 3))
        q2 = lambda name: n2[:, STAT_ROWS.index(name)]
        k2 = lambda name: jnp.maximum(n2[:, STAT_ROWS.index(name)], n2c[:, STAT_ROWS.index(name)])

        ext = values_transposed
        o_na = na_attention(naq, nak, nav, nak_c, nav_c, prm["bias"], rows)
        o_mla = attention(q2("mq"), k2("mk"), k2("mv"), mq, cat(mk_c, mk), ext(cat(mv_c, mv)), HEADS_MLA)
        o_diff = attention(q2("dq"), k2("dk"), k2("dv"), dq, cat(dk_c, dk), ext(cat(dv_c, dv)), HEADS_DIFF)
        o_gqa = attention(q2("gq"), k2("gk"), k2("gv"), gq, cat(gk_c, gk), ext(cat(gv_c, gv)), HEADS_GQA)
        x_mid, h2, logits = outproj_mod_router(x, o_na, o_mla, o_diff, o_gqa, mod, prm["g_sub"], g_ffn[l], lam,
                                               mats, prm["w_out"], w_router_t, lam_init)
        if need_ctx:
            oc_na = flash_attention(naq_c, nak_c, ext(nav_c), HEADS_PAIRED)
            oc_mla = flash_attention(mq_c, mk_c, ext(mv_c), HEADS_MLA)
            oc_diff = flash_attention(dq_c, dk_c, ext(dv_c), HEADS_DIFF)
            oc_gqa = flash_attention(gq_c, gk_c, ext(gv_c), HEADS_GQA)
            xc_mid, hc2, logits_c = outproj_mod_router(xc, oc_na, oc_mla, oc_diff, oc_gqa, mod_c, prm["g_sub"],
                                                       g_ffn[l], lam, mats, prm["w_out"], w_router_t, lam_init)
            x, xc = expert_choice_ffn([(x_mid, h2, logits, mod), (xc_mid, hc2, logits_c, mod_c)],
                                      tri, w_gate, w_up, w_down, l)
        else:
            (x,) = expert_choice_ffn([(x_mid, h2, logits, mod)], tri, w_gate, w_up, w_down, l)
    return x
```
